```python
import math
import jax, jax.numpy as jnp
from jax import lax
import numpy as np

D_MODEL = 1024
BATCH = 8
SEQ = 2048
DEPTH = 2

HEAD_DIM = 64
ROPE_THETA = 10000.0
A_Q_HEADS = 8
A_KV_HEADS = 2
A_WINDOW = 128
A_BLOCK = 128
B_Q_HEADS = 8
B_Q_RANK = 256
B_IDX_HEADS = 8
B_IDX_DIM = HEAD_DIM
B_TOPK_MAX = 256
B_QBLOCK = 128
C_HEADS = 16
C_BLOCK = 256
C_TOPK = 3
C_QCHUNK = 16
D_FF = 4 * D_MODEL
N_EVEN = (DEPTH + 1) // 2
N_ODD = DEPTH // 2
DN_ALPHA = (2 * DEPTH) ** 0.25
DN_BETA = (8 * DEPTH) ** -0.25
LN_EPS = 1e-5
RMS_EPS = 1e-6

AB_WIDTHS = (A_Q_HEADS * HEAD_DIM, A_KV_HEADS * HEAD_DIM, A_KV_HEADS * HEAD_DIM,
             B_Q_RANK, HEAD_DIM, HEAD_DIM, B_IDX_DIM, B_IDX_HEADS)
AB_IN_WIDTH = sum(AB_WIDTHS)
AB_MIX_WIDTH = (A_Q_HEADS + B_Q_HEADS) * HEAD_DIM
C_WIDTH = C_HEADS * HEAD_DIM

kernel_name = "hybrid_swa_dsa_moba_deepnorm_adaln"


def _split_widths(t, widths):
    return jnp.split(t, np.cumsum(widths)[:-1].tolist(), axis=-1)


def layer_norm(x, g, b):
    xf = x.astype(jnp.float32)
    mu = jnp.mean(xf, axis=-1, keepdims=True)
    var = jnp.mean(jnp.square(xf - mu), axis=-1, keepdims=True)
    return ((xf - mu) * lax.rsqrt(var + LN_EPS) * g.astype(jnp.float32) + b.astype(jnp.float32)).astype(x.dtype)


def rms_norm(x, g):
    xf = x.astype(jnp.float32)
    ms = jnp.mean(jnp.square(xf), axis=-1, keepdims=True)
    return (xf * lax.rsqrt(ms + RMS_EPS) * g.astype(jnp.float32)).astype(x.dtype)


def rope_tables(positions):
    inv = ROPE_THETA ** (-jnp.arange(0, HEAD_DIM, 2, dtype=jnp.float32) / HEAD_DIM)
    ang = positions.astype(jnp.float32)[..., None] * inv
    ang = jnp.concatenate([ang, ang], axis=-1)
    return jnp.cos(ang)[:, :, None, :], jnp.sin(ang)[:, :, None, :]


def apply_rope(t, cos, sin):
    t1, t2 = jnp.split(t, 2, axis=-1)
    rot = jnp.concatenate([-t2, t1], axis=-1)
    return (t * cos + rot * sin).astype(t.dtype)


def ada_modulation(c, w, b):
    m = jax.nn.silu(c) @ w + b
    shift, scale, gate = jnp.split(m[:, None, :], 3, axis=-1)
    return shift, scale, 1.0 + gate


def swa_sink_attention(q, k, v, sinks):
    Bn, S, HQ, hd = q.shape
    HKV = k.shape[2]
    G = HQ // HKV
    nb = S // A_BLOCK
    qb = q.reshape(Bn, nb, A_BLOCK, HKV, G, hd)

    def band(t):
        tb = t.reshape(Bn, nb, A_BLOCK, HKV, hd)
        prev = jnp.pad(tb[:, :-1], ((0, 0), (1, 0), (0, 0), (0, 0), (0, 0)))
        return jnp.concatenate([prev, tb], axis=2)

    kb, vb = band(k), band(v)
    s = jnp.einsum('bnqkgd,bnskd->bnkgqs', qb, kb, preferred_element_type=jnp.float32) * (hd ** -0.5)
    qpos = jnp.arange(nb)[:, None, None] * A_BLOCK + jnp.arange(A_BLOCK)[None, :, None]
    kpos = jnp.arange(nb)[:, None, None] * A_BLOCK - A_BLOCK + jnp.arange(2 * A_BLOCK)[None, None, :]
    diff = qpos - kpos
    mask = (diff >= 0) & (diff < A_WINDOW) & (kpos >= 0)
    s = jnp.where(mask[None, :, None, None], s, -jnp.inf)
    sink = jnp.broadcast_to(sinks.astype(jnp.float32).reshape(HKV, G)[None, None, :, :, None, None],
                            s.shape[:-1] + (1,))
    p = jax.nn.softmax(jnp.concatenate([s, sink], axis=-1), axis=-1)[..., :-1]
    o = jnp.einsum('bnkgqs,bnskd->bnqkgd', p.astype(vb.dtype), vb)
    return o.reshape(Bn, S, HQ * hd)


def dsa_attention(q, k, v, iq, ik, iw):
    Bn, S, H, hd = q.shape
    n_top = min(B_TOPK_MAX, S // 4)
    nqb = S // B_QBLOCK

    def to_blocks(t):
        return jnp.moveaxis(t.reshape((Bn, nqb, B_QBLOCK) + t.shape[2:]), 1, 0)

    kpos = jnp.arange(S)
    bidx = jnp.arange(Bn)[:, None, None]
    scale = hd ** -0.5

    def one_block(args):
        i, qb, iqb, iwb = args
        tpos = i * B_QBLOCK + jnp.arange(B_QBLOCK)
        raw = jnp.einsum('bqhd,bsd->bhqs', iqb, ik, preferred_element_type=jnp.float32)
        score = jnp.einsum('bhqs,bqh->bqs', jax.nn.relu(raw), iwb.astype(jnp.float32))
        causal = kpos[None, :] <= tpos[:, None]
        score = jnp.where(causal[None], score, -jnp.inf)
        _, idx = lax.top_k(score, n_top)
        valid = idx <= tpos[None, :, None]
        kg = k[bidx, idx]
        vg = v[bidx, idx]
        s = jnp.einsum('bqhd,bqkd->bhqk', qb, kg, preferred_element_type=jnp.float32) * scale
        s = jnp.where(valid[:, None], s, -jnp.inf)
        p = jax.nn.softmax(s, axis=-1)
        o = jnp.einsum('bhqk,bqkd->bqhd', p.astype(vg.dtype), vg)
        return o.reshape(Bn, B_QBLOCK, H * hd)

    out = lax.map(one_block, (jnp.arange(nqb), to_blocks(q), to_blocks(iq), to_blocks(iw)))
    return jnp.moveaxis(out, 0, 1).reshape(Bn, S, H * hd)


def moba_attention(q, k, v):
    Bn, S, H, hd = q.shape
    nb = -(-S // C_BLOCK)
    Sp = nb * C_BLOCK
    pad = ((0, 0), (0, Sp - S), (0, 0), (0, 0))
    kb = jnp.pad(k, pad).reshape(Bn, nb, C_BLOCK, H, hd).transpose(0, 3, 1, 2, 4)
    vb = jnp.pad(v, pad).reshape(Bn, nb, C_BLOCK, H, hd).transpose(0, 3, 1, 2, 4)
    kmean = jnp.mean(kb.astype(jnp.float32), axis=3)
    n_sel = min(C_TOPK, nb - 1)
    nqc = S // C_QCHUNK
    qc = q.reshape(Bn, nqc, C_QCHUNK, H, hd).transpose(1, 0, 3, 2, 4)
    bi = jnp.arange(Bn)[:, None, None, None]
    hi = jnp.arange(H)[None, :, None, None]
    scale = hd ** -0.5

    def one_chunk(args):
        i, qblk = args
        tpos = i * C_QCHUNK + jnp.arange(C_QCHUNK)
        own = tpos // C_BLOCK
        own_idx = jnp.broadcast_to(own[None, None, :, None], (Bn, H, C_QCHUNK, 1))
        own_ok = jnp.ones((Bn, H, C_QCHUNK, 1), dtype=bool)
        if n_sel > 0:
            gate = jnp.einsum('bhqd,bhnd->bhqn', qblk, kmean, preferred_element_type=jnp.float32)
            past = jnp.arange(nb)[None, :] < own[:, None]
            gate = jnp.where(past[None, None], gate, -jnp.inf)
            _, sel = lax.top_k(gate, n_sel)
            sel_ok = sel < own[None, None, :, None]
            blk = jnp.concatenate([sel, own_idx], axis=-1)
            blk_ok = jnp.concatenate([sel_ok, own_ok], axis=-1)
        else:
            blk, blk_ok = own_idx, own_ok
        kg = kb[bi, hi, blk]
        vg = vb[bi, hi, blk]
        s = jnp.einsum('bhqd,bhqnjd->bhqnj', qblk, kg, preferred_element_type=jnp.float32) * scale
        kpos = blk[..., None] * C_BLOCK + jnp.arange(C_BLOCK)
        ok = blk_ok[..., None] & (kpos <= tpos[None, None, :, None, None])
        s = jnp.where(ok, s, -jnp.inf)
        p = jax.nn.softmax(s.reshape(s.shape[:3] + (-1,)), axis=-1).reshape(s.shape)
        o = jnp.einsum('bhqnj,bhqnjd->bqhd', p.astype(vg.dtype), vg)
        return o.reshape(Bn, C_QCHUNK, H * hd)

    out = lax.map(one_chunk, (jnp.arange(nqc), qc))
    return jnp.moveaxis(out, 0, 1).reshape(Bn, S, H * hd)


def mixer_ab(h, cos, sin, w_in, q_norm, w_uq, w_uiq, sinks, w_out):
    Bn, S, _ = h.shape
    aq, ak, av, cq, bk, bv, bik, biw = _split_widths(h @ w_in, AB_WIDTHS)
    aq = apply_rope(aq.reshape(Bn, S, A_Q_HEADS, HEAD_DIM), cos, sin)
    ak = apply_rope(ak.reshape(Bn, S, A_KV_HEADS, HEAD_DIM), cos, sin)
    av = av.reshape(Bn, S, A_KV_HEADS, HEAD_DIM)
    ya = swa_sink_attention(aq, ak, av, sinks)
    cq = rms_norm(cq, q_norm)
    bq = apply_rope((cq @ w_uq).reshape(Bn, S, B_Q_HEADS, HEAD_DIM), cos, sin)
    biq = apply_rope((cq @ w_uiq).reshape(Bn, S, B_IDX_HEADS, B_IDX_DIM), cos, sin)
    bk = apply_rope(bk[:, :, None, :], cos, sin)[:, :, 0]
    bik = apply_rope(bik[:, :, None, :], cos, sin)[:, :, 0]
    iw = biw * (B_IDX_HEADS ** -0.5 * B_IDX_DIM ** -0.5)
    yb = dsa_attention(bq, bk, bv, biq, bik, iw)
    return jnp.concatenate([ya, yb], axis=-1) @ w_out


def mixer_c(h, cos, sin, w_in, w_out):
    Bn, S, _ = h.shape
    q, k, v = jnp.split(h @ w_in, 3, axis=-1)
    q = apply_rope(q.reshape(Bn, S, C_HEADS, HEAD_DIM), cos, sin)
    k = apply_rope(k.reshape(Bn, S, C_HEADS, HEAD_DIM), cos, sin)
    v = v.reshape(Bn, S, C_HEADS, HEAD_DIM)
    return moba_attention(q, k, v) @ w_out


def squared_relu_mlp(h, w1, w2):
    return jnp.square(jax.nn.relu(h @ w1)) @ w2


def setup_inputs(seed: int = 0) -> dict:
    key = jax.random.key(seed)
    ks = jax.random.split(key, 20)
    f32 = jnp.float32
    n = lambda k, shp, s: jax.random.normal(k, shp, f32) * s
    D = D_MODEL
    return {
        "x": n(ks[0], (BATCH, SEQ, D), 1.0),
        "c": n(ks[1], (BATCH, D), 1.0),
        "positions": jnp.broadcast_to(jnp.arange(SEQ, dtype=jnp.int32)[None, :], (BATCH, SEQ)),
        "ab_w_in": n(ks[2], (N_EVEN, D, AB_IN_WIDTH), D ** -0.5),
        "ab_q_norm": 1.0 + n(ks[3], (N_EVEN, B_Q_RANK), 0.02),
        "ab_w_uq": n(ks[4], (N_EVEN, B_Q_RANK, B_Q_HEADS * HEAD_DIM), B_Q_RANK ** -0.5),
        "ab_w_uiq": n(ks[5], (N_EVEN, B_Q_RANK, B_IDX_HEADS * B_IDX_DIM), B_Q_RANK ** -0.5),
        "ab_sinks": n(ks[6], (N_EVEN, A_Q_HEADS), 1.0),
        "ab_w_out": n(ks[7], (N_EVEN, AB_MIX_WIDTH, D), AB_MIX_WIDTH ** -0.5 * DN_BETA),
        "c_w_in": n(ks[8], (N_ODD, D, 3 * C_WIDTH), D ** -0.5),
        "c_w_out": n(ks[9], (N_ODD, C_WIDTH, D), C_WIDTH ** -0.5 * DN_BETA),
        "ada_w": n(ks[10], (DEPTH, 2, D, 3 * D), 0.1 * D ** -0.5),
        "ada_b": n(ks[11], (DEPTH, 2, 3 * D), 0.02),
        "ln_g": 1.0 + n(ks[12], (DEPTH, 2, D), 0.02),
        "ln_b": n(ks[13], (DEPTH, 2, D), 0.02),
        "mlp_w1": n(ks[14], (DEPTH, D, D_FF), D ** -0.5),
        "mlp_w2": n(ks[15], (DEPTH, D_FF, D), D_FF ** -0.5 * DN_BETA),
    }


def reference(x, c, positions, ab_w_in, ab_q_norm, ab_w_uq, ab_w_uiq, ab_sinks, ab_w_out,
              c_w_in, c_w_out, ada_w, ada_b, ln_g, ln_b, mlp_w1, mlp_w2):
    cos, sin = rope_tables(positions)
    for layer in range(DEPTH):
        shift, scale, gate = ada_modulation(c, ada_w[layer, 0], ada_b[layer, 0])
        h = x * (1.0 + scale) + shift
        if layer % 2 == 0:
            e = layer // 2
            y = mixer_ab(h, cos, sin, ab_w_in[e], ab_q_norm[e], ab_w_uq[e], ab_w_uiq[e],
                         ab_sinks[e], ab_w_out[e])
        else:
            o = layer // 2
            y = mixer_c(h, cos, sin, c_w_in[o], c_w_out[o])
        x = layer_norm(DN_ALPHA * x + gate * y, ln_g[layer, 0], ln_b[layer, 0])
        shift, scale, gate = ada_modulation(c, ada_w[layer, 1], ada_b[layer, 1])
        h = x * (1.0 + scale) + shift
        y = squared_relu_mlp(h, mlp_w1[layer], mlp_w2[layer])
        x = layer_norm(DN_ALPHA * x + gate * y, ln_g[layer, 1], ln_b[layer, 1])
    return x
```

```python
import functools

import jax
import jax.numpy as jnp
from jax import lax
from jax.experimental import pallas as pl
from jax.experimental.pallas import tpu as pltpu

D_MODEL = 1024
HEAD_DIM = 64
ROPE_THETA = 10000.0
DEPTH = 2
A_Q_HEADS = 8
A_KV_HEADS = 2
A_BLOCK = 128
B_Q_HEADS = 8
B_Q_RANK = 256
B_IDX_HEADS = 8
B_TOPK = 256
B_QBLOCK = 128
C_HEADS = 16
C_BLOCK = 256
C_TOPK = 3
D_FF = 4 * D_MODEL
DN_ALPHA = (2 * DEPTH) ** 0.25
LN_EPS = 1e-5
RMS_EPS = 1e-6
AB_IN_WIDTH = 1224
AB_IN_PAD = 1280
ATT_SCALE = HEAD_DIM ** -0.5
IDX_SCALE = B_IDX_HEADS ** -0.5 * HEAD_DIM ** -0.5

F32 = jnp.float32
BF16 = jnp.bfloat16
NEG_INF = float("-inf")
INT_MIN = -(2 ** 31)
VMEM_LIMIT = 48 * 1024 * 1024

_NT = (((1,), (1,)), ((), ()))


def _params(sem):
    return pltpu.CompilerParams(dimension_semantics=sem, vmem_limit_bytes=VMEM_LIMIT)


def _rope(t, cos, sin_a, sin_b):
    w = t.shape[-1]
    return t * cos + pltpu.roll(t, 32, 1) * sin_a + pltpu.roll(t, w - 32, 1) * sin_b


def _tile_lanes(t, n):
    return t if n == 1 else jnp.concatenate([t] * n, axis=-1)


def _layer_norm(z, g, b):
    mu = jnp.mean(z, axis=-1, keepdims=True)
    zc = z - mu
    var = jnp.mean(zc * zc, axis=-1, keepdims=True)
    return zc * lax.rsqrt(var + LN_EPS) * g + b


def _ada_kernel(c_ref, w_ref, b_ref, o_ref):
    c = c_ref[...]
    sc = c / (1.0 + jnp.exp(-c))
    o_ref[0] = jnp.dot(sc, w_ref[0], preferred_element_type=F32) + b_ref[0]


def _ada_modulation(c, ada_w, ada_b):
    n = ada_w.shape[0] * ada_w.shape[1]
    bsz, d = c.shape
    w = ada_w.reshape(n, d, 3 * d)
    b = ada_b.reshape(n, 1, 3 * d)
    tn = 1024
    return pl.pallas_call(
        _ada_kernel,
        grid=(n, 3 * d // tn),
        in_specs=[
            pl.BlockSpec((bsz, d), lambda i, j: (0, 0)),
            pl.BlockSpec((1, d, tn), lambda i, j: (i, 0, j)),
            pl.BlockSpec((1, 1, tn), lambda i, j: (i, 0, j)),
        ],
        out_specs=pl.BlockSpec((1, bsz, tn), lambda i, j: (i, 0, j)),
        out_shape=jax.ShapeDtypeStruct((n, bsz, 3 * d), F32),
        compiler_params=_params(("arbitrary", "arbitrary")),
        name="ada_modulation",
    )(c, w, b)


def _proj_ab_kernel(x_ref, shift_ref, scale_ref, cos_ref, sa_ref, sb_ref, win_ref, qn_ref, wuq_ref, wuiq_ref,
                    aq_ref, ak_ref, av_ref, bq_ref, biq_ref, misc_ref, iw_ref):
    h = x_ref[0] * (1.0 + scale_ref[0]) + shift_ref[0]
    proj = jnp.dot(h.astype(BF16), win_ref[...], preferred_element_type=F32)
    cos, sa, sb = cos_ref[0], sa_ref[0], sb_ref[0]
    cos4, sa4, sb4 = _tile_lanes(cos, 4), _tile_lanes(sa, 4), _tile_lanes(sb, 4)

    aq_ref[0] = (_rope(proj[:, 0:512], cos4, sa4, sb4) * ATT_SCALE).astype(BF16)
    ak_ref[0] = _rope(proj[:, 512:640], cos, sa, sb).astype(BF16)
    av_ref[0] = proj[:, 640:768].astype(BF16)

    cq = proj[:, 768:1024]
    ms = jnp.mean(cq * cq, axis=-1, keepdims=True)
    cqn = (cq * lax.rsqrt(ms + RMS_EPS) * qn_ref[...]).astype(BF16)
    bq = jnp.dot(cqn, wuq_ref[...], preferred_element_type=F32)
    biq = jnp.dot(cqn, wuiq_ref[...], preferred_element_type=F32)
    bq_ref[0] = (_rope(bq, cos4, sa4, sb4) * ATT_SCALE).astype(BF16)
    biq_ref[0] = _rope(biq, cos4, sa4, sb4).astype(BF16)

    lane = lax.broadcasted_iota(jnp.int32, cos.shape, 1)
    roped = lane < HEAD_DIM
    c0 = jnp.where(roped, cos, 1.0)
    a0 = jnp.where(roped, sa, 0.0)
    b0 = jnp.where(roped, sb, 0.0)
    m0 = _rope(proj[:, 1024:1152], c0, a0, b0)
    m1 = _rope(proj[:, 1152:1280], c0, a0, b0)
    misc_ref[0] = jnp.concatenate([m0, m1], axis=-1).astype(BF16)
    iw_ref[0] = proj[:, 1216:1224] * IDX_SCALE


def _proj_ab(x, shift, scale, cos, sa, sb, w_in, q_norm, w_uq, w_uiq, tm=512):
    bsz, s, d = x.shape
    row = lambda w: pl.BlockSpec((1, tm, w), lambda b, i: (b, i, 0))
    vec = pl.BlockSpec((1, 1, d), lambda b, i: (b, 0, 0))
    full = lambda a: pl.BlockSpec(a.shape, lambda b, i: (0,) * a.ndim)
    widths = (512, 128, 128, 512, 512, 256)
    out_shape = [jax.ShapeDtypeStruct((bsz, s, w), BF16) for w in widths]
    out_shape.append(jax.ShapeDtypeStruct((bsz, s, B_IDX_HEADS), F32))
    out_specs = [row(w) for w in widths] + [row(B_IDX_HEADS)]
    return pl.pallas_call(
        _proj_ab_kernel,
        grid=(bsz, s // tm),
        in_specs=[row(d), vec, vec, row(128), row(128), row(128), full(w_in), full(q_norm), full(w_uq), full(w_uiq)],
        out_specs=out_specs,
        out_shape=out_shape,
        compiler_params=_params(("parallel", "parallel")),
        name="proj_ab",
    )(x, shift, scale, cos, sa, sb, w_in, q_norm, w_uq, w_uiq)


def _swa_kernel(sink_ref, q_ref, kp_ref, kc_ref, vp_ref, vc_ref, o_ref):
    j = pl.program_id(1)
    q = q_ref[0]
    kb = jnp.concatenate([kp_ref[0], kc_ref[0]], axis=0)
    vb = jnp.concatenate([vp_ref[0], vc_ref[0]], axis=0)
    qi = lax.broadcasted_iota(jnp.int32, (A_BLOCK, 2 * A_BLOCK), 0)
    c = lax.broadcasted_iota(jnp.int32, (A_BLOCK, 2 * A_BLOCK), 1)
    mask = (c > qi) & (c <= qi + A_BLOCK) & ((c >= A_BLOCK) | (j > 0))
    group = A_Q_HEADS // A_KV_HEADS
    outs = []
    for hq in range(A_Q_HEADS):
        kh = hq // group
        k_h = kb[:, kh * HEAD_DIM:(kh + 1) * HEAD_DIM]
        v_h = vb[:, kh * HEAD_DIM:(kh + 1) * HEAD_DIM]
        s = lax.dot_general(q[:, hq * HEAD_DIM:(hq + 1) * HEAD_DIM], k_h, _NT, preferred_element_type=F32)
        s = jnp.where(mask, s, NEG_INF)
        sink = sink_ref[hq]
        m = jnp.maximum(jnp.max(s, axis=-1, keepdims=True), sink)
        p = jnp.exp(s - m)
        den = jnp.sum(p, axis=-1, keepdims=True) + jnp.exp(sink - m)
        o = jnp.dot(p.astype(BF16), v_h, preferred_element_type=F32)
        outs.append(o / den)
    o_ref[0] = jnp.concatenate(outs, axis=-1).astype(BF16)


def _swa_attention(sinks, aq, ak, av):
    bsz, s, _ = aq.shape
    nb = s // A_BLOCK
    kvw = A_KV_HEADS * HEAD_DIM
    cur = pl.BlockSpec((1, A_BLOCK, kvw), lambda b, j: (b, j, 0))
    prev = pl.BlockSpec((1, A_BLOCK, kvw), lambda b, j: (b, jnp.maximum(j - 1, 0), 0))
    qspec = pl.BlockSpec((1, A_BLOCK, A_Q_HEADS * HEAD_DIM), lambda b, j: (b, j, 0))
    return pl.pallas_call(
        _swa_kernel,
        grid=(bsz, nb),
        in_specs=[pl.BlockSpec(memory_space=pltpu.SMEM), qspec, prev, cur, prev, cur],
        out_specs=qspec,
        out_shape=jax.ShapeDtypeStruct(aq.shape, BF16),
        compiler_params=_params(("parallel", "parallel")),
        name="swa_attention",
    )(sinks, aq, ak, ak, av, av)


def _dsa_kernel(q_ref, iq_ref, iw_ref, misc_ref, o_ref):
    i = pl.program_id(1)
    s_len = misc_ref.shape[1]
    tq = q_ref.shape[1]
    kk = misc_ref[0, :, 0:64]
    vv = misc_ref[0, :, 64:128]
    ik = misc_ref[0, :, 128:192]
    iq = iq_ref[0]
    iw = iw_ref[0]

    score = jnp.zeros((tq, s_len), F32)
    for h in range(B_IDX_HEADS):
        raw = lax.dot_general(iq[:, h * HEAD_DIM:(h + 1) * HEAD_DIM], ik, _NT, preferred_element_type=F32)
        score = score + jnp.maximum(raw, 0.0) * iw[:, h:h + 1]
    score = jnp.where(score == 0.0, 0.0, score)

    tpos = i * tq + lax.broadcasted_iota(jnp.int32, (tq, s_len), 0)
    kpos = lax.broadcasted_iota(jnp.int32, (tq, s_len), 1)
    causal = kpos <= tpos

    bits = lax.bitcast_convert_type(score, jnp.int32)
    key = jnp.where(bits >= 0, bits, bits ^ jnp.int32(0x7FFFFFFF))
    key = jnp.where(causal, key, INT_MIN)

    def search(it, thr):
        cand = thr + lax.shift_left(jnp.int32(1), 31 - it)
        cnt = jnp.sum(jnp.where(key >= cand, 1.0, 0.0), axis=-1, keepdims=True)
        return jnp.where(cnt >= float(B_TOPK), cand, thr)

    thr = lax.fori_loop(0, 32, search, jnp.full((tq, 1), INT_MIN, jnp.int32))

    gt = key > thr
    eq = key == thr
    need = float(B_TOPK) - jnp.sum(jnp.where(gt, 1.0, 0.0), axis=-1, keepdims=True)
    chunk = 256
    ra = lax.broadcasted_iota(jnp.int32, (chunk, chunk), 0)
    rb = lax.broadcasted_iota(jnp.int32, (chunk, chunk), 1)
    upper = jnp.where(ra < rb, 1.0, 0.0).astype(BF16)
    eqf = jnp.where(eq, 1.0, 0.0)
    carry = jnp.zeros((tq, 1), F32)
    prefix = []
    for cidx in range(s_len // chunk):
        e = eqf[:, cidx * chunk:(cidx + 1) * chunk]
        prefix.append(jnp.dot(e.astype(BF16), upper, preferred_element_type=F32) + carry)
        carry = carry + jnp.sum(e, axis=-1, keepdims=True)
    prefix = jnp.concatenate(prefix, axis=-1)
    sel = causal & (gt | (eq & (prefix < need)))

    q = q_ref[0]
    outs = []
    for h in range(B_Q_HEADS):
        s = lax.dot_general(q[:, h * HEAD_DIM:(h + 1) * HEAD_DIM], kk, _NT, preferred_element_type=F32)
        s = jnp.where(sel, s, NEG_INF)
        m = jnp.max(s, axis=-1, keepdims=True)
        p = jnp.exp(s - m)
        den = jnp.sum(p, axis=-1, keepdims=True)
        o = jnp.dot(p.astype(BF16), vv, preferred_element_type=F32)
        outs.append(o / den)
    o_ref[0] = jnp.concatenate(outs, axis=-1).astype(BF16)


def _dsa_attention(bq, biq, iw, misc):
    bsz, s, _ = bq.shape
    tq = B_QBLOCK
    qspec = pl.BlockSpec((1, tq, B_Q_HEADS * HEAD_DIM), lambda b, i: (b, i, 0))
    return pl.pallas_call(
        _dsa_kernel,
        grid=(bsz, s // tq),
        in_specs=[qspec, qspec,
                  pl.BlockSpec((1, tq, B_IDX_HEADS), lambda b, i: (b, i, 0)),
                  pl.BlockSpec((1, s, misc.shape[2]), lambda b, i: (b, 0, 0))],
        out_specs=qspec,
        out_shape=jax.ShapeDtypeStruct(bq.shape, BF16),
        compiler_params=_params(("parallel", "arbitrary")),
        name="dsa_attention",
    )(bq, biq, iw, misc)


def _outproj_ln_kernel(*refs, widths):
    n = len(widths)
    parts = refs[:n]
    w_ref, x_ref, gate_ref, g_ref, b_ref, o_ref = refs[n:]
    y = None
    off = 0
    for p_ref, wd in zip(parts, widths):
        t = jnp.dot(p_ref[0], w_ref[off:off + wd, :], preferred_element_type=F32)
        y = t if y is None else y + t
        off += wd
    z = DN_ALPHA * x_ref[0] + gate_ref[0] * y
    o_ref[0] = _layer_norm(z, g_ref[...], b_ref[...])


def _outproj_ln(parts, w_out, x, gate, g, b, tm=512):
    bsz, s, d = x.shape
    widths = tuple(p.shape[-1] for p in parts)
    row = lambda w: pl.BlockSpec((1, tm, w), lambda bi, i: (bi, i, 0))
    vec = pl.BlockSpec((1, 1, d), lambda bi, i: (bi, 0, 0))
    full = lambda a: pl.BlockSpec(a.shape, lambda bi, i: (0,) * a.ndim)
    return pl.pallas_call(
        functools.partial(_outproj_ln_kernel, widths=widths),
        grid=(bsz, s // tm),
        in_specs=[row(w) for w in widths] + [full(w_out), row(d), vec, full(g), full(b)],
        out_specs=row(d),
        out_shape=jax.ShapeDtypeStruct(x.shape, F32),
        compiler_params=_params(("parallel", "parallel")),
        name="outproj_ln",
    )(*parts, w_out, x, gate, g, b)


def _mlp_kernel(x_ref, shift_ref, scale_ref, gate_ref, w1_ref, w2_ref, g_ref, b_ref, o_ref, h_scr, acc_scr):
    f = pl.program_id(2)

    @pl.when(f == 0)
    def _():
        h_scr[...] = (x_ref[0] * (1.0 + scale_ref[0]) + shift_ref[0]).astype(BF16)
        acc_scr[...] = jnp.zeros_like(acc_scr)

    u = jnp.maximum(jnp.dot(h_scr[...], w1_ref[...], preferred_element_type=F32), 0.0)
    acc_scr[...] += jnp.dot((u * u).astype(BF16), w2_ref[...], preferred_element_type=F32)

    @pl.when(f == pl.num_programs(2) - 1)
    def _():
        z = DN_ALPHA * x_ref[0] + gate_ref[0] * acc_scr[...]
        o_ref[0] = _layer_norm(z, g_ref[...], b_ref[...])


def _mlp_ln(x, shift, scale, gate, w1, w2, g, b, tm=512, tf=1024):
    bsz, s, d = x.shape
    dff = w1.shape[1]
    row = pl.BlockSpec((1, tm, d), lambda bi, i, f: (bi, i, 0))
    vec = pl.BlockSpec((1, 1, d), lambda bi, i, f: (bi, 0, 0))
    full = lambda a: pl.BlockSpec(a.shape, lambda bi, i, f: (0,) * a.ndim)
    return pl.pallas_call(
        _mlp_kernel,
        grid=(bsz, s // tm, dff // tf),
        in_specs=[row, vec, vec, vec,
                  pl.BlockSpec((d, tf), lambda bi, i, f: (0, f)),
                  pl.BlockSpec((tf, d), lambda bi, i, f: (f, 0)),
                  full(g), full(b)],
        out_specs=row,
        out_shape=jax.ShapeDtypeStruct(x.shape, F32),
        scratch_shapes=[pltpu.VMEM((tm, d), BF16), pltpu.VMEM((tm, d), F32)],
        compiler_params=_params(("parallel", "parallel", "arbitrary")),
        name="mlp_ln",
    )(x, shift, scale, gate, w1, w2, g, b)


def _proj_c_kernel(x_ref, shift_ref, scale_ref, cos_ref, sa_ref, sb_ref, win_ref, q_ref, k_ref, v_ref, km_ref):
    h = x_ref[0] * (1.0 + scale_ref[0]) + shift_ref[0]
    proj = jnp.dot(h.astype(BF16), win_ref[...], preferred_element_type=F32)
    cw = C_HEADS * HEAD_DIM
    n = cw // 128
    cos, sa, sb = _tile_lanes(cos_ref[0], n), _tile_lanes(sa_ref[0], n), _tile_lanes(sb_ref[0], n)
    q_ref[0] = (_rope(proj[:, 0:cw], cos, sa, sb) * ATT_SCALE).astype(BF16)
    k = _rope(proj[:, cw:2 * cw], cos, sa, sb)
    k_ref[0] = k.astype(BF16)
    v_ref[0] = proj[:, 2 * cw:3 * cw].astype(BF16)
    km_ref[0, 0] = jnp.mean(k, axis=0, keepdims=True)


def _proj_c(x, shift, scale, cos, sa, sb, w_in):
    bsz, s, d = x.shape
    tm = C_BLOCK
    cw = C_HEADS * HEAD_DIM
    row = lambda w: pl.BlockSpec((1, tm, w), lambda b, i: (b, i, 0))
    vec = pl.BlockSpec((1, 1, d), lambda b, i: (b, 0, 0))
    full = lambda a: pl.BlockSpec(a.shape, lambda b, i: (0,) * a.ndim)
    qkv = jax.ShapeDtypeStruct((bsz, s, cw), BF16)
    return pl.pallas_call(
        _proj_c_kernel,
        grid=(bsz, s // tm),
        in_specs=[row(d), vec, vec, row(128), row(128), row(128), full(w_in)],
        out_specs=[row(cw), row(cw), row(cw), pl.BlockSpec((1, 1, 1, cw), lambda b, i: (b, i, 0, 0))],
        out_shape=[qkv, qkv, qkv, jax.ShapeDtypeStruct((bsz, s // tm, 1, cw), F32)],
        compiler_params=_params(("parallel", "parallel")),
        name="proj_c",
    )(x, shift, scale, cos, sa, sb, w_in)


def _moba_kernel(q_ref, k_ref, v_ref, km_ref, o_ref, m_scr, l_scr, acc_scr):
    i = pl.program_id(2)
    tq = q_ref.shape[1]
    nb = k_ref.shape[1] // C_BLOCK
    lane = lax.broadcasted_iota(jnp.int32, (tq, 128), 1)
    n_idx = lane & (nb - 1)
    row = lax.broadcasted_iota(jnp.int32, (tq, C_BLOCK), 0)
    col = lax.broadcasted_iota(jnp.int32, (tq, C_BLOCK), 1)
    outs = []
    for hh in range(2):
        hs = slice(hh * HEAD_DIM, (hh + 1) * HEAD_DIM)
        q = q_ref[0, :, hs]
        gate = jnp.dot(q, km_ref[0, hh].astype(BF16), preferred_element_type=F32)
        cnt = jnp.zeros((tq, 128), F32)
        for jj in range(1, nb):
            other = pltpu.roll(gate, jj, 1)
            m_idx = (n_idx - jj) & (nb - 1)
            beats = (other > gate) | ((other == gate) & (m_idx < n_idx))
            cnt = cnt + jnp.where(beats & (m_idx < i), 1.0, 0.0)
        bias = jnp.where((n_idx < i) & (cnt < float(C_TOPK)), 0.0, NEG_INF)

        r0 = pl.multiple_of(i * C_BLOCK, C_BLOCK)
        ko = k_ref[0, pl.ds(r0, C_BLOCK), hs]
        vo = v_ref[0, pl.ds(r0, C_BLOCK), hs]
        s = lax.dot_general(q, ko, _NT, preferred_element_type=F32)
        s = jnp.where(col <= row, s, NEG_INF)
        m0 = jnp.max(s, axis=-1, keepdims=True)
        p = jnp.exp(s - m0)
        m_scr[...] = m0
        l_scr[...] = jnp.sum(p, axis=-1, keepdims=True)
        acc_scr[...] = jnp.dot(p.astype(BF16), vo, preferred_element_type=F32)

        for j in range(nb - 1):
            @pl.when(j < i)
            def _(j=j):
                kj = k_ref[0, j * C_BLOCK:(j + 1) * C_BLOCK, hs]
                vj = v_ref[0, j * C_BLOCK:(j + 1) * C_BLOCK, hs]
                sj = lax.dot_general(q, kj, _NT, preferred_element_type=F32) + bias[:, j:j + 1]
                m_old = m_scr[...]
                m_new = jnp.maximum(m_old, jnp.max(sj, axis=-1, keepdims=True))
                a = jnp.exp(m_old - m_new)
                pj = jnp.exp(sj - m_new)
                l_scr[...] = a * l_scr[...] + jnp.sum(pj, axis=-1, keepdims=True)
                acc_scr[...] = a * acc_scr[...] + jnp.dot(pj.astype(BF16), vj, preferred_element_type=F32)
                m_scr[...] = m_new

        outs.append(acc_scr[...] / l_scr[...])
    o_ref[0] = jnp.concatenate(outs, axis=-1).astype(BF16)


def _moba_attention(q, k, v, kmrep):
    bsz, s, cw = q.shape
    tq = C_BLOCK
    qspec = pl.BlockSpec((1, tq, 128), lambda b, hp, i: (b, i, hp))
    kvspec = pl.BlockSpec((1, s, 128), lambda b, hp, i: (b, 0, hp))
    return pl.pallas_call(
        _moba_kernel,
        grid=(bsz, cw // 128, s // tq),
        in_specs=[qspec, kvspec, kvspec,
                  pl.BlockSpec((1, 2, HEAD_DIM, 128), lambda b, hp, i: (b, hp, 0, 0))],
        out_specs=qspec,
        out_shape=jax.ShapeDtypeStruct(q.shape, BF16),
        scratch_shapes=[pltpu.VMEM((tq, 1), F32), pltpu.VMEM((tq, 1), F32), pltpu.VMEM((tq, HEAD_DIM), F32)],
        compiler_params=_params(("parallel", "parallel", "arbitrary")),
        name="moba_attention",
    )(q, k, v, kmrep)


def _rope_tables(positions):
    inv = ROPE_THETA ** (-jnp.arange(0, HEAD_DIM, 2, dtype=F32) / HEAD_DIM)
    ang = positions.astype(F32)[..., None] * inv
    cos = jnp.cos(ang)
    sin = jnp.sin(ang)
    zero = jnp.zeros_like(sin)
    cos_t = jnp.concatenate([cos] * 4, axis=-1)
    sin_a = jnp.concatenate([zero, sin] * 2, axis=-1)
    sin_b = jnp.concatenate([-sin, zero] * 2, axis=-1)
    return cos_t, sin_a, sin_b


def kernel(x, c, positions, ab_w_in, ab_q_norm, ab_w_uq, ab_w_uiq, ab_sinks, ab_w_out, c_w_in, c_w_out,
           ada_w, ada_b, ln_g, ln_b, mlp_w1, mlp_w2):
    bsz, s, d = x.shape
    cos, sa, sb = _rope_tables(positions)
    mod = _ada_modulation(c, ada_w, ada_b)

    def mods(idx):
        m = mod[idx]
        return m[:, None, 0:d], m[:, None, d:2 * d], m[:, None, 2 * d:3 * d] + 1.0

    for layer in range(DEPTH):
        shift, scale, gate = mods(2 * layer)
        g0, b0 = ln_g[layer, 0][None], ln_b[layer, 0][None]
        if layer % 2 == 0:
            e = layer // 2
            w_in = jnp.pad(ab_w_in[e], ((0, 0), (0, AB_IN_PAD - AB_IN_WIDTH))).astype(BF16)
            aq, ak, av, bq, biq, misc, iw = _proj_ab(
                x, shift, scale, cos, sa, sb, w_in, ab_q_norm[e][None],
                ab_w_uq[e].astype(BF16), ab_w_uiq[e].astype(BF16))
            ya = _swa_attention(ab_sinks[e], aq, ak, av)
            yb = _dsa_attention(bq, biq, iw, misc)
            x = _outproj_ln([ya, yb], ab_w_out[e].astype(BF16), x, gate, g0, b0)
        else:
            o = layer // 2
            q, k, v, kmean = _proj_c(x, shift, scale, cos, sa, sb, c_w_in[o].astype(BF16))
            nb = s // C_BLOCK
            km = kmean.reshape(bsz, nb, C_HEADS, HEAD_DIM).transpose(0, 2, 3, 1)
            kmrep = jnp.tile(km, (1, 1, 1, 128 // nb))
            y = _moba_attention(q, k, v, kmrep)
            x = _outproj_ln([y], c_w_out[o].astype(BF16), x, gate, g0, b0)
        shift, scale, gate = mods(2 * layer + 1)
        x = _mlp_ln(x, shift, scale, gate, mlp_w1[layer].astype(BF16), mlp_w2[layer].astype(BF16),
                    ln_g[layer, 1][None], ln_b[layer, 1][None])
    return x
```

```python
import functools

import jax
import jax.numpy as jnp
from jax import lax
from jax.experimental import pallas as pl
from jax.experimental.pallas import tpu as pltpu

D_MODEL = 1024
HEAD_DIM = 64
ROPE_THETA = 10000.0
DEPTH = 2
A_Q_HEADS = 8
A_KV_HEADS = 2
A_BLOCK = 128
B_Q_HEADS = 8
B_Q_RANK = 256
B_IDX_HEADS = 8
B_TOPK = 256
B_QBLOCK = 128
C_HEADS = 16
C_BLOCK = 256
C_TOPK = 3
D_FF = 4 * D_MODEL
DN_ALPHA = (2 * DEPTH) ** 0.25
LN_EPS = 1e-5
RMS_EPS = 1e-6
AB_IN_WIDTH = 1224
AB_IN_PAD = 1280
ATT_SCALE = HEAD_DIM ** -0.5
IDX_SCALE = B_IDX_HEADS ** -0.5 * HEAD_DIM ** -0.5

F32 = jnp.float32
BF16 = jnp.bfloat16
NEG_INF = float("-inf")
MASK_NEG = -1e30
INT_MIN = -(2 ** 31)
VMEM_LIMIT = 48 * 1024 * 1024

_NT = (((1,), (1,)), ((), ()))


def _params(sem):
    return pltpu.CompilerParams(dimension_semantics=sem, vmem_limit_bytes=VMEM_LIMIT)


def _rope(t, cos, sin_a, sin_b):
    w = t.shape[-1]
    return t * cos + pltpu.roll(t, 32, 1) * sin_a + pltpu.roll(t, w - 32, 1) * sin_b


def _tile_lanes(t, n):
    return t if n == 1 else jnp.concatenate([t] * n, axis=-1)


def _layer_norm(z, g, b):
    mu = jnp.mean(z, axis=-1, keepdims=True)
    zc = z - mu
    var = jnp.mean(zc * zc, axis=-1, keepdims=True)
    return zc * lax.rsqrt(var + LN_EPS) * g + b


def _ada_kernel(c_ref, w_ref, b_ref, o_ref):
    c = c_ref[...]
    sc = c / (1.0 + jnp.exp(-c))
    o_ref[0] = jnp.dot(sc, w_ref[0], preferred_element_type=F32) + b_ref[0]


def _ada_modulation(c, ada_w, ada_b):
    n = ada_w.shape[0] * ada_w.shape[1]
    bsz, d = c.shape
    w = ada_w.reshape(n, d, 3 * d)
    b = ada_b.reshape(n, 1, 3 * d)
    tn = 1024
    return pl.pallas_call(
        _ada_kernel,
        grid=(n, 3 * d // tn),
        in_specs=[
            pl.BlockSpec((bsz, d), lambda i, j: (0, 0)),
            pl.BlockSpec((1, d, tn), lambda i, j: (i, 0, j)),
            pl.BlockSpec((1, 1, tn), lambda i, j: (i, 0, j)),
        ],
        out_specs=pl.BlockSpec((1, bsz, tn), lambda i, j: (i, 0, j)),
        out_shape=jax.ShapeDtypeStruct((n, bsz, 3 * d), F32),
        compiler_params=_params(("arbitrary", "arbitrary")),
        name="ada_modulation",
    )(c, w, b)


def _proj_ab_kernel(x_ref, shift_ref, scale_ref, cos_ref, sa_ref, sb_ref, win_ref, qn_ref, wuq_ref, wuiq_ref,
                    aq_ref, ak_ref, av_ref, bq_ref, biq_ref, misc_ref, iw_ref):
    h = x_ref[0] * (1.0 + scale_ref[0]) + shift_ref[0]
    proj = jnp.dot(h.astype(BF16), win_ref[...], preferred_element_type=F32)
    cos, sa, sb = cos_ref[0], sa_ref[0], sb_ref[0]
    cos4, sa4, sb4 = _tile_lanes(cos, 4), _tile_lanes(sa, 4), _tile_lanes(sb, 4)

    aq_ref[0] = (_rope(proj[:, 0:512], cos4, sa4, sb4) * ATT_SCALE).astype(BF16)
    ak_ref[0] = _rope(proj[:, 512:640], cos, sa, sb).astype(BF16)
    av_ref[0] = proj[:, 640:768].astype(BF16)

    cq = proj[:, 768:1024]
    ms = jnp.mean(cq * cq, axis=-1, keepdims=True)
    cqn = (cq * lax.rsqrt(ms + RMS_EPS) * qn_ref[...]).astype(BF16)
    bq = jnp.dot(cqn, wuq_ref[...], preferred_element_type=F32)
    biq = jnp.dot(cqn, wuiq_ref[...], preferred_element_type=F32)
    bq_ref[0] = (_rope(bq, cos4, sa4, sb4) * ATT_SCALE).astype(BF16)
    biq_ref[0] = _rope(biq, cos4, sa4, sb4).astype(BF16)

    lane = lax.broadcasted_iota(jnp.int32, cos.shape, 1)
    roped = lane < HEAD_DIM
    c0 = jnp.where(roped, cos, 1.0)
    a0 = jnp.where(roped, sa, 0.0)
    b0 = jnp.where(roped, sb, 0.0)
    m0 = _rope(proj[:, 1024:1152], c0, a0, b0)
    m1 = _rope(proj[:, 1152:1280], c0, a0, b0)
    misc_ref[0] = jnp.concatenate([m0, m1], axis=-1).astype(BF16)
    iw_ref[0] = proj[:, 1216:1224] * IDX_SCALE


def _proj_ab(x, shift, scale, cos, sa, sb, w_in, q_norm, w_uq, w_uiq, tm=512):
    bsz, s, d = x.shape
    row = lambda w: pl.BlockSpec((1, tm, w), lambda b, i: (b, i, 0))
    vec = pl.BlockSpec((1, 1, d), lambda b, i: (b, 0, 0))
    full = lambda a: pl.BlockSpec(a.shape, lambda b, i: (0,) * a.ndim)
    widths = (512, 128, 128, 512, 512, 256)
    out_shape = [jax.ShapeDtypeStruct((bsz, s, w), BF16) for w in widths]
    out_shape.append(jax.ShapeDtypeStruct((bsz, s, B_IDX_HEADS), F32))
    out_specs = [row(w) for w in widths] + [row(B_IDX_HEADS)]
    return pl.pallas_call(
        _proj_ab_kernel,
        grid=(bsz, s // tm),
        in_specs=[row(d), vec, vec, row(128), row(128), row(128), full(w_in), full(q_norm), full(w_uq), full(w_uiq)],
        out_specs=out_specs,
        out_shape=out_shape,
        compiler_params=_params(("parallel", "parallel")),
        name="proj_ab",
    )(x, shift, scale, cos, sa, sb, w_in, q_norm, w_uq, w_uiq)


def _swa_kernel(sink_ref, q_ref, kp_ref, kc_ref, vp_ref, vc_ref, o_ref):
    j = pl.program_id(1)
    q = q_ref[0]
    kb = jnp.concatenate([kp_ref[0], kc_ref[0]], axis=0)
    vb = jnp.concatenate([vp_ref[0], vc_ref[0]], axis=0)
    qi = lax.broadcasted_iota(jnp.int32, (A_BLOCK, 2 * A_BLOCK), 0)
    c = lax.broadcasted_iota(jnp.int32, (A_BLOCK, 2 * A_BLOCK), 1)
    mask = (c > qi) & (c <= qi + A_BLOCK) & ((c >= A_BLOCK) | (j > 0))
    group = A_Q_HEADS // A_KV_HEADS
    outs = []
    for hq in range(A_Q_HEADS):
        kh = hq // group
        k_h = kb[:, kh * HEAD_DIM:(kh + 1) * HEAD_DIM]
        v_h = vb[:, kh * HEAD_DIM:(kh + 1) * HEAD_DIM]
        s = lax.dot_general(q[:, hq * HEAD_DIM:(hq + 1) * HEAD_DIM], k_h, _NT, preferred_element_type=F32)
        s = jnp.where(mask, s, NEG_INF)
        sink = sink_ref[hq]
        m = jnp.maximum(jnp.max(s, axis=-1, keepdims=True), sink)
        p = jnp.exp(s - m)
        den = jnp.sum(p, axis=-1, keepdims=True) + jnp.exp(sink - m)
        o = jnp.dot(p.astype(BF16), v_h, preferred_element_type=F32)
        outs.append(o / den)
    o_ref[0] = jnp.concatenate(outs, axis=-1).astype(BF16)


def _swa_attention(sinks, aq, ak, av):
    bsz, s, _ = aq.shape
    nb = s // A_BLOCK
    kvw = A_KV_HEADS * HEAD_DIM
    cur = pl.BlockSpec((1, A_BLOCK, kvw), lambda b, j: (b, j, 0))
    prev = pl.BlockSpec((1, A_BLOCK, kvw), lambda b, j: (b, jnp.maximum(j - 1, 0), 0))
    qspec = pl.BlockSpec((1, A_BLOCK, A_Q_HEADS * HEAD_DIM), lambda b, j: (b, j, 0))
    return pl.pallas_call(
        _swa_kernel,
        grid=(bsz, nb),
        in_specs=[pl.BlockSpec(memory_space=pltpu.SMEM), qspec, prev, cur, prev, cur],
        out_specs=qspec,
        out_shape=jax.ShapeDtypeStruct(aq.shape, BF16),
        compiler_params=_params(("parallel", "parallel")),
        name="swa_attention",
    )(sinks, aq, ak, ak, av, av)


def _dsa_kernel(q_ref, iq_ref, iw_ref, misc_ref, o_ref):
    i = pl.program_id(1)
    s_len = misc_ref.shape[1]
    tq = q_ref.shape[1]
    kk = misc_ref[0, :, 0:64]
    vv = misc_ref[0, :, 64:128]
    ik = misc_ref[0, :, 128:192]
    iq = iq_ref[0]
    iw = iw_ref[0]

    score = jnp.zeros((tq, s_len), F32)
    for h in range(B_IDX_HEADS):
        raw = lax.dot_general(iq[:, h * HEAD_DIM:(h + 1) * HEAD_DIM], ik, _NT, preferred_element_type=F32)
        score = score + jnp.maximum(raw, 0.0) * iw[:, h:h + 1]
    score = jnp.where(score == 0.0, 0.0, score)

    tpos = i * tq + lax.broadcasted_iota(jnp.int32, (tq, s_len), 0)
    kpos = lax.broadcasted_iota(jnp.int32, (tq, s_len), 1)
    causal = kpos <= tpos

    bits = lax.bitcast_convert_type(score, jnp.int32)
    key = jnp.where(bits >= 0, bits, bits ^ jnp.int32(0x7FFFFFFF))
    key = jnp.where(causal, key, INT_MIN)

    def search(it, thr):
        cand = thr + lax.shift_left(jnp.int32(1), 31 - it)
        cnt = jnp.sum(jnp.where(key >= cand, 1.0, 0.0), axis=-1, keepdims=True)
        return jnp.where(cnt >= float(B_TOPK), cand, thr)

    thr = lax.fori_loop(0, 32, search, jnp.full((tq, 1), INT_MIN, jnp.int32))

    gt = key > thr
    eq = key == thr
    need = float(B_TOPK) - jnp.sum(jnp.where(gt, 1.0, 0.0), axis=-1, keepdims=True)
    chunk = 256
    ra = lax.broadcasted_iota(jnp.int32, (chunk, chunk), 0)
    rb = lax.broadcasted_iota(jnp.int32, (chunk, chunk), 1)
    upper = jnp.where(ra < rb, 1.0, 0.0).astype(BF16)
    eqf = jnp.where(eq, 1.0, 0.0)
    carry = jnp.zeros((tq, 1), F32)
    prefix = []
    for cidx in range(s_len // chunk):
        e = eqf[:, cidx * chunk:(cidx + 1) * chunk]
        prefix.append(jnp.dot(e.astype(BF16), upper, preferred_element_type=F32) + carry)
        carry = carry + jnp.sum(e, axis=-1, keepdims=True)
    prefix = jnp.concatenate(prefix, axis=-1)
    sel = causal & (gt | (eq & (prefix < need)))

    q = q_ref[0]
    outs = []
    for h in range(B_Q_HEADS):
        s = lax.dot_general(q[:, h * HEAD_DIM:(h + 1) * HEAD_DIM], kk, _NT, preferred_element_type=F32)
        s = jnp.where(sel, s, NEG_INF)
        m = jnp.max(s, axis=-1, keepdims=True)
        p = jnp.exp(s - m)
        den = jnp.sum(p, axis=-1, keepdims=True)
        o = jnp.dot(p.astype(BF16), vv, preferred_element_type=F32)
        outs.append(o / den)
    o_ref[0] = jnp.concatenate(outs, axis=-1).astype(BF16)


def _dsa_attention(bq, biq, iw, misc):
    bsz, s, _ = bq.shape
    tq = B_QBLOCK
    qspec = pl.BlockSpec((1, tq, B_Q_HEADS * HEAD_DIM), lambda b, i: (b, i, 0))
    return pl.pallas_call(
        _dsa_kernel,
        grid=(bsz, s // tq),
        in_specs=[qspec, qspec,
                  pl.BlockSpec((1, tq, B_IDX_HEADS), lambda b, i: (b, i, 0)),
                  pl.BlockSpec((1, s, misc.shape[2]), lambda b, i: (b, 0, 0))],
        out_specs=qspec,
        out_shape=jax.ShapeDtypeStruct(bq.shape, BF16),
        compiler_params=_params(("parallel", "arbitrary")),
        name="dsa_attention",
    )(bq, biq, iw, misc)


def _outproj_ln_kernel(*refs, widths):
    n = len(widths)
    parts = refs[:n]
    w_ref, x_ref, gate_ref, g_ref, b_ref, o_ref = refs[n:]
    y = None
    off = 0
    for p_ref, wd in zip(parts, widths):
        t = jnp.dot(p_ref[0], w_ref[off:off + wd, :], preferred_element_type=F32)
        y = t if y is None else y + t
        off += wd
    z = DN_ALPHA * x_ref[0] + gate_ref[0] * y
    o_ref[0] = _layer_norm(z, g_ref[...], b_ref[...])


def _outproj_ln(parts, w_out, x, gate, g, b, tm=512):
    bsz, s, d = x.shape
    widths = tuple(p.shape[-1] for p in parts)
    row = lambda w: pl.BlockSpec((1, tm, w), lambda bi, i: (bi, i, 0))
    vec = pl.BlockSpec((1, 1, d), lambda bi, i: (bi, 0, 0))
    full = lambda a: pl.BlockSpec(a.shape, lambda bi, i: (0,) * a.ndim)
    return pl.pallas_call(
        functools.partial(_outproj_ln_kernel, widths=widths),
        grid=(bsz, s // tm),
        in_specs=[row(w) for w in widths] + [full(w_out), row(d), vec, full(g), full(b)],
        out_specs=row(d),
        out_shape=jax.ShapeDtypeStruct(x.shape, F32),
        compiler_params=_params(("parallel", "parallel")),
        name="outproj_ln",
    )(*parts, w_out, x, gate, g, b)


def _mlp_kernel(x_ref, shift_ref, scale_ref, gate_ref, w1_ref, w2_ref, g_ref, b_ref, o_ref, h_scr, acc_scr):
    f = pl.program_id(2)

    @pl.when(f == 0)
    def _():
        h_scr[...] = (x_ref[0] * (1.0 + scale_ref[0]) + shift_ref[0]).astype(BF16)
        acc_scr[...] = jnp.zeros_like(acc_scr)

    u = jnp.maximum(jnp.dot(h_scr[...], w1_ref[...], preferred_element_type=F32), 0.0)
    acc_scr[...] += jnp.dot((u * u).astype(BF16), w2_ref[...], preferred_element_type=F32)

    @pl.when(f == pl.num_programs(2) - 1)
    def _():
        z = DN_ALPHA * x_ref[0] + gate_ref[0] * acc_scr[...]
        o_ref[0] = _layer_norm(z, g_ref[...], b_ref[...])


def _mlp_ln(x, shift, scale, gate, w1, w2, g, b, tm=512, tf=1024):
    bsz, s, d = x.shape
    dff = w1.shape[1]
    row = pl.BlockSpec((1, tm, d), lambda bi, i, f: (bi, i, 0))
    vec = pl.BlockSpec((1, 1, d), lambda bi, i, f: (bi, 0, 0))
    full = lambda a: pl.BlockSpec(a.shape, lambda bi, i, f: (0,) * a.ndim)
    return pl.pallas_call(
        _mlp_kernel,
        grid=(bsz, s // tm, dff // tf),
        in_specs=[row, vec, vec, vec,
                  pl.BlockSpec((d, tf), lambda bi, i, f: (0, f)),
                  pl.BlockSpec((tf, d), lambda bi, i, f: (f, 0)),
                  full(g), full(b)],
        out_specs=row,
        out_shape=jax.ShapeDtypeStruct(x.shape, F32),
        scratch_shapes=[pltpu.VMEM((tm, d), BF16), pltpu.VMEM((tm, d), F32)],
        compiler_params=_params(("parallel", "parallel", "arbitrary")),
        name="mlp_ln",
    )(x, shift, scale, gate, w1, w2, g, b)


def _proj_c_kernel(x_ref, shift_ref, scale_ref, cos_ref, sa_ref, sb_ref, win_ref, q_ref, k_ref, v_ref, km_ref):
    h = x_ref[0] * (1.0 + scale_ref[0]) + shift_ref[0]
    proj = jnp.dot(h.astype(BF16), win_ref[...], preferred_element_type=F32)
    cw = C_HEADS * HEAD_DIM
    n = cw // 128
    cos, sa, sb = _tile_lanes(cos_ref[0], n), _tile_lanes(sa_ref[0], n), _tile_lanes(sb_ref[0], n)
    q_ref[0] = (_rope(proj[:, 0:cw], cos, sa, sb) * ATT_SCALE).astype(BF16)
    k = _rope(proj[:, cw:2 * cw], cos, sa, sb)
    k_ref[0] = k.astype(BF16)
    v_ref[0] = proj[:, 2 * cw:3 * cw].astype(BF16)
    km_ref[0, 0] = jnp.mean(k, axis=0, keepdims=True)


def _proj_c(x, shift, scale, cos, sa, sb, w_in):
    bsz, s, d = x.shape
    tm = C_BLOCK
    cw = C_HEADS * HEAD_DIM
    row = lambda w: pl.BlockSpec((1, tm, w), lambda b, i: (b, i, 0))
    vec = pl.BlockSpec((1, 1, d), lambda b, i: (b, 0, 0))
    full = lambda a: pl.BlockSpec(a.shape, lambda b, i: (0,) * a.ndim)
    qkv = jax.ShapeDtypeStruct((bsz, s, cw), BF16)
    return pl.pallas_call(
        _proj_c_kernel,
        grid=(bsz, s // tm),
        in_specs=[row(d), vec, vec, row(128), row(128), row(128), full(w_in)],
        out_specs=[row(cw), row(cw), row(cw), pl.BlockSpec((1, 1, 1, cw), lambda b, i: (b, i, 0, 0))],
        out_shape=[qkv, qkv, qkv, jax.ShapeDtypeStruct((bsz, s // tm, 1, cw), F32)],
        compiler_params=_params(("parallel", "parallel")),
        name="proj_c",
    )(x, shift, scale, cos, sa, sb, w_in)


MOBA_HEADS_PER_STEP = 4


def _fold_max(s):
    return jnp.maximum(s[:, :128], s[:, 128:])


def _moba_kernel(q_ref, k_ref, v_ref, km_ref, o_ref, s_scr, mx_scr, l_scr, acc_scr):
    i = pl.program_id(2)
    tq = q_ref.shape[1]
    nb = k_ref.shape[1] // C_BLOCK
    nh = MOBA_HEADS_PER_STEP
    own = nb - 1
    row = lax.broadcasted_iota(jnp.int32, (tq, C_BLOCK), 0)
    col = lax.broadcasted_iota(jnp.int32, (tq, C_BLOCK), 1)
    causal = col <= row
    r0 = pl.multiple_of(i * C_BLOCK, C_BLOCK)
    hsl = [slice(hh * HEAD_DIM, (hh + 1) * HEAD_DIM) for hh in range(nh)]
    blk = lambda j: slice(j * C_BLOCK, (j + 1) * C_BLOCK)

    n_idx = lax.broadcasted_iota(jnp.int32, (nb, tq), 0)
    qs, biases = [], []
    for hh in range(nh):
        q = q_ref[0, :, hsl[hh]]
        qs.append(q)
        gate = lax.dot_general(km_ref[0, hh].astype(BF16), q, _NT, preferred_element_type=F32)
        cnt = jnp.zeros((nb, tq), F32)
        for m_idx in range(nb - 1):
            other = gate[m_idx:m_idx + 1, :]
            tie = jnp.where(n_idx > m_idx, 1.0, 0.0)
            beats = jnp.where(other > gate, 1.0, jnp.where(other == gate, tie, 0.0))
            cnt = cnt + jnp.where(m_idx < i, beats, 0.0)
        bias_t = jnp.where(n_idx < i, jnp.where(cnt < float(C_TOPK), 0.0, MASK_NEG), MASK_NEG)
        bias_t = jnp.concatenate([bias_t, jnp.zeros((128 - nb, tq), F32)], axis=0)
        biases.append(bias_t.T.astype(BF16))

    for hh in range(nh):
        ko = k_ref[0, pl.ds(r0, C_BLOCK), hsl[hh]]
        s = lax.dot_general(qs[hh], ko, _NT, preferred_element_type=F32)
        s = jnp.where(causal, s, NEG_INF)
        s_scr[hh, :, blk(own)] = s
        mx_scr[hh] = _fold_max(s)
    blk_row = lax.broadcasted_iota(jnp.int32, (128, C_BLOCK), 0)
    for j in range(nb - 1):
        @pl.when(j < i)
        def _(j=j):
            pick = jnp.where(blk_row == j, 1.0, 0.0).astype(BF16)
            for hh in range(nh):
                kj = k_ref[0, blk(j), hsl[hh]]
                s = (lax.dot_general(qs[hh], kj, _NT, preferred_element_type=F32)
                     + jnp.dot(biases[hh], pick, preferred_element_type=F32))
                s_scr[hh, :, blk(j)] = s
                mx_scr[hh] = jnp.maximum(mx_scr[hh], _fold_max(s))

    for hh in range(nh):
        m = jnp.max(mx_scr[hh], axis=-1, keepdims=True)
        mx_scr[hh] = jnp.broadcast_to(m, (tq, 128))

    def probs(hh, j):
        m = mx_scr[hh]
        p0 = jnp.exp(s_scr[hh, :, j * C_BLOCK:j * C_BLOCK + 128] - m)
        p1 = jnp.exp(s_scr[hh, :, j * C_BLOCK + 128:(j + 1) * C_BLOCK] - m)
        return p0 + p1, jnp.concatenate([p0, p1], axis=-1).astype(BF16)

    for hh in range(nh):
        vo = v_ref[0, pl.ds(r0, C_BLOCK), hsl[hh]]
        psum, p = probs(hh, own)
        l_scr[hh] = psum
        acc_scr[hh] = jnp.dot(p, vo, preferred_element_type=F32)
    for j in range(nb - 1):
        @pl.when(j < i)
        def _(j=j):
            for hh in range(nh):
                vj = v_ref[0, blk(j), hsl[hh]]
                psum, p = probs(hh, j)
                l_scr[hh] += psum
                acc_scr[hh] += jnp.dot(p, vj, preferred_element_type=F32)

    outs = []
    for hh in range(nh):
        den = jnp.sum(l_scr[hh], axis=-1, keepdims=True)
        outs.append(acc_scr[hh] / den)
    o_ref[0] = jnp.concatenate(outs, axis=-1).astype(BF16)


def _moba_attention(q, k, v, kmean):
    bsz, s, cw = q.shape
    tq = C_BLOCK
    nh = MOBA_HEADS_PER_STEP
    w = nh * HEAD_DIM
    qspec = pl.BlockSpec((1, tq, w), lambda b, hg, i: (b, i, hg))
    kvspec = pl.BlockSpec((1, s, w), lambda b, hg, i: (b, 0, hg))
    return pl.pallas_call(
        _moba_kernel,
        grid=(bsz, cw // w, s // tq),
        in_specs=[qspec, kvspec, kvspec,
                  pl.BlockSpec((1, nh, s // C_BLOCK, HEAD_DIM), lambda b, hg, i: (b, hg, 0, 0))],
        out_specs=qspec,
        out_shape=jax.ShapeDtypeStruct(q.shape, BF16),
        scratch_shapes=[pltpu.VMEM((nh, tq, s), F32), pltpu.VMEM((nh, tq, 128), F32),
                        pltpu.VMEM((nh, tq, 128), F32), pltpu.VMEM((nh, tq, HEAD_DIM), F32)],
        compiler_params=_params(("parallel", "parallel", "arbitrary")),
        name="moba_attention",
    )(q, k, v, kmean)


def _rope_tables(positions):
    inv = ROPE_THETA ** (-jnp.arange(0, HEAD_DIM, 2, dtype=F32) / HEAD_DIM)
    ang = positions.astype(F32)[..., None] * inv
    cos = jnp.cos(ang)
    sin = jnp.sin(ang)
    zero = jnp.zeros_like(sin)
    cos_t = jnp.concatenate([cos] * 4, axis=-1)
    sin_a = jnp.concatenate([zero, sin] * 2, axis=-1)
    sin_b = jnp.concatenate([-sin, zero] * 2, axis=-1)
    return cos_t, sin_a, sin_b


def kernel(x, c, positions, ab_w_in, ab_q_norm, ab_w_uq, ab_w_uiq, ab_sinks, ab_w_out, c_w_in, c_w_out,
           ada_w, ada_b, ln_g, ln_b, mlp_w1, mlp_w2):
    bsz, s, d = x.shape
    cos, sa, sb = _rope_tables(positions)
    mod = _ada_modulation(c, ada_w, ada_b)

    def mods(idx):
        m = mod[idx]
        return m[:, None, 0:d], m[:, None, d:2 * d], m[:, None, 2 * d:3 * d] + 1.0

    for layer in range(DEPTH):
        shift, scale, gate = mods(2 * layer)
        g0, b0 = ln_g[layer, 0][None], ln_b[layer, 0][None]
        if layer % 2 == 0:
            e = layer // 2
            w_in = jnp.pad(ab_w_in[e], ((0, 0), (0, AB_IN_PAD - AB_IN_WIDTH))).astype(BF16)
            aq, ak, av, bq, biq, misc, iw = _proj_ab(
                x, shift, scale, cos, sa, sb, w_in, ab_q_norm[e][None],
                ab_w_uq[e].astype(BF16), ab_w_uiq[e].astype(BF16))
            ya = _swa_attention(ab_sinks[e], aq, ak, av)
            yb = _dsa_attention(bq, biq, iw, misc)
            x = _outproj_ln([ya, yb], ab_w_out[e].astype(BF16), x, gate, g0, b0)
        else:
            o = layer // 2
            q, k, v, kmean = _proj_c(x, shift, scale, cos, sa, sb, c_w_in[o].astype(BF16))
            km = kmean.reshape(bsz, s // C_BLOCK, C_HEADS, HEAD_DIM).transpose(0, 2, 1, 3)
            y = _moba_attention(q, k, v, km)
            x = _outproj_ln([y], c_w_out[o].astype(BF16), x, gate, g0, b0)
        shift, scale, gate = mods(2 * layer + 1)
        x = _mlp_ln(x, shift, scale, gate, mlp_w1[layer].astype(BF16), mlp_w2[layer].astype(BF16),
                    ln_g[layer, 1][None], ln_b[layer, 1][None])
    return x
```

```python
import functools

import jax
import jax.numpy as jnp
from jax import lax
from jax.experimental import pallas as pl
from jax.experimental.pallas import tpu as pltpu

D_MODEL = 1024
HEAD_DIM = 64
ROPE_THETA = 10000.0
DEPTH = 2
A_Q_HEADS = 8
A_KV_HEADS = 2
A_BLOCK = 128
B_Q_HEADS = 8
B_Q_RANK = 256
B_IDX_HEADS = 8
B_TOPK = 256
B_QBLOCK = 128
C_HEADS = 16
C_BLOCK = 256
C_TOPK = 3
D_FF = 4 * D_MODEL
DN_ALPHA = (2 * DEPTH) ** 0.25
LN_EPS = 1e-5
RMS_EPS = 1e-6
AB_IN_WIDTH = 1224
AB_IN_PAD = 1280
ATT_SCALE = HEAD_DIM ** -0.5
IDX_SCALE = B_IDX_HEADS ** -0.5 * HEAD_DIM ** -0.5

F32 = jnp.float32
BF16 = jnp.bfloat16
NEG_INF = float("-inf")
MASK_NEG = -1e30
INT_MIN = -(2 ** 31)
VMEM_LIMIT = 48 * 1024 * 1024

_NT = (((1,), (1,)), ((), ()))


def _params(sem):
    return pltpu.CompilerParams(dimension_semantics=sem, vmem_limit_bytes=VMEM_LIMIT)


def _rope(t, cos, sin_a, sin_b):
    w = t.shape[-1]
    return t * cos + pltpu.roll(t, 32, 1) * sin_a + pltpu.roll(t, w - 32, 1) * sin_b


def _tile_lanes(t, n):
    return t if n == 1 else jnp.concatenate([t] * n, axis=-1)


def _layer_norm(z, g, b):
    mu = jnp.mean(z, axis=-1, keepdims=True)
    zc = z - mu
    var = jnp.mean(zc * zc, axis=-1, keepdims=True)
    return zc * lax.rsqrt(var + LN_EPS) * g + b


def _ada_kernel(c_ref, w_ref, b_ref, o_ref):
    c = c_ref[...]
    sc = c / (1.0 + jnp.exp(-c))
    o_ref[0] = jnp.dot(sc, w_ref[0], preferred_element_type=F32) + b_ref[0]


def _ada_modulation(c, ada_w, ada_b):
    n = ada_w.shape[0] * ada_w.shape[1]
    bsz, d = c.shape
    w = ada_w.reshape(n, d, 3 * d)
    b = ada_b.reshape(n, 1, 3 * d)
    tn = 1024
    return pl.pallas_call(
        _ada_kernel,
        grid=(n, 3 * d // tn),
        in_specs=[
            pl.BlockSpec((bsz, d), lambda i, j: (0, 0)),
            pl.BlockSpec((1, d, tn), lambda i, j: (i, 0, j)),
            pl.BlockSpec((1, 1, tn), lambda i, j: (i, 0, j)),
        ],
        out_specs=pl.BlockSpec((1, bsz, tn), lambda i, j: (i, 0, j)),
        out_shape=jax.ShapeDtypeStruct((n, bsz, 3 * d), F32),
        compiler_params=_params(("arbitrary", "arbitrary")),
        name="ada_modulation",
    )(c, w, b)


def _proj_ab_kernel(x_ref, shift_ref, scale_ref, cos_ref, sa_ref, sb_ref, win_ref, qn_ref, wuq_ref, wuiq_ref,
                    aq_ref, ak_ref, av_ref, bq_ref, biq_ref, misc_ref, iw_ref):
    h = x_ref[0] * (1.0 + scale_ref[0]) + shift_ref[0]
    proj = jnp.dot(h.astype(BF16), win_ref[...], preferred_element_type=F32)
    cos, sa, sb = cos_ref[0], sa_ref[0], sb_ref[0]
    cos4, sa4, sb4 = _tile_lanes(cos, 4), _tile_lanes(sa, 4), _tile_lanes(sb, 4)

    aq_ref[0] = (_rope(proj[:, 0:512], cos4, sa4, sb4) * ATT_SCALE).astype(BF16)
    ak_ref[0] = _rope(proj[:, 512:640], cos, sa, sb).astype(BF16)
    av_ref[0] = proj[:, 640:768].astype(BF16)

    cq = proj[:, 768:1024]
    ms = jnp.mean(cq * cq, axis=-1, keepdims=True)
    cqn = (cq * lax.rsqrt(ms + RMS_EPS) * qn_ref[...]).astype(BF16)
    bq = jnp.dot(cqn, wuq_ref[...], preferred_element_type=F32)
    biq = jnp.dot(cqn, wuiq_ref[...], preferred_element_type=F32)
    bq_ref[0] = (_rope(bq, cos4, sa4, sb4) * ATT_SCALE).astype(BF16)
    biq_ref[0] = _rope(biq, cos4, sa4, sb4).astype(BF16)

    lane = lax.broadcasted_iota(jnp.int32, cos.shape, 1)
    roped = lane < HEAD_DIM
    c0 = jnp.where(roped, cos, 1.0)
    a0 = jnp.where(roped, sa, 0.0)
    b0 = jnp.where(roped, sb, 0.0)
    m0 = _rope(proj[:, 1024:1152], c0, a0, b0)
    m1 = _rope(proj[:, 1152:1280], c0, a0, b0)
    misc_ref[0] = jnp.concatenate([m0, m1], axis=-1).astype(BF16)
    iw_ref[0] = proj[:, 1216:1224] * IDX_SCALE


def _proj_ab(x, shift, scale, cos, sa, sb, w_in, q_norm, w_uq, w_uiq, tm=512):
    bsz, s, d = x.shape
    row = lambda w: pl.BlockSpec((1, tm, w), lambda b, i: (b, i, 0))
    vec = pl.BlockSpec((1, 1, d), lambda b, i: (b, 0, 0))
    full = lambda a: pl.BlockSpec(a.shape, lambda b, i: (0,) * a.ndim)
    widths = (512, 128, 128, 512, 512, 256)
    out_shape = [jax.ShapeDtypeStruct((bsz, s, w), BF16) for w in widths]
    out_shape.append(jax.ShapeDtypeStruct((bsz, s, B_IDX_HEADS), F32))
    out_specs = [row(w) for w in widths] + [row(B_IDX_HEADS)]
    return pl.pallas_call(
        _proj_ab_kernel,
        grid=(bsz, s // tm),
        in_specs=[row(d), vec, vec, row(128), row(128), row(128), full(w_in), full(q_norm), full(w_uq), full(w_uiq)],
        out_specs=out_specs,
        out_shape=out_shape,
        compiler_params=_params(("parallel", "parallel")),
        name="proj_ab",
    )(x, shift, scale, cos, sa, sb, w_in, q_norm, w_uq, w_uiq)


def _swa_kernel(sink_ref, q_ref, kp_ref, kc_ref, vp_ref, vc_ref, o_ref):
    j = pl.program_id(1)
    q = q_ref[0]
    kb = jnp.concatenate([kp_ref[0], kc_ref[0]], axis=0)
    vb = jnp.concatenate([vp_ref[0], vc_ref[0]], axis=0)
    qi = lax.broadcasted_iota(jnp.int32, (A_BLOCK, 2 * A_BLOCK), 0)
    c = lax.broadcasted_iota(jnp.int32, (A_BLOCK, 2 * A_BLOCK), 1)
    mask = (c > qi) & (c <= qi + A_BLOCK) & ((c >= A_BLOCK) | (j > 0))
    group = A_Q_HEADS // A_KV_HEADS
    outs = []
    for hq in range(A_Q_HEADS):
        kh = hq // group
        k_h = kb[:, kh * HEAD_DIM:(kh + 1) * HEAD_DIM]
        v_h = vb[:, kh * HEAD_DIM:(kh + 1) * HEAD_DIM]
        s = lax.dot_general(q[:, hq * HEAD_DIM:(hq + 1) * HEAD_DIM], k_h, _NT, preferred_element_type=F32)
        s = jnp.where(mask, s, NEG_INF)
        sink = sink_ref[hq]
        m = jnp.maximum(jnp.max(s, axis=-1, keepdims=True), sink)
        p = jnp.exp(s - m)
        den = jnp.sum(p, axis=-1, keepdims=True) + jnp.exp(sink - m)
        o = jnp.dot(p.astype(BF16), v_h, preferred_element_type=F32)
        outs.append(o / den)
    o_ref[0] = jnp.concatenate(outs, axis=-1).astype(BF16)


def _swa_attention(sinks, aq, ak, av):
    bsz, s, _ = aq.shape
    nb = s // A_BLOCK
    kvw = A_KV_HEADS * HEAD_DIM
    cur = pl.BlockSpec((1, A_BLOCK, kvw), lambda b, j: (b, j, 0))
    prev = pl.BlockSpec((1, A_BLOCK, kvw), lambda b, j: (b, jnp.maximum(j - 1, 0), 0))
    qspec = pl.BlockSpec((1, A_BLOCK, A_Q_HEADS * HEAD_DIM), lambda b, j: (b, j, 0))
    return pl.pallas_call(
        _swa_kernel,
        grid=(bsz, nb),
        in_specs=[pl.BlockSpec(memory_space=pltpu.SMEM), qspec, prev, cur, prev, cur],
        out_specs=qspec,
        out_shape=jax.ShapeDtypeStruct(aq.shape, BF16),
        compiler_params=_params(("parallel", "parallel")),
        name="swa_attention",
    )(sinks, aq, ak, ak, av, av)


DSA_TQ = 256
DSA_KC = 256


def _dsa_kernel(q_ref, iq_ref, iwt_ref, misc_ref, vt_ref, o_ref, key_scr, acc_scr):
    i = pl.program_id(1)
    tq, kc = DSA_TQ, DSA_KC
    nkc = i + 1
    krow = lax.broadcasted_iota(jnp.int32, (kc, tq), 0)
    qcol = lax.broadcasted_iota(jnp.int32, (kc, tq), 1)
    on_or_below_diag = krow <= qcol
    iwt = iwt_ref[0]
    hsl = [slice(h * HEAD_DIM, (h + 1) * HEAD_DIM) for h in range(B_Q_HEADS)]
    iqs = [iq_ref[0, :, hsl[h]] for h in range(B_IDX_HEADS)]
    qs = [q_ref[0, :, hsl[h]] for h in range(B_Q_HEADS)]

    def score_body(c, carry):
        k0 = pl.multiple_of(c * kc, kc)
        ik = misc_ref[0, pl.ds(k0, kc), 128:192]
        sc = jnp.zeros((kc, tq), F32)
        for h in range(B_IDX_HEADS):
            raw = lax.dot_general(ik, iqs[h], _NT, preferred_element_type=F32)
            sc = sc + jnp.maximum(raw, 0.0) * iwt[h:h + 1, :]
        sc = jnp.where(sc == 0.0, 0.0, sc)
        bits = lax.bitcast_convert_type(sc, jnp.int32)
        key = jnp.where(bits >= 0, bits, bits ^ jnp.int32(0x7FFFFFFF))
        key_scr[c] = jnp.where(c < i, key, jnp.where(on_or_below_diag, key, INT_MIN))
        return carry

    lax.fori_loop(0, nkc, score_body, 0)

    def search(it, thr):
        cand = thr + lax.shift_left(jnp.int32(1), 31 - it)

        def count(c, acc):
            return acc + jnp.sum(jnp.where(key_scr[c] >= cand, 1.0, 0.0), axis=0, keepdims=True)

        cnt = lax.fori_loop(0, nkc, count, jnp.zeros((1, tq), F32))
        return jnp.where(cnt >= float(B_TOPK), cand, thr)

    thr = lax.fori_loop(0, 32, search, jnp.full((1, tq), INT_MIN, jnp.int32))

    def stats(c, carry):
        ngt, neq = carry
        k = key_scr[c]
        return (ngt + jnp.sum(jnp.where(k > thr, 1.0, 0.0), axis=0, keepdims=True),
                neq + jnp.sum(jnp.where(k == thr, 1.0, 0.0), axis=0, keepdims=True))

    zero_row = jnp.zeros((1, tq), F32)
    ngt, neq = lax.fori_loop(0, nkc, stats, (zero_row, zero_row))
    need = float(B_TOPK) - ngt

    @pl.when(jnp.max(neq - need) > 0.0)
    def _():
        ra = lax.broadcasted_iota(jnp.int32, (kc, kc), 0)
        rb = lax.broadcasted_iota(jnp.int32, (kc, kc), 1)
        lower = jnp.where(rb < ra, 1.0, 0.0).astype(BF16)

        def drop_late_ties(c, before):
            k = key_scr[c]
            eq = jnp.where(k == thr, 1.0, 0.0)
            prefix = jnp.dot(lower, eq.astype(BF16), preferred_element_type=F32) + before
            key_scr[c] = jnp.where(k == thr, jnp.where(prefix >= need, INT_MIN, k), k)
            return before + jnp.sum(eq, axis=0, keepdims=True)

        lax.fori_loop(0, nkc, drop_late_ties, zero_row)

    thr_sel = jnp.maximum(thr, INT_MIN + 1)

    acc_scr[...] = jnp.zeros_like(acc_scr)

    def attend(c, carry):
        ms, ls = carry
        k0 = pl.multiple_of(c * kc, kc)
        kk = misc_ref[0, pl.ds(k0, kc), 0:64]
        vt = vt_ref[0, :, pl.ds(k0, kc)]
        sel = key_scr[c] >= thr_sel
        ss = [jnp.where(sel, lax.dot_general(kk, qs[h], _NT, preferred_element_type=F32), MASK_NEG)
              for h in range(B_Q_HEADS)]
        ms_new, ls_new, alphas, ps = [], [], [], []
        for h in range(B_Q_HEADS):
            m_new = jnp.maximum(ms[h], jnp.max(ss[h], axis=0, keepdims=True))
            alpha = jnp.exp(ms[h] - m_new)
            p = jnp.exp(ss[h] - m_new)
            ls_new.append(alpha * ls[h] + jnp.sum(p, axis=0, keepdims=True))
            ms_new.append(m_new)
            alphas.append(alpha)
            ps.append(p.astype(BF16))
        for h in range(B_Q_HEADS):
            acc_scr[h] = alphas[h] * acc_scr[h] + jnp.dot(vt, ps[h], preferred_element_type=F32)
        return tuple(ms_new), tuple(ls_new)

    init = (tuple(jnp.full((1, tq), MASK_NEG, F32) for _ in range(B_Q_HEADS)),
            tuple(jnp.zeros((1, tq), F32) for _ in range(B_Q_HEADS)))
    _, ls = lax.fori_loop(0, nkc, attend, init)

    outs = []
    for h in range(0, B_Q_HEADS, 2):
        pair = jnp.concatenate([acc_scr[h] / ls[h], acc_scr[h + 1] / ls[h + 1]], axis=0)
        outs.append(pair.T)
    o_ref[0] = jnp.concatenate(outs, axis=-1).astype(BF16)


def _dsa_attention(bq, biq, iwt, misc, vt):
    bsz, s, _ = bq.shape
    tq = DSA_TQ
    qspec = pl.BlockSpec((1, tq, B_Q_HEADS * HEAD_DIM), lambda b, i: (b, i, 0))
    return pl.pallas_call(
        _dsa_kernel,
        grid=(bsz, s // tq),
        in_specs=[qspec, qspec,
                  pl.BlockSpec((1, B_IDX_HEADS, tq), lambda b, i: (b, 0, i)),
                  pl.BlockSpec((1, s, misc.shape[2]), lambda b, i: (b, 0, 0)),
                  pl.BlockSpec((1, HEAD_DIM, s), lambda b, i: (b, 0, 0))],
        out_specs=qspec,
        out_shape=jax.ShapeDtypeStruct(bq.shape, BF16),
        scratch_shapes=[pltpu.VMEM((s // DSA_KC, DSA_KC, tq), jnp.int32),
                        pltpu.VMEM((B_Q_HEADS, HEAD_DIM, tq), F32)],
        compiler_params=_params(("parallel", "arbitrary")),
        name="dsa_attention",
    )(bq, biq, iwt, misc, vt)


def _outproj_ln_kernel(*refs, widths):
    n = len(widths)
    parts = refs[:n]
    w_ref, x_ref, gate_ref, g_ref, b_ref, o_ref = refs[n:]
    y = None
    off = 0
    for p_ref, wd in zip(parts, widths):
        t = jnp.dot(p_ref[0], w_ref[off:off + wd, :], preferred_element_type=F32)
        y = t if y is None else y + t
        off += wd
    z = DN_ALPHA * x_ref[0] + gate_ref[0] * y
    o_ref[0] = _layer_norm(z, g_ref[...], b_ref[...])


def _outproj_ln(parts, w_out, x, gate, g, b, tm=512):
    bsz, s, d = x.shape
    widths = tuple(p.shape[-1] for p in parts)
    row = lambda w: pl.BlockSpec((1, tm, w), lambda bi, i: (bi, i, 0))
    vec = pl.BlockSpec((1, 1, d), lambda bi, i: (bi, 0, 0))
    full = lambda a: pl.BlockSpec(a.shape, lambda bi, i: (0,) * a.ndim)
    return pl.pallas_call(
        functools.partial(_outproj_ln_kernel, widths=widths),
        grid=(bsz, s // tm),
        in_specs=[row(w) for w in widths] + [full(w_out), row(d), vec, full(g), full(b)],
        out_specs=row(d),
        out_shape=jax.ShapeDtypeStruct(x.shape, F32),
        compiler_params=_params(("parallel", "parallel")),
        name="outproj_ln",
    )(*parts, w_out, x, gate, g, b)


def _mlp_kernel(x_ref, shift_ref, scale_ref, gate_ref, w1_ref, w2_ref, g_ref, b_ref, o_ref, h_scr, acc_scr):
    f = pl.program_id(2)

    @pl.when(f == 0)
    def _():
        h_scr[...] = (x_ref[0] * (1.0 + scale_ref[0]) + shift_ref[0]).astype(BF16)
        acc_scr[...] = jnp.zeros_like(acc_scr)

    u = jnp.maximum(jnp.dot(h_scr[...], w1_ref[...], preferred_element_type=F32), 0.0)
    acc_scr[...] += jnp.dot((u * u).astype(BF16), w2_ref[...], preferred_element_type=F32)

    @pl.when(f == pl.num_programs(2) - 1)
    def _():
        z = DN_ALPHA * x_ref[0] + gate_ref[0] * acc_scr[...]
        o_ref[0] = _layer_norm(z, g_ref[...], b_ref[...])


def _mlp_ln(x, shift, scale, gate, w1, w2, g, b, tm=512, tf=1024):
    bsz, s, d = x.shape
    dff = w1.shape[1]
    row = pl.BlockSpec((1, tm, d), lambda bi, i, f: (bi, i, 0))
    vec = pl.BlockSpec((1, 1, d), lambda bi, i, f: (bi, 0, 0))
    full = lambda a: pl.BlockSpec(a.shape, lambda bi, i, f: (0,) * a.ndim)
    return pl.pallas_call(
        _mlp_kernel,
        grid=(bsz, s // tm, dff // tf),
        in_specs=[row, vec, vec, vec,
                  pl.BlockSpec((d, tf), lambda bi, i, f: (0, f)),
                  pl.BlockSpec((tf, d), lambda bi, i, f: (f, 0)),
                  full(g), full(b)],
        out_specs=row,
        out_shape=jax.ShapeDtypeStruct(x.shape, F32),
        scratch_shapes=[pltpu.VMEM((tm, d), BF16), pltpu.VMEM((tm, d), F32)],
        compiler_params=_params(("parallel", "parallel", "arbitrary")),
        name="mlp_ln",
    )(x, shift, scale, gate, w1, w2, g, b)


def _proj_c_kernel(x_ref, shift_ref, scale_ref, cos_ref, sa_ref, sb_ref, win_ref, q_ref, k_ref, vt_ref, km_ref):
    h = x_ref[0] * (1.0 + scale_ref[0]) + shift_ref[0]
    proj = jnp.dot(h.astype(BF16), win_ref[...], preferred_element_type=F32)
    cw = C_HEADS * HEAD_DIM
    n = cw // 128
    cos, sa, sb = _tile_lanes(cos_ref[0], n), _tile_lanes(sa_ref[0], n), _tile_lanes(sb_ref[0], n)
    q_ref[0] = (_rope(proj[:, 0:cw], cos, sa, sb) * ATT_SCALE).astype(BF16)
    k = _rope(proj[:, cw:2 * cw], cos, sa, sb)
    k_ref[0] = k.astype(BF16)
    vt_ref[0] = proj[:, 2 * cw:3 * cw].T.astype(BF16)
    km_ref[0, 0] = jnp.mean(k, axis=0, keepdims=True)


def _proj_c(x, shift, scale, cos, sa, sb, w_in):
    bsz, s, d = x.shape
    tm = C_BLOCK
    cw = C_HEADS * HEAD_DIM
    row = lambda w: pl.BlockSpec((1, tm, w), lambda b, i: (b, i, 0))
    vec = pl.BlockSpec((1, 1, d), lambda b, i: (b, 0, 0))
    full = lambda a: pl.BlockSpec(a.shape, lambda b, i: (0,) * a.ndim)
    qkv = jax.ShapeDtypeStruct((bsz, s, cw), BF16)
    return pl.pallas_call(
        _proj_c_kernel,
        grid=(bsz, s // tm),
        in_specs=[row(d), vec, vec, row(128), row(128), row(128), full(w_in)],
        out_specs=[row(cw), row(cw), pl.BlockSpec((1, cw, tm), lambda b, i: (b, 0, i)),
                   pl.BlockSpec((1, 1, 1, cw), lambda b, i: (b, i, 0, 0))],
        out_shape=[qkv, qkv, jax.ShapeDtypeStruct((bsz, cw, s), BF16),
                   jax.ShapeDtypeStruct((bsz, s // tm, 1, cw), F32)],
        compiler_params=_params(("parallel", "parallel")),
        name="proj_c",
    )(x, shift, scale, cos, sa, sb, w_in)


MOBA_HEADS_PER_STEP = 8


def _moba_kernel(q_ref, k_ref, vt_ref, km_ref, o_ref, bias_scr, acc_scr):
    i = pl.program_id(2)
    tq = q_ref.shape[1]
    nb = k_ref.shape[1] // C_BLOCK
    nh = MOBA_HEADS_PER_STEP
    hsl = [slice(hh * HEAD_DIM, (hh + 1) * HEAD_DIM) for hh in range(nh)]
    qs = [q_ref[0, :, hsl[hh]] for hh in range(nh)]

    n_idx = lax.broadcasted_iota(jnp.int32, (nb, tq), 0)
    for hh in range(nh):
        gate = lax.dot_general(km_ref[0, hh].astype(BF16), qs[hh], _NT, preferred_element_type=F32)
        cnt = jnp.zeros((nb, tq), F32)
        for m_idx in range(nb - 1):
            other = gate[m_idx:m_idx + 1, :]
            tie = jnp.where(n_idx > m_idx, 1.0, 0.0)
            beats = jnp.where(other > gate, 1.0, jnp.where(other == gate, tie, 0.0))
            cnt = cnt + jnp.where(m_idx < i, beats, 0.0)
        bias_scr[hh] = jnp.where(n_idx < i, jnp.where(cnt < float(C_TOPK), 0.0, MASK_NEG), MASK_NEG)

    def block_update(blk, mask, ms, ls):
        k0 = pl.multiple_of(blk * C_BLOCK, C_BLOCK)
        ss = []
        for hh in range(nh):
            kj = k_ref[0, pl.ds(k0, C_BLOCK), hsl[hh]]
            ss.append(mask(hh, lax.dot_general(kj, qs[hh], _NT, preferred_element_type=F32)))
        ms_new, ls_new, alphas, ps = [], [], [], []
        for hh in range(nh):
            m_new = jnp.maximum(ms[hh], jnp.max(ss[hh], axis=0, keepdims=True))
            alpha = jnp.exp(ms[hh] - m_new)
            p = jnp.exp(ss[hh] - m_new)
            ls_new.append(alpha * ls[hh] + jnp.sum(p, axis=0, keepdims=True))
            ms_new.append(m_new)
            alphas.append(alpha)
            ps.append(p.astype(BF16))
        for hh in range(nh):
            vj = vt_ref[0, hsl[hh], pl.ds(k0, C_BLOCK)]
            acc_scr[hh] = alphas[hh] * acc_scr[hh] + jnp.dot(vj, ps[hh], preferred_element_type=F32)
        return tuple(ms_new), tuple(ls_new)

    krow = lax.broadcasted_iota(jnp.int32, (C_BLOCK, tq), 0)
    qcol = lax.broadcasted_iota(jnp.int32, (C_BLOCK, tq), 1)
    causal = krow <= qcol
    acc_scr[...] = jnp.zeros_like(acc_scr)
    ms0 = tuple(jnp.full((1, tq), MASK_NEG, F32) for _ in range(nh))
    ls0 = tuple(jnp.zeros((1, tq), F32) for _ in range(nh))
    carry = block_update(i, lambda hh, s: jnp.where(causal, s, MASK_NEG), ms0, ls0)

    def past_block(j, carry):
        return block_update(j, lambda hh, s: s + bias_scr[hh, pl.ds(j, 1), :], carry[0], carry[1])

    _, ls = lax.fori_loop(0, i, past_block, carry)

    outs = []
    for hh in range(0, nh, 2):
        pair = jnp.concatenate([acc_scr[hh] / ls[hh], acc_scr[hh + 1] / ls[hh + 1]], axis=0)
        outs.append(pair.T)
    o_ref[0] = jnp.concatenate(outs, axis=-1).astype(BF16)


def _moba_attention(q, k, vt, kmean):
    bsz, s, cw = q.shape
    tq = C_BLOCK
    nh = MOBA_HEADS_PER_STEP
    w = nh * HEAD_DIM
    nb = s // C_BLOCK
    qspec = pl.BlockSpec((1, tq, w), lambda b, hg, i: (b, i, hg))
    return pl.pallas_call(
        _moba_kernel,
        grid=(bsz, cw // w, s // tq),
        in_specs=[qspec,
                  pl.BlockSpec((1, s, w), lambda b, hg, i: (b, 0, hg)),
                  pl.BlockSpec((1, w, s), lambda b, hg, i: (b, hg, 0)),
                  pl.BlockSpec((1, nh, nb, HEAD_DIM), lambda b, hg, i: (b, hg, 0, 0))],
        out_specs=qspec,
        out_shape=jax.ShapeDtypeStruct(q.shape, BF16),
        scratch_shapes=[pltpu.VMEM((nh, nb, tq), F32), pltpu.VMEM((nh, HEAD_DIM, tq), F32)],
        compiler_params=_params(("parallel", "parallel", "arbitrary")),
        name="moba_attention",
    )(q, k, vt, kmean)


def _rope_tables(positions):
    inv = ROPE_THETA ** (-jnp.arange(0, HEAD_DIM, 2, dtype=F32) / HEAD_DIM)
    ang = positions.astype(F32)[..., None] * inv
    cos = jnp.cos(ang)
    sin = jnp.sin(ang)
    zero = jnp.zeros_like(sin)
    cos_t = jnp.concatenate([cos] * 4, axis=-1)
    sin_a = jnp.concatenate([zero, sin] * 2, axis=-1)
    sin_b = jnp.concatenate([-sin, zero] * 2, axis=-1)
    return cos_t, sin_a, sin_b


def kernel(x, c, positions, ab_w_in, ab_q_norm, ab_w_uq, ab_w_uiq, ab_sinks, ab_w_out, c_w_in, c_w_out,
           ada_w, ada_b, ln_g, ln_b, mlp_w1, mlp_w2):
    bsz, s, d = x.shape
    cos, sa, sb = _rope_tables(positions)
    mod = _ada_modulation(c, ada_w, ada_b)

    def mods(idx):
        m = mod[idx]
        return m[:, None, 0:d], m[:, None, d:2 * d], m[:, None, 2 * d:3 * d] + 1.0

    for layer in range(DEPTH):
        shift, scale, gate = mods(2 * layer)
        g0, b0 = ln_g[layer, 0][None], ln_b[layer, 0][None]
        if layer % 2 == 0:
            e = layer // 2
            w_in = jnp.pad(ab_w_in[e], ((0, 0), (0, AB_IN_PAD - AB_IN_WIDTH))).astype(BF16)
            aq, ak, av, bq, biq, misc, iw = _proj_ab(
                x, shift, scale, cos, sa, sb, w_in, ab_q_norm[e][None],
                ab_w_uq[e].astype(BF16), ab_w_uiq[e].astype(BF16))
            ya = _swa_attention(ab_sinks[e], aq, ak, av)
            vt = jnp.swapaxes(misc[:, :, 64:128], 1, 2)
            yb = _dsa_attention(bq, biq, jnp.swapaxes(iw, 1, 2), misc, vt)
            x = _outproj_ln([ya, yb], ab_w_out[e].astype(BF16), x, gate, g0, b0)
        else:
            o = layer // 2
            q, k, vt, kmean = _proj_c(x, shift, scale, cos, sa, sb, c_w_in[o].astype(BF16))
            km = kmean.reshape(bsz, s // C_BLOCK, C_HEADS, HEAD_DIM).transpose(0, 2, 1, 3)
            y = _moba_attention(q, k, vt, km)
            x = _outproj_ln([y], c_w_out[o].astype(BF16), x, gate, g0, b0)
        shift, scale, gate = mods(2 * layer + 1)
        x = _mlp_ln(x, shift, scale, gate, mlp_w1[layer].astype(BF16), mlp_w2[layer].astype(BF16),
                    ln_g[layer, 1][None], ln_b[layer, 1][None])
    return x
```

```python
import functools

import jax
import jax.numpy as jnp
from jax import lax
from jax.experimental import pallas as pl
from jax.experimental.pallas import tpu as pltpu

D_MODEL = 1024
HEAD_DIM = 64
ROPE_THETA = 10000.0
DEPTH = 2
A_Q_HEADS = 8
A_KV_HEADS = 2
A_BLOCK = 128
B_Q_HEADS = 8
B_Q_RANK = 256
B_IDX_HEADS = 8
B_TOPK = 256
B_QBLOCK = 128
C_HEADS = 16
C_BLOCK = 256
C_TOPK = 3
D_FF = 4 * D_MODEL
DN_ALPHA = (2 * DEPTH) ** 0.25
LN_EPS = 1e-5
RMS_EPS = 1e-6
AB_IN_WIDTH = 1224
AB_IN_PAD = 1280
LOG2E = 1.4426950408889634
ATT_SCALE = HEAD_DIM ** -0.5 * LOG2E
IDX_SCALE = B_IDX_HEADS ** -0.5 * HEAD_DIM ** -0.5

F32 = jnp.float32
BF16 = jnp.bfloat16
NEG_INF = float("-inf")
MASK_NEG = -1e30
INT_MIN = -(2 ** 31)
HALF = 2 ** 15
VMEM_LIMIT = 48 * 1024 * 1024

_NT = (((1,), (1,)), ((), ()))


def _params(sem):
    return pltpu.CompilerParams(dimension_semantics=sem, vmem_limit_bytes=VMEM_LIMIT)


def _rope(t, cos, sin_a, sin_b):
    w = t.shape[-1]
    return t * cos + pltpu.roll(t, 32, 1) * sin_a + pltpu.roll(t, w - 32, 1) * sin_b


def _tile_lanes(t, n):
    return t if n == 1 else jnp.concatenate([t] * n, axis=-1)


def _layer_norm(z, g, b):
    mu = jnp.mean(z, axis=-1, keepdims=True)
    zc = z - mu
    var = jnp.mean(zc * zc, axis=-1, keepdims=True)
    return zc * lax.rsqrt(var + LN_EPS) * g + b


def _ada_kernel(c_ref, w_ref, b_ref, o_ref):
    c = c_ref[...]
    sc = c / (1.0 + jnp.exp(-c))
    o_ref[0] = jnp.dot(sc, w_ref[0], preferred_element_type=F32) + b_ref[0]


def _ada_modulation(c, ada_w, ada_b):
    n = ada_w.shape[0] * ada_w.shape[1]
    bsz, d = c.shape
    w = ada_w.reshape(n, d, 3 * d)
    b = ada_b.reshape(n, 1, 3 * d)
    tn = 1024
    return pl.pallas_call(
        _ada_kernel,
        grid=(n, 3 * d // tn),
        in_specs=[
            pl.BlockSpec((bsz, d), lambda i, j: (0, 0)),
            pl.BlockSpec((1, d, tn), lambda i, j: (i, 0, j)),
            pl.BlockSpec((1, 1, tn), lambda i, j: (i, 0, j)),
        ],
        out_specs=pl.BlockSpec((1, bsz, tn), lambda i, j: (i, 0, j)),
        out_shape=jax.ShapeDtypeStruct((n, bsz, 3 * d), F32),
        compiler_params=_params(("arbitrary", "arbitrary")),
        name="ada_modulation",
    )(c, w, b)


def _proj_ab_kernel(x_ref, shift_ref, scale_ref, cos_ref, sa_ref, sb_ref, win_ref, qn_ref, wuq_ref, wuiq_ref,
                    aq_ref, ak_ref, av_ref, bq_ref, biq_ref, misc_ref, iw_ref):
    h = x_ref[0] * (1.0 + scale_ref[0]) + shift_ref[0]
    proj = jnp.dot(h.astype(BF16), win_ref[...], preferred_element_type=F32)
    cos, sa, sb = cos_ref[0], sa_ref[0], sb_ref[0]
    cos4, sa4, sb4 = _tile_lanes(cos, 4), _tile_lanes(sa, 4), _tile_lanes(sb, 4)

    aq_ref[0] = (_rope(proj[:, 0:512], cos4, sa4, sb4) * ATT_SCALE).astype(BF16)
    ak_ref[0] = _rope(proj[:, 512:640], cos, sa, sb).astype(BF16)
    av_ref[0] = proj[:, 640:768].astype(BF16)

    cq = proj[:, 768:1024]
    ms = jnp.mean(cq * cq, axis=-1, keepdims=True)
    cqn = (cq * lax.rsqrt(ms + RMS_EPS) * qn_ref[...]).astype(BF16)
    bq = jnp.dot(cqn, wuq_ref[...], preferred_element_type=F32)
    biq = jnp.dot(cqn, wuiq_ref[...], preferred_element_type=F32)
    bq_ref[0] = (_rope(bq, cos4, sa4, sb4) * ATT_SCALE).astype(BF16)
    biq_ref[0] = _rope(biq, cos4, sa4, sb4).astype(BF16)

    lane = lax.broadcasted_iota(jnp.int32, cos.shape, 1)
    roped = lane < HEAD_DIM
    c0 = jnp.where(roped, cos, 1.0)
    a0 = jnp.where(roped, sa, 0.0)
    b0 = jnp.where(roped, sb, 0.0)
    m0 = _rope(proj[:, 1024:1152], c0, a0, b0)
    m1 = _rope(proj[:, 1152:1280], c0, a0, b0)
    misc_ref[0] = jnp.concatenate([m0, m1], axis=-1).astype(BF16)
    iw_ref[0] = proj[:, 1216:1224] * IDX_SCALE


def _proj_ab(x, shift, scale, cos, sa, sb, w_in, q_norm, w_uq, w_uiq, tm=512):
    bsz, s, d = x.shape
    row = lambda w: pl.BlockSpec((1, tm, w), lambda b, i: (b, i, 0))
    vec = pl.BlockSpec((1, 1, d), lambda b, i: (b, 0, 0))
    full = lambda a: pl.BlockSpec(a.shape, lambda b, i: (0,) * a.ndim)
    widths = (512, 128, 128, 512, 512, 256)
    out_shape = [jax.ShapeDtypeStruct((bsz, s, w), BF16) for w in widths]
    out_shape.append(jax.ShapeDtypeStruct((bsz, s, B_IDX_HEADS), F32))
    out_specs = [row(w) for w in widths] + [row(B_IDX_HEADS)]
    return pl.pallas_call(
        _proj_ab_kernel,
        grid=(bsz, s // tm),
        in_specs=[row(d), vec, vec, row(128), row(128), row(128), full(w_in), full(q_norm), full(w_uq), full(w_uiq)],
        out_specs=out_specs,
        out_shape=out_shape,
        compiler_params=_params(("parallel", "parallel")),
        name="proj_ab",
    )(x, shift, scale, cos, sa, sb, w_in, q_norm, w_uq, w_uiq)


def _swa_kernel(sink_ref, q_ref, kp_ref, kc_ref, vp_ref, vc_ref, o_ref):
    j = pl.program_id(1)
    q = q_ref[0]
    kb = jnp.concatenate([kp_ref[0], kc_ref[0]], axis=0)
    vb = jnp.concatenate([vp_ref[0], vc_ref[0]], axis=0)
    qi = lax.broadcasted_iota(jnp.int32, (A_BLOCK, 2 * A_BLOCK), 0)
    c = lax.broadcasted_iota(jnp.int32, (A_BLOCK, 2 * A_BLOCK), 1)
    mask = (c > qi) & (c <= qi + A_BLOCK) & ((c >= A_BLOCK) | (j > 0))
    group = A_Q_HEADS // A_KV_HEADS
    outs = []
    for hq in range(A_Q_HEADS):
        kh = hq // group
        k_h = kb[:, kh * HEAD_DIM:(kh + 1) * HEAD_DIM]
        v_h = vb[:, kh * HEAD_DIM:(kh + 1) * HEAD_DIM]
        s = lax.dot_general(q[:, hq * HEAD_DIM:(hq + 1) * HEAD_DIM], k_h, _NT, preferred_element_type=F32)
        s = jnp.where(mask, s, NEG_INF)
        sink = sink_ref[hq] * LOG2E
        m = jnp.maximum(jnp.max(s, axis=-1, keepdims=True), sink)
        p = jnp.exp2(s - m)
        den = jnp.sum(p, axis=-1, keepdims=True) + jnp.exp2(sink - m)
        o = jnp.dot(p.astype(BF16), v_h, preferred_element_type=F32)
        outs.append(o / den)
    o_ref[0] = jnp.concatenate(outs, axis=-1).astype(BF16)


def _swa_attention(sinks, aq, ak, av):
    bsz, s, _ = aq.shape
    nb = s // A_BLOCK
    kvw = A_KV_HEADS * HEAD_DIM
    cur = pl.BlockSpec((1, A_BLOCK, kvw), lambda b, j: (b, j, 0))
    prev = pl.BlockSpec((1, A_BLOCK, kvw), lambda b, j: (b, jnp.maximum(j - 1, 0), 0))
    qspec = pl.BlockSpec((1, A_BLOCK, A_Q_HEADS * HEAD_DIM), lambda b, j: (b, j, 0))
    return pl.pallas_call(
        _swa_kernel,
        grid=(bsz, nb),
        in_specs=[pl.BlockSpec(memory_space=pltpu.SMEM), qspec, prev, cur, prev, cur],
        out_specs=qspec,
        out_shape=jax.ShapeDtypeStruct(aq.shape, BF16),
        compiler_params=_params(("parallel", "parallel")),
        name="swa_attention",
    )(sinks, aq, ak, ak, av, av)


DSA_TQ = 256
DSA_KC = 256


def _dsa_kernel(q_ref, iq_ref, iwt_ref, misc_ref, vt_ref, o_ref, key_scr, hi_scr, lo_scr, acc_scr):
    i = pl.program_id(1)
    tq, kc = DSA_TQ, DSA_KC
    nkc = i + 1
    krow = lax.broadcasted_iota(jnp.int32, (kc, tq), 0)
    qcol = lax.broadcasted_iota(jnp.int32, (kc, tq), 1)
    on_or_below_diag = krow <= qcol
    iwt = iwt_ref[0]
    hsl = [slice(h * HEAD_DIM, (h + 1) * HEAD_DIM) for h in range(B_Q_HEADS)]
    iqs = [iq_ref[0, :, hsl[h]] for h in range(B_IDX_HEADS)]
    qs = [q_ref[0, :, hsl[h]] for h in range(B_Q_HEADS)]

    def score_body(c, carry):
        k0 = pl.multiple_of(c * kc, kc)
        ik = misc_ref[0, pl.ds(k0, kc), 128:192]
        sc = jnp.zeros((kc, tq), F32)
        for h in range(B_IDX_HEADS):
            raw = lax.dot_general(ik, iqs[h], _NT, preferred_element_type=F32)
            sc = sc + jnp.maximum(raw, 0.0) * iwt[h:h + 1, :]
        sc = jnp.where(sc == 0.0, 0.0, sc)
        bits = lax.bitcast_convert_type(sc, jnp.int32)
        key = jnp.where(bits >= 0, bits, bits ^ jnp.int32(0x7FFFFFFF))
        key = jnp.where(c < i, key, jnp.where(on_or_below_diag, key, INT_MIN))
        key_scr[c] = key
        hi_scr[c] = (key >> 16).astype(jnp.int16)
        lo_scr[c] = ((key & 0xFFFF) - HALF).astype(jnp.int16)
        return carry

    lax.fori_loop(0, nkc, score_body, 0)

    def count_ge(ref, c, cand):
        ones = jnp.where(ref[c] >= cand.astype(jnp.int16), jnp.int16(1), jnp.int16(0))
        part = ones[0:16]
        for r in range(1, kc // 16):
            part = part + ones[16 * r:16 * (r + 1)]
        return jnp.sum(part.astype(jnp.int32), axis=0, keepdims=True)

    def half_search(ref, wanted):
        def step(it, thr):
            cand = thr + lax.shift_left(jnp.int32(1), 15 - it)
            cnt = lax.fori_loop(0, nkc, lambda c, acc: acc + count_ge(ref, c, cand),
                                jnp.zeros((1, tq), jnp.int32))
            return jnp.where(cnt >= wanted, cand, thr)

        return lax.fori_loop(0, 16, step, jnp.full((1, tq), -HALF, jnp.int32))

    thr_hi = half_search(hi_scr, B_TOPK)

    def mask_low(c, above):
        hi = hi_scr[c].astype(jnp.int32)
        lo_scr[c] = jnp.where(hi == thr_hi, lo_scr[c].astype(jnp.int32), -HALF).astype(jnp.int16)
        return above + jnp.sum(jnp.where(hi > thr_hi, 1, 0), axis=0, keepdims=True)

    above = lax.fori_loop(0, nkc, mask_low, jnp.zeros((1, tq), jnp.int32))
    thr_lo = half_search(lo_scr, B_TOPK - above)
    thr = (thr_hi << 16) | (thr_lo + HALF)

    def stats(c, carry):
        ngt, neq = carry
        k = key_scr[c]
        return (ngt + jnp.sum(jnp.where(k > thr, 1.0, 0.0), axis=0, keepdims=True),
                neq + jnp.sum(jnp.where(k == thr, 1.0, 0.0), axis=0, keepdims=True))

    zero_row = jnp.zeros((1, tq), F32)
    ngt, neq = lax.fori_loop(0, nkc, stats, (zero_row, zero_row))
    need = float(B_TOPK) - ngt

    @pl.when(jnp.max(neq - need) > 0.0)
    def _():
        ra = lax.broadcasted_iota(jnp.int32, (kc, kc), 0)
        rb = lax.broadcasted_iota(jnp.int32, (kc, kc), 1)
        lower = jnp.where(rb < ra, 1.0, 0.0).astype(BF16)

        def drop_late_ties(c, before):
            k = key_scr[c]
            eq = jnp.where(k == thr, 1.0, 0.0)
            prefix = jnp.dot(lower, eq.astype(BF16), preferred_element_type=F32) + before
            key_scr[c] = jnp.where(k == thr, jnp.where(prefix >= need, INT_MIN, k), k)
            return before + jnp.sum(eq, axis=0, keepdims=True)

        lax.fori_loop(0, nkc, drop_late_ties, zero_row)

    thr_sel = jnp.maximum(thr, INT_MIN + 1)

    acc_scr[...] = jnp.zeros_like(acc_scr)

    def attend(c, carry):
        ms, ls = carry
        k0 = pl.multiple_of(c * kc, kc)
        kk = misc_ref[0, pl.ds(k0, kc), 0:64]
        vt = vt_ref[0, :, pl.ds(k0, kc)]
        sel = key_scr[c] >= thr_sel
        ss = [jnp.where(sel, lax.dot_general(kk, qs[h], _NT, preferred_element_type=F32), MASK_NEG)
              for h in range(B_Q_HEADS)]
        ms_new, ls_new, alphas, ps = [], [], [], []
        for h in range(B_Q_HEADS):
            m_new = jnp.maximum(ms[h], jnp.max(ss[h], axis=0, keepdims=True))
            alpha = jnp.exp2(ms[h] - m_new)
            p = jnp.exp2(ss[h] - m_new)
            ls_new.append(alpha * ls[h] + jnp.sum(p, axis=0, keepdims=True))
            ms_new.append(m_new)
            alphas.append(alpha)
            ps.append(p.astype(BF16))
        for h in range(B_Q_HEADS):
            acc_scr[h] = alphas[h] * acc_scr[h] + jnp.dot(vt, ps[h], preferred_element_type=F32)
        return tuple(ms_new), tuple(ls_new)

    init = (tuple(jnp.full((1, tq), MASK_NEG, F32) for _ in range(B_Q_HEADS)),
            tuple(jnp.zeros((1, tq), F32) for _ in range(B_Q_HEADS)))
    _, ls = lax.fori_loop(0, nkc, attend, init)

    outs = []
    for h in range(0, B_Q_HEADS, 2):
        pair = jnp.concatenate([acc_scr[h] / ls[h], acc_scr[h + 1] / ls[h + 1]], axis=0)
        outs.append(pair.T)
    o_ref[0] = jnp.concatenate(outs, axis=-1).astype(BF16)


def _dsa_attention(bq, biq, iwt, misc, vt):
    bsz, s, _ = bq.shape
    tq = DSA_TQ
    qspec = pl.BlockSpec((1, tq, B_Q_HEADS * HEAD_DIM), lambda b, i: (b, i, 0))
    return pl.pallas_call(
        _dsa_kernel,
        grid=(bsz, s // tq),
        in_specs=[qspec, qspec,
                  pl.BlockSpec((1, B_IDX_HEADS, tq), lambda b, i: (b, 0, i)),
                  pl.BlockSpec((1, s, misc.shape[2]), lambda b, i: (b, 0, 0)),
                  pl.BlockSpec((1, HEAD_DIM, s), lambda b, i: (b, 0, 0))],
        out_specs=qspec,
        out_shape=jax.ShapeDtypeStruct(bq.shape, BF16),
        scratch_shapes=[pltpu.VMEM((s // DSA_KC, DSA_KC, tq), jnp.int32),
                        pltpu.VMEM((s // DSA_KC, DSA_KC, tq), jnp.int16),
                        pltpu.VMEM((s // DSA_KC, DSA_KC, tq), jnp.int16),
                        pltpu.VMEM((B_Q_HEADS, HEAD_DIM, tq), F32)],
        compiler_params=_params(("parallel", "arbitrary")),
        name="dsa_attention",
    )(bq, biq, iwt, misc, vt)


def _outproj_ln_kernel(*refs, widths):
    n = len(widths)
    parts = refs[:n]
    w_ref, x_ref, gate_ref, g_ref, b_ref, o_ref = refs[n:]
    y = None
    off = 0
    for p_ref, wd in zip(parts, widths):
        t = jnp.dot(p_ref[0], w_ref[off:off + wd, :], preferred_element_type=F32)
        y = t if y is None else y + t
        off += wd
    z = DN_ALPHA * x_ref[0] + gate_ref[0] * y
    o_ref[0] = _layer_norm(z, g_ref[...], b_ref[...])


def _outproj_ln(parts, w_out, x, gate, g, b, tm=512):
    bsz, s, d = x.shape
    widths = tuple(p.shape[-1] for p in parts)
    row = lambda w: pl.BlockSpec((1, tm, w), lambda bi, i: (bi, i, 0))
    vec = pl.BlockSpec((1, 1, d), lambda bi, i: (bi, 0, 0))
    full = lambda a: pl.BlockSpec(a.shape, lambda bi, i: (0,) * a.ndim)
    return pl.pallas_call(
        functools.partial(_outproj_ln_kernel, widths=widths),
        grid=(bsz, s // tm),
        in_specs=[row(w) for w in widths] + [full(w_out), row(d), vec, full(g), full(b)],
        out_specs=row(d),
        out_shape=jax.ShapeDtypeStruct(x.shape, F32),
        compiler_params=_params(("parallel", "parallel")),
        name="outproj_ln",
    )(*parts, w_out, x, gate, g, b)


def _mlp_kernel(x_ref, shift_ref, scale_ref, gate_ref, w1_ref, w2_ref, g_ref, b_ref, o_ref, h_scr, acc_scr):
    f = pl.program_id(2)

    @pl.when(f == 0)
    def _():
        h_scr[...] = (x_ref[0] * (1.0 + scale_ref[0]) + shift_ref[0]).astype(BF16)
        acc_scr[...] = jnp.zeros_like(acc_scr)

    u = jnp.maximum(jnp.dot(h_scr[...], w1_ref[...], preferred_element_type=F32), 0.0)
    acc_scr[...] += jnp.dot((u * u).astype(BF16), w2_ref[...], preferred_element_type=F32)

    @pl.when(f == pl.num_programs(2) - 1)
    def _():
        z = DN_ALPHA * x_ref[0] + gate_ref[0] * acc_scr[...]
        o_ref[0] = _layer_norm(z, g_ref[...], b_ref[...])


def _mlp_ln(x, shift, scale, gate, w1, w2, g, b, tm=512, tf=1024):
    bsz, s, d = x.shape
    dff = w1.shape[1]
    row = pl.BlockSpec((1, tm, d), lambda bi, i, f: (bi, i, 0))
    vec = pl.BlockSpec((1, 1, d), lambda bi, i, f: (bi, 0, 0))
    full = lambda a: pl.BlockSpec(a.shape, lambda bi, i, f: (0,) * a.ndim)
    return pl.pallas_call(
        _mlp_kernel,
        grid=(bsz, s // tm, dff // tf),
        in_specs=[row, vec, vec, vec,
                  pl.BlockSpec((d, tf), lambda bi, i, f: (0, f)),
                  pl.BlockSpec((tf, d), lambda bi, i, f: (f, 0)),
                  full(g), full(b)],
        out_specs=row,
        out_shape=jax.ShapeDtypeStruct(x.shape, F32),
        scratch_shapes=[pltpu.VMEM((tm, d), BF16), pltpu.VMEM((tm, d), F32)],
        compiler_params=_params(("parallel", "parallel", "arbitrary")),
        name="mlp_ln",
    )(x, shift, scale, gate, w1, w2, g, b)


def _proj_c_kernel(x_ref, shift_ref, scale_ref, cos_ref, sa_ref, sb_ref, win_ref, q_ref, k_ref, vt_ref, km_ref):
    h = x_ref[0] * (1.0 + scale_ref[0]) + shift_ref[0]
    proj = jnp.dot(h.astype(BF16), win_ref[...], preferred_element_type=F32)
    cw = C_HEADS * HEAD_DIM
    n = cw // 128
    cos, sa, sb = _tile_lanes(cos_ref[0], n), _tile_lanes(sa_ref[0], n), _tile_lanes(sb_ref[0], n)
    q_ref[0] = (_rope(proj[:, 0:cw], cos, sa, sb) * ATT_SCALE).astype(BF16)
    k = _rope(proj[:, cw:2 * cw], cos, sa, sb)
    k_ref[0] = k.astype(BF16)
    vt_ref[0] = proj[:, 2 * cw:3 * cw].T.astype(BF16)
    km_ref[0, 0] = jnp.mean(k, axis=0, keepdims=True)


def _proj_c(x, shift, scale, cos, sa, sb, w_in):
    bsz, s, d = x.shape
    tm = C_BLOCK
    cw = C_HEADS * HEAD_DIM
    row = lambda w: pl.BlockSpec((1, tm, w), lambda b, i: (b, i, 0))
    vec = pl.BlockSpec((1, 1, d), lambda b, i: (b, 0, 0))
    full = lambda a: pl.BlockSpec(a.shape, lambda b, i: (0,) * a.ndim)
    qkv = jax.ShapeDtypeStruct((bsz, s, cw), BF16)
    return pl.pallas_call(
        _proj_c_kernel,
        grid=(bsz, s // tm),
        in_specs=[row(d), vec, vec, row(128), row(128), row(128), full(w_in)],
        out_specs=[row(cw), row(cw), pl.BlockSpec((1, cw, tm), lambda b, i: (b, 0, i)),
                   pl.BlockSpec((1, 1, 1, cw), lambda b, i: (b, i, 0, 0))],
        out_shape=[qkv, qkv, jax.ShapeDtypeStruct((bsz, cw, s), BF16),
                   jax.ShapeDtypeStruct((bsz, s // tm, 1, cw), F32)],
        compiler_params=_params(("parallel", "parallel")),
        name="proj_c",
    )(x, shift, scale, cos, sa, sb, w_in)


MOBA_HEADS_PER_STEP = 16


def _moba_kernel(q_ref, k_ref, vt_ref, km_ref, o_ref, bias_scr, acc_scr):
    i = pl.program_id(2)
    tq = q_ref.shape[1]
    nb = k_ref.shape[1] // C_BLOCK
    nh = MOBA_HEADS_PER_STEP
    hsl = [slice(hh * HEAD_DIM, (hh + 1) * HEAD_DIM) for hh in range(nh)]
    qs = [q_ref[0, :, hsl[hh]] for hh in range(nh)]

    n_idx = lax.broadcasted_iota(jnp.int32, (nb, tq), 0)
    for hh in range(nh):
        gate = lax.dot_general(km_ref[0, hh].astype(BF16), qs[hh], _NT, preferred_element_type=F32)
        cnt = jnp.zeros((nb, tq), F32)
        for m_idx in range(nb - 1):
            other = gate[m_idx:m_idx + 1, :]
            tie = jnp.where(n_idx > m_idx, 1.0, 0.0)
            beats = jnp.where(other > gate, 1.0, jnp.where(other == gate, tie, 0.0))
            cnt = cnt + jnp.where(m_idx < i, beats, 0.0)
        bias_scr[hh] = jnp.where(n_idx < i, jnp.where(cnt < float(C_TOPK), 0.0, MASK_NEG), MASK_NEG)

    def block_update(blk, mask, bias, ms, ls):
        k0 = pl.multiple_of(blk * C_BLOCK, C_BLOCK)
        ss = []
        for hh in range(nh):
            kj = k_ref[0, pl.ds(k0, C_BLOCK), hsl[hh]]
            ss.append(mask(lax.dot_general(kj, qs[hh], _NT, preferred_element_type=F32)))
        ms_new, ls_new, alphas, ps = [], [], [], []
        for hh in range(nh):
            b = bias(hh)
            m_new = jnp.maximum(ms[hh], jnp.max(ss[hh], axis=0, keepdims=True) + b)
            alpha = jnp.exp2(ms[hh] - m_new)
            p = jnp.exp2(ss[hh] - (m_new - b))
            ls_new.append(alpha * ls[hh] + jnp.sum(p, axis=0, keepdims=True))
            ms_new.append(m_new)
            alphas.append(alpha)
            ps.append(p.astype(BF16))
        for hh in range(nh):
            vj = vt_ref[0, hsl[hh], pl.ds(k0, C_BLOCK)]
            acc_scr[hh] = alphas[hh] * acc_scr[hh] + jnp.dot(vj, ps[hh], preferred_element_type=F32)
        return tuple(ms_new), tuple(ls_new)

    krow = lax.broadcasted_iota(jnp.int32, (C_BLOCK, tq), 0)
    qcol = lax.broadcasted_iota(jnp.int32, (C_BLOCK, tq), 1)
    causal = krow <= qcol
    acc_scr[...] = jnp.zeros_like(acc_scr)
    ms0 = tuple(jnp.full((1, tq), MASK_NEG, F32) for _ in range(nh))
    ls0 = tuple(jnp.zeros((1, tq), F32) for _ in range(nh))
    carry = block_update(i, lambda s: jnp.where(causal, s, MASK_NEG), lambda hh: 0.0, ms0, ls0)

    def past_block(j, carry):
        return block_update(j, lambda s: s, lambda hh: bias_scr[hh, pl.ds(j, 1), :], carry[0], carry[1])

    _, ls = lax.fori_loop(0, i, past_block, carry)

    outs = []
    for hh in range(0, nh, 2):
        pair = jnp.concatenate([acc_scr[hh] / ls[hh], acc_scr[hh + 1] / ls[hh + 1]], axis=0)
        outs.append(pair.T)
    o_ref[0] = jnp.concatenate(outs, axis=-1).astype(BF16)


def _moba_attention(q, k, vt, kmean):
    bsz, s, cw = q.shape
    tq = C_BLOCK
    nh = MOBA_HEADS_PER_STEP
    w = nh * HEAD_DIM
    nb = s // C_BLOCK
    qspec = pl.BlockSpec((1, tq, w), lambda b, hg, i: (b, i, hg))
    return pl.pallas_call(
        _moba_kernel,
        grid=(bsz, cw // w, s // tq),
        in_specs=[qspec,
                  pl.BlockSpec((1, s, w), lambda b, hg, i: (b, 0, hg)),
                  pl.BlockSpec((1, w, s), lambda b, hg, i: (b, hg, 0)),
                  pl.BlockSpec((1, nh, nb, HEAD_DIM), lambda b, hg, i: (b, hg, 0, 0))],
        out_specs=qspec,
        out_shape=jax.ShapeDtypeStruct(q.shape, BF16),
        scratch_shapes=[pltpu.VMEM((nh, nb, tq), F32), pltpu.VMEM((nh, HEAD_DIM, tq), F32)],
        compiler_params=_params(("parallel", "parallel", "arbitrary")),
        name="moba_attention",
    )(q, k, vt, kmean)


def _rope_tables(positions):
    inv = ROPE_THETA ** (-jnp.arange(0, HEAD_DIM, 2, dtype=F32) / HEAD_DIM)
    ang = positions.astype(F32)[..., None] * inv
    cos, sin = lax.optimization_barrier((jnp.cos(ang), jnp.sin(ang)))
    zero = jnp.zeros_like(sin)
    cos_t = jnp.concatenate([cos] * 4, axis=-1)
    sin_a = jnp.concatenate([zero, sin] * 2, axis=-1)
    sin_b = jnp.concatenate([-sin, zero] * 2, axis=-1)
    return cos_t, sin_a, sin_b


def kernel(x, c, positions, ab_w_in, ab_q_norm, ab_w_uq, ab_w_uiq, ab_sinks, ab_w_out, c_w_in, c_w_out,
           ada_w, ada_b, ln_g, ln_b, mlp_w1, mlp_w2):
    bsz, s, d = x.shape
    cos, sa, sb = _rope_tables(positions)
    mod = _ada_modulation(c, ada_w, ada_b)

    def mods(idx):
        m = mod[idx]
        return m[:, None, 0:d], m[:, None, d:2 * d], m[:, None, 2 * d:3 * d] + 1.0

    for layer in range(DEPTH):
        shift, scale, gate = mods(2 * layer)
        g0, b0 = ln_g[layer, 0][None], ln_b[layer, 0][None]
        if layer % 2 == 0:
            e = layer // 2
            w_in = jnp.pad(ab_w_in[e], ((0, 0), (0, AB_IN_PAD - AB_IN_WIDTH))).astype(BF16)
            aq, ak, av, bq, biq, misc, iw = _proj_ab(
                x, shift, scale, cos, sa, sb, w_in, ab_q_norm[e][None],
                ab_w_uq[e].astype(BF16), ab_w_uiq[e].astype(BF16))
            ya = _swa_attention(ab_sinks[e], aq, ak, av)
            vt = jnp.swapaxes(misc[:, :, 64:128], 1, 2)
            yb = _dsa_attention(bq, biq, jnp.swapaxes(iw, 1, 2), misc, vt)
            x = _outproj_ln([ya, yb], ab_w_out[e].astype(BF16), x, gate, g0, b0)
        else:
            o = layer // 2
            q, k, vt, kmean = _proj_c(x, shift, scale, cos, sa, sb, c_w_in[o].astype(BF16))
            km = kmean.reshape(bsz, s // C_BLOCK, C_HEADS, HEAD_DIM).transpose(0, 2, 1, 3)
            y = _moba_attention(q, k, vt, km)
            x = _outproj_ln([y], c_w_out[o].astype(BF16), x, gate, g0, b0)
        shift, scale, gate = mods(2 * layer + 1)
        x = _mlp_ln(x, shift, scale, gate, mlp_w1[layer].astype(BF16), mlp_w2[layer].astype(BF16),
                    ln_g[layer, 1][None], ln_b[layer, 1][None])
    return x
```

```python
import functools

import jax
import jax.numpy as jnp
from jax import lax
from jax.experimental import pallas as pl
from jax.experimental.pallas import tpu as pltpu

D_MODEL = 1024
HEAD_DIM = 64
ROPE_THETA = 10000.0
DEPTH = 2
A_Q_HEADS = 8
A_KV_HEADS = 2
A_BLOCK = 128
B_Q_HEADS = 8
B_Q_RANK = 256
B_IDX_HEADS = 8
B_TOPK = 256
B_QBLOCK = 128
C_HEADS = 16
C_BLOCK = 256
C_TOPK = 3
D_FF = 4 * D_MODEL
DN_ALPHA = (2 * DEPTH) ** 0.25
LN_EPS = 1e-5
RMS_EPS = 1e-6
AB_IN_WIDTH = 1224
AB_IN_PAD = 1280
LOG2E = 1.4426950408889634
ATT_SCALE = HEAD_DIM ** -0.5 * LOG2E
IDX_SCALE = B_IDX_HEADS ** -0.5 * HEAD_DIM ** -0.5

F32 = jnp.float32
BF16 = jnp.bfloat16
NEG_INF = float("-inf")
MASK_NEG = -1e30
INT_MIN = -(2 ** 31)
HALF = 2 ** 15
VMEM_LIMIT = 48 * 1024 * 1024

_NT = (((1,), (1,)), ((), ()))


def _params(sem):
    return pltpu.CompilerParams(dimension_semantics=sem, vmem_limit_bytes=VMEM_LIMIT)


def _rope(t, cos, sin_a, sin_b):
    w = t.shape[-1]
    return t * cos + pltpu.roll(t, 32, 1) * sin_a + pltpu.roll(t, w - 32, 1) * sin_b


def _rope_coeffs(tab):
    low = (lax.broadcasted_iota(jnp.int32, tab.shape, 1) & (HEAD_DIM - 1)) < HEAD_DIM // 2
    swapped = pltpu.roll(tab, HEAD_DIM // 2, 1)
    cos = jnp.where(low, tab, swapped)
    sin = jnp.where(low, swapped, tab)
    return cos, jnp.where(low, 0.0, sin), jnp.where(low, -sin, 0.0)


def _tile_lanes(t, n):
    return t if n == 1 else jnp.concatenate([t] * n, axis=-1)


def _layer_norm(z, g, b):
    mu = jnp.mean(z, axis=-1, keepdims=True)
    zc = z - mu
    var = jnp.mean(zc * zc, axis=-1, keepdims=True)
    return zc * lax.rsqrt(var + LN_EPS) * g + b


def _ada_kernel(c_ref, w_ref, b_ref, o_ref):
    c = c_ref[...]
    sc = c / (1.0 + jnp.exp(-c))
    o_ref[0] = jnp.dot(sc, w_ref[0], preferred_element_type=F32) + b_ref[0]


def _ada_modulation(c, ada_w, ada_b):
    n = ada_w.shape[0] * ada_w.shape[1]
    bsz, d = c.shape
    w = ada_w.reshape(n, d, 3 * d)
    b = ada_b.reshape(n, 1, 3 * d)
    tn = 1024
    return pl.pallas_call(
        _ada_kernel,
        grid=(n, 3 * d // tn),
        in_specs=[
            pl.BlockSpec((bsz, d), lambda i, j: (0, 0)),
            pl.BlockSpec((1, d, tn), lambda i, j: (i, 0, j)),
            pl.BlockSpec((1, 1, tn), lambda i, j: (i, 0, j)),
        ],
        out_specs=pl.BlockSpec((1, bsz, tn), lambda i, j: (i, 0, j)),
        out_shape=jax.ShapeDtypeStruct((n, bsz, 3 * d), F32),
        compiler_params=_params(("arbitrary", "arbitrary")),
        name="ada_modulation",
    )(c, w, b)


def _proj_ab_kernel(x_ref, shift_ref, scale_ref, tab_ref, win_ref, qn_ref, wuq_ref, wuiq_ref,
                    aq_ref, ak_ref, av_ref, bq_ref, biq_ref, misc_ref, iw_ref):
    h = x_ref[0] * (1.0 + scale_ref[0]) + shift_ref[0]
    proj = jnp.dot(h.astype(BF16), win_ref[...], preferred_element_type=F32)
    cos, sa, sb = _rope_coeffs(tab_ref[0])
    cos4, sa4, sb4 = _tile_lanes(cos, 4), _tile_lanes(sa, 4), _tile_lanes(sb, 4)

    aq_ref[0] = (_rope(proj[:, 0:512], cos4, sa4, sb4) * ATT_SCALE).astype(BF16)
    ak_ref[0] = _rope(proj[:, 512:640], cos, sa, sb).astype(BF16)
    av_ref[0] = proj[:, 640:768].astype(BF16)

    cq = proj[:, 768:1024]
    ms = jnp.mean(cq * cq, axis=-1, keepdims=True)
    cqn = (cq * lax.rsqrt(ms + RMS_EPS) * qn_ref[...]).astype(BF16)
    bq = jnp.dot(cqn, wuq_ref[...], preferred_element_type=F32)
    biq = jnp.dot(cqn, wuiq_ref[...], preferred_element_type=F32)
    bq_ref[0] = (_rope(bq, cos4, sa4, sb4) * ATT_SCALE).astype(BF16)
    biq_ref[0] = _rope(biq, cos4, sa4, sb4).astype(BF16)

    lane = lax.broadcasted_iota(jnp.int32, cos.shape, 1)
    roped = lane < HEAD_DIM
    c0 = jnp.where(roped, cos, 1.0)
    a0 = jnp.where(roped, sa, 0.0)
    b0 = jnp.where(roped, sb, 0.0)
    m0 = _rope(proj[:, 1024:1152], c0, a0, b0)
    m1 = _rope(proj[:, 1152:1280], c0, a0, b0)
    misc_ref[0] = jnp.concatenate([m0, m1], axis=-1).astype(BF16)
    iw_ref[0] = proj[:, 1216:1224] * IDX_SCALE


def _proj_ab(x, shift, scale, rope_tab, w_in, q_norm, w_uq, w_uiq, tm=512):
    bsz, s, d = x.shape
    row = lambda w: pl.BlockSpec((1, tm, w), lambda b, i: (b, i, 0))
    vec = pl.BlockSpec((1, 1, d), lambda b, i: (b, 0, 0))
    full = lambda a: pl.BlockSpec(a.shape, lambda b, i: (0,) * a.ndim)
    widths = (512, 128, 128, 512, 512, 256)
    out_shape = [jax.ShapeDtypeStruct((bsz, s, w), BF16) for w in widths]
    out_shape.append(jax.ShapeDtypeStruct((bsz, s, B_IDX_HEADS), F32))
    out_specs = [row(w) for w in widths] + [row(B_IDX_HEADS)]
    return pl.pallas_call(
        _proj_ab_kernel,
        grid=(bsz, s // tm),
        in_specs=[row(d), vec, vec, row(128), full(w_in), full(q_norm), full(w_uq), full(w_uiq)],
        out_specs=out_specs,
        out_shape=out_shape,
        compiler_params=_params(("parallel", "parallel")),
        name="proj_ab",
    )(x, shift, scale, rope_tab, w_in, q_norm, w_uq, w_uiq)


def _swa_kernel(sink_ref, q_ref, kp_ref, kc_ref, vp_ref, vc_ref, o_ref):
    j = pl.program_id(1)
    q = q_ref[0]
    kb = jnp.concatenate([kp_ref[0], kc_ref[0]], axis=0)
    vb = jnp.concatenate([vp_ref[0], vc_ref[0]], axis=0)
    qi = lax.broadcasted_iota(jnp.int32, (A_BLOCK, 2 * A_BLOCK), 0)
    c = lax.broadcasted_iota(jnp.int32, (A_BLOCK, 2 * A_BLOCK), 1)
    mask = (c > qi) & (c <= qi + A_BLOCK) & ((c >= A_BLOCK) | (j > 0))
    group = A_Q_HEADS // A_KV_HEADS
    outs = []
    for hq in range(A_Q_HEADS):
        kh = hq // group
        k_h = kb[:, kh * HEAD_DIM:(kh + 1) * HEAD_DIM]
        v_h = vb[:, kh * HEAD_DIM:(kh + 1) * HEAD_DIM]
        s = lax.dot_general(q[:, hq * HEAD_DIM:(hq + 1) * HEAD_DIM], k_h, _NT, preferred_element_type=F32)
        s = jnp.where(mask, s, NEG_INF)
        sink = sink_ref[hq] * LOG2E
        m = jnp.maximum(jnp.max(s, axis=-1, keepdims=True), sink)
        p = jnp.exp2(s - m)
        den = jnp.sum(p, axis=-1, keepdims=True) + jnp.exp2(sink - m)
        o = jnp.dot(p.astype(BF16), v_h, preferred_element_type=F32)
        outs.append(o / den)
    o_ref[0] = jnp.concatenate(outs, axis=-1).astype(BF16)


def _swa_attention(sinks, aq, ak, av):
    bsz, s, _ = aq.shape
    nb = s // A_BLOCK
    kvw = A_KV_HEADS * HEAD_DIM
    cur = pl.BlockSpec((1, A_BLOCK, kvw), lambda b, j: (b, j, 0))
    prev = pl.BlockSpec((1, A_BLOCK, kvw), lambda b, j: (b, jnp.maximum(j - 1, 0), 0))
    qspec = pl.BlockSpec((1, A_BLOCK, A_Q_HEADS * HEAD_DIM), lambda b, j: (b, j, 0))
    return pl.pallas_call(
        _swa_kernel,
        grid=(bsz, nb),
        in_specs=[pl.BlockSpec(memory_space=pltpu.SMEM), qspec, prev, cur, prev, cur],
        out_specs=qspec,
        out_shape=jax.ShapeDtypeStruct(aq.shape, BF16),
        compiler_params=_params(("parallel", "parallel")),
        name="swa_attention",
    )(sinks, aq, ak, ak, av, av)


DSA_TQ = 256
DSA_KC = 256


def _dsa_kernel(q_ref, iq_ref, iwt_ref, misc_ref, vt_ref, o_ref, key_scr, hi_scr, lo_scr, acc_scr):
    i = pl.program_id(1)
    tq, kc = DSA_TQ, DSA_KC
    nkc = i + 1
    krow = lax.broadcasted_iota(jnp.int32, (kc, tq), 0)
    qcol = lax.broadcasted_iota(jnp.int32, (kc, tq), 1)
    on_or_below_diag = krow <= qcol
    iwt = iwt_ref[0]
    hsl = [slice(h * HEAD_DIM, (h + 1) * HEAD_DIM) for h in range(B_Q_HEADS)]
    iqs = [iq_ref[0, :, hsl[h]] for h in range(B_IDX_HEADS)]
    qs = [q_ref[0, :, hsl[h]] for h in range(B_Q_HEADS)]

    def score_body(c, carry):
        k0 = pl.multiple_of(c * kc, kc)
        ik = misc_ref[0, pl.ds(k0, kc), 128:192]
        sc = jnp.zeros((kc, tq), F32)
        for h in range(B_IDX_HEADS):
            raw = lax.dot_general(ik, iqs[h], _NT, preferred_element_type=F32)
            sc = sc + jnp.maximum(raw, 0.0) * iwt[h:h + 1, :]
        sc = jnp.where(sc == 0.0, 0.0, sc)
        bits = lax.bitcast_convert_type(sc, jnp.int32)
        key = jnp.where(bits >= 0, bits, bits ^ jnp.int32(0x7FFFFFFF))
        key = jnp.where(c < i, key, jnp.where(on_or_below_diag, key, INT_MIN))
        key_scr[c] = key
        hi_scr[c] = (key >> 16).astype(jnp.int16)
        lo_scr[c] = ((key & 0xFFFF) - HALF).astype(jnp.int16)
        return carry

    lax.fori_loop(0, nkc, score_body, 0)

    def count_ge(ref, c, cand):
        ones = jnp.where(ref[c] >= cand.astype(jnp.int16), jnp.int16(1), jnp.int16(0))
        part = ones[0:16]
        for r in range(1, kc // 16):
            part = part + ones[16 * r:16 * (r + 1)]
        return jnp.sum(part.astype(jnp.int32), axis=0, keepdims=True)

    def half_search(ref, wanted):
        def step(it, thr):
            cand = thr + lax.shift_left(jnp.int32(1), 15 - it)
            cnt = lax.fori_loop(0, nkc, lambda c, acc: acc + count_ge(ref, c, cand),
                                jnp.zeros((1, tq), jnp.int32))
            return jnp.where(cnt >= wanted, cand, thr)

        return lax.fori_loop(0, 16, step, jnp.full((1, tq), -HALF, jnp.int32))

    thr_hi = half_search(hi_scr, B_TOPK)

    def mask_low(c, above):
        hi = hi_scr[c].astype(jnp.int32)
        lo_scr[c] = jnp.where(hi == thr_hi, lo_scr[c].astype(jnp.int32), -HALF).astype(jnp.int16)
        return above + jnp.sum(jnp.where(hi > thr_hi, 1, 0), axis=0, keepdims=True)

    above = lax.fori_loop(0, nkc, mask_low, jnp.zeros((1, tq), jnp.int32))
    thr_lo = half_search(lo_scr, B_TOPK - above)
    thr = (thr_hi << 16) | (thr_lo + HALF)

    def stats(c, carry):
        ngt, neq = carry
        k = key_scr[c]
        return (ngt + jnp.sum(jnp.where(k > thr, 1.0, 0.0), axis=0, keepdims=True),
                neq + jnp.sum(jnp.where(k == thr, 1.0, 0.0), axis=0, keepdims=True))

    zero_row = jnp.zeros((1, tq), F32)
    ngt, neq = lax.fori_loop(0, nkc, stats, (zero_row, zero_row))
    need = float(B_TOPK) - ngt

    @pl.when(jnp.max(neq - need) > 0.0)
    def _():
        ra = lax.broadcasted_iota(jnp.int32, (kc, kc), 0)
        rb = lax.broadcasted_iota(jnp.int32, (kc, kc), 1)
        lower = jnp.where(rb < ra, 1.0, 0.0).astype(BF16)

        def drop_late_ties(c, before):
            k = key_scr[c]
            eq = jnp.where(k == thr, 1.0, 0.0)
            prefix = jnp.dot(lower, eq.astype(BF16), preferred_element_type=F32) + before
            key_scr[c] = jnp.where(k == thr, jnp.where(prefix >= need, INT_MIN, k), k)
            return before + jnp.sum(eq, axis=0, keepdims=True)

        lax.fori_loop(0, nkc, drop_late_ties, zero_row)

    thr_sel = jnp.maximum(thr, INT_MIN + 1)

    acc_scr[...] = jnp.zeros_like(acc_scr)

    def attend(c, carry):
        ms, ls = carry
        k0 = pl.multiple_of(c * kc, kc)
        kk = misc_ref[0, pl.ds(k0, kc), 0:64]
        vt = vt_ref[0, :, pl.ds(k0, kc)]
        sel = key_scr[c] >= thr_sel
        ss = [jnp.where(sel, lax.dot_general(kk, qs[h], _NT, preferred_element_type=F32), MASK_NEG)
              for h in range(B_Q_HEADS)]
        ms_new, ls_new, alphas, ps = [], [], [], []
        for h in range(B_Q_HEADS):
            m_new = jnp.maximum(ms[h], jnp.max(ss[h], axis=0, keepdims=True))
            alpha = jnp.exp2(ms[h] - m_new)
            p = jnp.exp2(ss[h] - m_new)
            ls_new.append(alpha * ls[h] + jnp.sum(p, axis=0, keepdims=True))
            ms_new.append(m_new)
            alphas.append(alpha)
            ps.append(p.astype(BF16))
        for h in range(B_Q_HEADS):
            acc_scr[h] = alphas[h] * acc_scr[h] + jnp.dot(vt, ps[h], preferred_element_type=F32)
        return tuple(ms_new), tuple(ls_new)

    init = (tuple(jnp.full((1, tq), MASK_NEG, F32) for _ in range(B_Q_HEADS)),
            tuple(jnp.zeros((1, tq), F32) for _ in range(B_Q_HEADS)))
    _, ls = lax.fori_loop(0, nkc, attend, init)

    outs = []
    for h in range(0, B_Q_HEADS, 2):
        pair = jnp.concatenate([acc_scr[h] / ls[h], acc_scr[h + 1] / ls[h + 1]], axis=0)
        outs.append(pair.T)
    o_ref[0] = jnp.concatenate(outs, axis=-1).astype(BF16)


def _dsa_attention(bq, biq, iwt, misc, vt):
    bsz, s, _ = bq.shape
    tq = DSA_TQ
    qspec = pl.BlockSpec((1, tq, B_Q_HEADS * HEAD_DIM), lambda b, i: (b, i, 0))
    return pl.pallas_call(
        _dsa_kernel,
        grid=(bsz, s // tq),
        in_specs=[qspec, qspec,
                  pl.BlockSpec((1, B_IDX_HEADS, tq), lambda b, i: (b, 0, i)),
                  pl.BlockSpec((1, s, misc.shape[2]), lambda b, i: (b, 0, 0)),
                  pl.BlockSpec((1, HEAD_DIM, s), lambda b, i: (b, 0, 0))],
        out_specs=qspec,
        out_shape=jax.ShapeDtypeStruct(bq.shape, BF16),
        scratch_shapes=[pltpu.VMEM((s // DSA_KC, DSA_KC, tq), jnp.int32),
                        pltpu.VMEM((s // DSA_KC, DSA_KC, tq), jnp.int16),
                        pltpu.VMEM((s // DSA_KC, DSA_KC, tq), jnp.int16),
                        pltpu.VMEM((B_Q_HEADS, HEAD_DIM, tq), F32)],
        compiler_params=_params(("parallel", "arbitrary")),
        name="dsa_attention",
    )(bq, biq, iwt, misc, vt)


POST_TM = 512
POST_TF = 1024
POST_VMEM_LIMIT = 56 * 1024 * 1024


def _post_attn_kernel(*refs, widths):
    n = len(widths)
    parts = refs[:n]
    (wout_ref, x_ref, gate1_ref, g1_ref, b1_ref, shift2_ref, scale2_ref, gate2_ref,
     w1_ref, w2_ref, g2_ref, b2_ref, o_ref) = refs[n:]
    y = None
    off = 0
    for p_ref, wd in zip(parts, widths):
        t = jnp.dot(p_ref[0], wout_ref[off:off + wd, :], preferred_element_type=F32)
        y = t if y is None else y + t
        off += wd
    x1 = _layer_norm(DN_ALPHA * x_ref[0] + gate1_ref[0] * y, g1_ref[...], b1_ref[...])
    h = (x1 * (1.0 + scale2_ref[0]) + shift2_ref[0]).astype(BF16)
    acc = None
    for f in range(w1_ref.shape[1] // POST_TF):
        cols = slice(f * POST_TF, (f + 1) * POST_TF)
        u = jnp.maximum(jnp.dot(h, w1_ref[:, cols], preferred_element_type=F32), 0.0)
        t = jnp.dot((u * u).astype(BF16), w2_ref[cols, :], preferred_element_type=F32)
        acc = t if acc is None else acc + t
    o_ref[0] = _layer_norm(DN_ALPHA * x1 + gate2_ref[0] * acc, g2_ref[...], b2_ref[...])


def _post_attn(parts, w_out, x, gate1, g1, b1, shift2, scale2, gate2, w1, w2, g2, b2):
    bsz, s, d = x.shape
    tm = POST_TM
    widths = tuple(p.shape[-1] for p in parts)
    row = lambda w: pl.BlockSpec((1, tm, w), lambda bi, i: (bi, i, 0))
    vec = pl.BlockSpec((1, 1, d), lambda bi, i: (bi, 0, 0))
    full = lambda a: pl.BlockSpec(a.shape, lambda bi, i: (0,) * a.ndim, pipeline_mode=pl.Buffered(1))
    return pl.pallas_call(
        functools.partial(_post_attn_kernel, widths=widths),
        grid=(bsz, s // tm),
        in_specs=([row(w) for w in widths]
                  + [full(w_out), row(d), vec, full(g1), full(b1), vec, vec, vec,
                     full(w1), full(w2), full(g2), full(b2)]),
        out_specs=row(d),
        out_shape=jax.ShapeDtypeStruct(x.shape, F32),
        compiler_params=pltpu.CompilerParams(dimension_semantics=("parallel", "parallel"),
                                             vmem_limit_bytes=POST_VMEM_LIMIT),
        name="post_attn",
    )(*parts, w_out, x, gate1, g1, b1, shift2, scale2, gate2, w1, w2, g2, b2)


def _proj_c_kernel(x_ref, shift_ref, scale_ref, tab_ref, win_ref, q_ref, k_ref, vt_ref, km_ref):
    h = x_ref[0] * (1.0 + scale_ref[0]) + shift_ref[0]
    proj = jnp.dot(h.astype(BF16), win_ref[...], preferred_element_type=F32)
    cw = C_HEADS * HEAD_DIM
    n = cw // 128
    cos, sa, sb = (_tile_lanes(t, n) for t in _rope_coeffs(tab_ref[0]))
    q_ref[0] = (_rope(proj[:, 0:cw], cos, sa, sb) * ATT_SCALE).astype(BF16)
    k = _rope(proj[:, cw:2 * cw], cos, sa, sb)
    k_ref[0] = k.astype(BF16)
    vt_ref[0] = proj[:, 2 * cw:3 * cw].T.astype(BF16)
    km_ref[0, 0] = jnp.mean(k, axis=0, keepdims=True)


def _proj_c(x, shift, scale, rope_tab, w_in):
    bsz, s, d = x.shape
    tm = C_BLOCK
    cw = C_HEADS * HEAD_DIM
    row = lambda w: pl.BlockSpec((1, tm, w), lambda b, i: (b, i, 0))
    vec = pl.BlockSpec((1, 1, d), lambda b, i: (b, 0, 0))
    full = lambda a: pl.BlockSpec(a.shape, lambda b, i: (0,) * a.ndim)
    qkv = jax.ShapeDtypeStruct((bsz, s, cw), BF16)
    return pl.pallas_call(
        _proj_c_kernel,
        grid=(bsz, s // tm),
        in_specs=[row(d), vec, vec, row(128), full(w_in)],
        out_specs=[row(cw), row(cw), pl.BlockSpec((1, cw, tm), lambda b, i: (b, 0, i)),
                   pl.BlockSpec((1, 1, 1, cw), lambda b, i: (b, i, 0, 0))],
        out_shape=[qkv, qkv, jax.ShapeDtypeStruct((bsz, cw, s), BF16),
                   jax.ShapeDtypeStruct((bsz, s // tm, 1, cw), F32)],
        compiler_params=_params(("parallel", "parallel")),
        name="proj_c",
    )(x, shift, scale, rope_tab, w_in)


MOBA_HEADS_PER_STEP = 16


def _moba_kernel(q_ref, k_ref, vt_ref, km_ref, o_ref, bias_scr, acc_scr):
    i = pl.program_id(2)
    tq = q_ref.shape[1]
    nb = k_ref.shape[1] // C_BLOCK
    nh = MOBA_HEADS_PER_STEP
    hsl = [slice(hh * HEAD_DIM, (hh + 1) * HEAD_DIM) for hh in range(nh)]
    qs = [q_ref[0, :, hsl[hh]] for hh in range(nh)]

    n_idx = lax.broadcasted_iota(jnp.int32, (nb, tq), 0)
    for hh in range(nh):
        gate = lax.dot_general(km_ref[0, hh].astype(BF16), qs[hh], _NT, preferred_element_type=F32)
        cnt = jnp.zeros((nb, tq), F32)
        for m_idx in range(nb - 1):
            other = gate[m_idx:m_idx + 1, :]
            tie = jnp.where(n_idx > m_idx, 1.0, 0.0)
            beats = jnp.where(other > gate, 1.0, jnp.where(other == gate, tie, 0.0))
            cnt = cnt + jnp.where(m_idx < i, beats, 0.0)
        bias_scr[hh] = jnp.where(n_idx < i, jnp.where(cnt < float(C_TOPK), 0.0, MASK_NEG), MASK_NEG)

    def block_update(blk, mask, bias, ms, ls):
        k0 = pl.multiple_of(blk * C_BLOCK, C_BLOCK)
        ss = []
        for hh in range(nh):
            kj = k_ref[0, pl.ds(k0, C_BLOCK), hsl[hh]]
            ss.append(mask(lax.dot_general(kj, qs[hh], _NT, preferred_element_type=F32)))
        ms_new, ls_new, alphas, ps = [], [], [], []
        for hh in range(nh):
            b = bias(hh)
            m_new = jnp.maximum(ms[hh], jnp.max(ss[hh], axis=0, keepdims=True) + b)
            alpha = jnp.exp2(ms[hh] - m_new)
            p = jnp.exp2(ss[hh] - (m_new - b))
            ls_new.append(alpha * ls[hh] + jnp.sum(p, axis=0, keepdims=True))
            ms_new.append(m_new)
            alphas.append(alpha)
            ps.append(p.astype(BF16))
        for hh in range(nh):
            vj = vt_ref[0, hsl[hh], pl.ds(k0, C_BLOCK)]
            acc_scr[hh] = alphas[hh] * acc_scr[hh] + jnp.dot(vj, ps[hh], preferred_element_type=F32)
        return tuple(ms_new), tuple(ls_new)

    krow = lax.broadcasted_iota(jnp.int32, (C_BLOCK, tq), 0)
    qcol = lax.broadcasted_iota(jnp.int32, (C_BLOCK, tq), 1)
    causal = krow <= qcol
    acc_scr[...] = jnp.zeros_like(acc_scr)
    ms0 = tuple(jnp.full((1, tq), MASK_NEG, F32) for _ in range(nh))
    ls0 = tuple(jnp.zeros((1, tq), F32) for _ in range(nh))
    carry = block_update(i, lambda s: jnp.where(causal, s, MASK_NEG), lambda hh: 0.0, ms0, ls0)

    def past_block(j, carry):
        return block_update(j, lambda s: s, lambda hh: bias_scr[hh, pl.ds(j, 1), :], carry[0], carry[1])

    _, ls = lax.fori_loop(0, i, past_block, carry)

    outs = []
    for hh in range(0, nh, 2):
        pair = jnp.concatenate([acc_scr[hh] / ls[hh], acc_scr[hh + 1] / ls[hh + 1]], axis=0)
        outs.append(pair.T)
    o_ref[0] = jnp.concatenate(outs, axis=-1).astype(BF16)


def _moba_attention(q, k, vt, kmean):
    bsz, s, cw = q.shape
    tq = C_BLOCK
    nh = MOBA_HEADS_PER_STEP
    w = nh * HEAD_DIM
    nb = s // C_BLOCK
    qspec = pl.BlockSpec((1, tq, w), lambda b, hg, i: (b, i, hg))
    return pl.pallas_call(
        _moba_kernel,
        grid=(bsz, cw // w, s // tq),
        in_specs=[qspec,
                  pl.BlockSpec((1, s, w), lambda b, hg, i: (b, 0, hg)),
                  pl.BlockSpec((1, w, s), lambda b, hg, i: (b, hg, 0)),
                  pl.BlockSpec((1, nh, nb, HEAD_DIM), lambda b, hg, i: (b, hg, 0, 0))],
        out_specs=qspec,
        out_shape=jax.ShapeDtypeStruct(q.shape, BF16),
        scratch_shapes=[pltpu.VMEM((nh, nb, tq), F32), pltpu.VMEM((nh, HEAD_DIM, tq), F32)],
        compiler_params=_params(("parallel", "parallel", "arbitrary")),
        name="moba_attention",
    )(q, k, vt, kmean)


def _rope_table(positions):
    bsz, s = positions.shape
    half = HEAD_DIM // 2
    inv = ROPE_THETA ** (-jnp.arange(0, HEAD_DIM, 2, dtype=F32) / HEAD_DIM)
    ang = (positions.astype(F32)[..., None] * inv).reshape(-1, 128)
    cos, sin = lax.optimization_barrier((jnp.cos(ang), jnp.sin(ang)))
    cos, sin = cos.reshape(bsz, s, half), sin.reshape(bsz, s, half)
    return jnp.concatenate([cos, sin, cos, sin], axis=-1)


def kernel(x, c, positions, ab_w_in, ab_q_norm, ab_w_uq, ab_w_uiq, ab_sinks, ab_w_out, c_w_in, c_w_out,
           ada_w, ada_b, ln_g, ln_b, mlp_w1, mlp_w2):
    bsz, s, d = x.shape
    rope_tab = _rope_table(positions)
    mod = _ada_modulation(c, ada_w, ada_b)

    def mods(idx):
        m = mod[idx]
        return m[:, None, 0:d], m[:, None, d:2 * d], m[:, None, 2 * d:3 * d] + 1.0

    for layer in range(DEPTH):
        shift, scale, gate = mods(2 * layer)
        shift2, scale2, gate2 = mods(2 * layer + 1)
        post = lambda parts, w_out: _post_attn(
            parts, w_out.astype(BF16), x, gate, ln_g[layer, 0][None], ln_b[layer, 0][None],
            shift2, scale2, gate2, mlp_w1[layer].astype(BF16), mlp_w2[layer].astype(BF16),
            ln_g[layer, 1][None], ln_b[layer, 1][None])
        if layer % 2 == 0:
            e = layer // 2
            w_in = jnp.pad(ab_w_in[e], ((0, 0), (0, AB_IN_PAD - AB_IN_WIDTH))).astype(BF16)
            aq, ak, av, bq, biq, misc, iw = _proj_ab(
                x, shift, scale, rope_tab, w_in, ab_q_norm[e][None],
                ab_w_uq[e].astype(BF16), ab_w_uiq[e].astype(BF16))
            ya = _swa_attention(ab_sinks[e], aq, ak, av)
            vt = jnp.swapaxes(misc[:, :, 64:128], 1, 2)
            yb = _dsa_attention(bq, biq, jnp.swapaxes(iw, 1, 2), misc, vt)
            x = post([ya, yb], ab_w_out[e])
        else:
            o = layer // 2
            q, k, vt, kmean = _proj_c(x, shift, scale, rope_tab, c_w_in[o].astype(BF16))
            km = kmean.reshape(bsz, s // C_BLOCK, C_HEADS, HEAD_DIM).transpose(0, 2, 1, 3)
            y = _moba_attention(q, k, vt, km)
            x = post([y], c_w_out[o])
    return x
```

```python
import functools

import jax
import jax.numpy as jnp
from jax import lax
from jax.experimental import pallas as pl
from jax.experimental.pallas import tpu as pltpu

D_MODEL = 1024
HEAD_DIM = 64
ROPE_THETA = 10000.0
DEPTH = 2
A_Q_HEADS = 8
A_KV_HEADS = 2
A_BLOCK = 128
B_Q_HEADS = 8
B_Q_RANK = 256
B_IDX_HEADS = 8
B_TOPK = 256
B_QBLOCK = 128
C_HEADS = 16
C_BLOCK = 256
C_TOPK = 3
D_FF = 4 * D_MODEL
DN_ALPHA = (2 * DEPTH) ** 0.25
LN_EPS = 1e-5
RMS_EPS = 1e-6
AB_IN_WIDTH = 1224
AB_IN_PAD = 1280
LOG2E = 1.4426950408889634
ATT_SCALE = HEAD_DIM ** -0.5 * LOG2E
IDX_SCALE = B_IDX_HEADS ** -0.5 * HEAD_DIM ** -0.5

F32 = jnp.float32
BF16 = jnp.bfloat16
NEG_INF = float("-inf")
MASK_NEG = -1e30
INT_MIN = -(2 ** 31)
HALF = 2 ** 15
VMEM_LIMIT = 48 * 1024 * 1024

_NT = (((1,), (1,)), ((), ()))


def _params(sem):
    return pltpu.CompilerParams(dimension_semantics=sem, vmem_limit_bytes=VMEM_LIMIT)


def _rope(t, cos, sin_a, sin_b):
    w = t.shape[-1]
    return t * cos + pltpu.roll(t, 32, 1) * sin_a + pltpu.roll(t, w - 32, 1) * sin_b


def _rope_coeffs(tab):
    low = (lax.broadcasted_iota(jnp.int32, tab.shape, 1) & (HEAD_DIM - 1)) < HEAD_DIM // 2
    swapped = pltpu.roll(tab, HEAD_DIM // 2, 1)
    cos = jnp.where(low, tab, swapped)
    sin = jnp.where(low, swapped, tab)
    return cos, jnp.where(low, 0.0, sin), jnp.where(low, -sin, 0.0)


def _tile_lanes(t, n):
    return t if n == 1 else jnp.concatenate([t] * n, axis=-1)


def _layer_norm(z, g, b):
    mu = jnp.mean(z, axis=-1, keepdims=True)
    zc = z - mu
    var = jnp.mean(zc * zc, axis=-1, keepdims=True)
    return zc * lax.rsqrt(var + LN_EPS) * g + b


def _ada_kernel(c_ref, w_ref, b_ref, o_ref):
    c = c_ref[...]
    sc = c / (1.0 + jnp.exp(-c))
    o_ref[0] = jnp.dot(sc, w_ref[0], preferred_element_type=F32) + b_ref[0]


def _ada_modulation(c, ada_w, ada_b):
    n = ada_w.shape[0] * ada_w.shape[1]
    bsz, d = c.shape
    w = ada_w.reshape(n, d, 3 * d)
    b = ada_b.reshape(n, 1, 3 * d)
    tn = 1024
    return pl.pallas_call(
        _ada_kernel,
        grid=(n, 3 * d // tn),
        in_specs=[
            pl.BlockSpec((bsz, d), lambda i, j: (0, 0)),
            pl.BlockSpec((1, d, tn), lambda i, j: (i, 0, j)),
            pl.BlockSpec((1, 1, tn), lambda i, j: (i, 0, j)),
        ],
        out_specs=pl.BlockSpec((1, bsz, tn), lambda i, j: (i, 0, j)),
        out_shape=jax.ShapeDtypeStruct((n, bsz, 3 * d), F32),
        compiler_params=_params(("arbitrary", "arbitrary")),
        name="ada_modulation",
    )(c, w, b)


def _proj_ab_kernel(x_ref, shift_ref, scale_ref, tab_ref, win_ref, qn_ref, wuq_ref, wuiq_ref,
                    aq_ref, ak_ref, avt_ref, bq_ref, biq_ref, misc_ref, bvt_ref, iwt_ref):
    h = x_ref[0] * (1.0 + scale_ref[0]) + shift_ref[0]
    proj = jnp.dot(h.astype(BF16), win_ref[...], preferred_element_type=F32)
    cos, sa, sb = _rope_coeffs(tab_ref[0])
    cos4, sa4, sb4 = _tile_lanes(cos, 4), _tile_lanes(sa, 4), _tile_lanes(sb, 4)

    aq_ref[0] = (_rope(proj[:, 0:512], cos4, sa4, sb4) * ATT_SCALE).astype(BF16)
    ak_ref[0] = _rope(proj[:, 512:640], cos, sa, sb).astype(BF16)
    avt_ref[0] = proj[:, 640:768].T.astype(BF16)

    cq = proj[:, 768:1024]
    ms = jnp.mean(cq * cq, axis=-1, keepdims=True)
    cqn = (cq * lax.rsqrt(ms + RMS_EPS) * qn_ref[...]).astype(BF16)
    bq = jnp.dot(cqn, wuq_ref[...], preferred_element_type=F32)
    biq = jnp.dot(cqn, wuiq_ref[...], preferred_element_type=F32)
    bq_ref[0] = (_rope(bq, cos4, sa4, sb4) * ATT_SCALE).astype(BF16)
    biq_ref[0] = _rope(biq, cos4, sa4, sb4).astype(BF16)

    lane = lax.broadcasted_iota(jnp.int32, cos.shape, 1)
    roped = lane < HEAD_DIM
    c0 = jnp.where(roped, cos, 1.0)
    a0 = jnp.where(roped, sa, 0.0)
    b0 = jnp.where(roped, sb, 0.0)
    m0 = _rope(proj[:, 1024:1152], c0, a0, b0)
    m1 = _rope(proj[:, 1152:1280], c0, a0, b0)
    misc_ref[0] = jnp.concatenate([m0, m1], axis=-1).astype(BF16)
    bvt_ref[0] = m0.T[HEAD_DIM:2 * HEAD_DIM].astype(BF16)
    iwt_ref[0] = m1.T[HEAD_DIM:HEAD_DIM + B_IDX_HEADS] * IDX_SCALE


def _proj_ab(x, shift, scale, rope_tab, w_in, q_norm, w_uq, w_uiq, tm=512):
    bsz, s, d = x.shape
    row = lambda w: pl.BlockSpec((1, tm, w), lambda b, i: (b, i, 0))
    vec = pl.BlockSpec((1, 1, d), lambda b, i: (b, 0, 0))
    full = lambda a: pl.BlockSpec(a.shape, lambda b, i: (0,) * a.ndim)
    col = lambda r: pl.BlockSpec((1, r, tm), lambda b, i: (b, 0, i))
    tok = lambda w: jax.ShapeDtypeStruct((bsz, s, w), BF16)
    kvw = A_KV_HEADS * HEAD_DIM
    out_shape = [tok(512), tok(kvw), jax.ShapeDtypeStruct((bsz, kvw, s), BF16), tok(512), tok(512), tok(256),
                 jax.ShapeDtypeStruct((bsz, HEAD_DIM, s), BF16), jax.ShapeDtypeStruct((bsz, B_IDX_HEADS, s), F32)]
    out_specs = [row(512), row(kvw), col(kvw), row(512), row(512), row(256), col(HEAD_DIM), col(B_IDX_HEADS)]
    return pl.pallas_call(
        _proj_ab_kernel,
        grid=(bsz, s // tm),
        in_specs=[row(d), vec, vec, row(128), full(w_in), full(q_norm), full(w_uq), full(w_uiq)],
        out_specs=out_specs,
        out_shape=out_shape,
        compiler_params=_params(("parallel", "parallel")),
        name="proj_ab",
    )(x, shift, scale, rope_tab, w_in, q_norm, w_uq, w_uiq)


SWA_TQ = 2 * A_BLOCK


def _swa_kernel(sink_ref, q_ref, kp_ref, kc_ref, vtp_ref, vtc_ref, o_ref):
    i = pl.program_id(1)
    group = A_Q_HEADS // A_KV_HEADS
    kband = jnp.concatenate([kp_ref[0], kc_ref[0]], axis=0)
    vtband = jnp.concatenate([vtp_ref[0], vtc_ref[0]], axis=1)
    c = lax.broadcasted_iota(jnp.int32, (2 * A_BLOCK, A_BLOCK), 0)
    qi = lax.broadcasted_iota(jnp.int32, (2 * A_BLOCK, A_BLOCK), 1)
    in_window = (c > qi) & (c <= qi + A_BLOCK)
    masks = [in_window & ((c >= A_BLOCK) | (i > 0)), in_window]

    ss = {}
    for t in range(2):
        for kh in range(A_KV_HEADS):
            kb = kband[t * A_BLOCK:(t + 2) * A_BLOCK, kh * HEAD_DIM:(kh + 1) * HEAD_DIM]
            qstack = jnp.concatenate(
                [q_ref[0, t * A_BLOCK:(t + 1) * A_BLOCK, hq * HEAD_DIM:(hq + 1) * HEAD_DIM]
                 for hq in range(kh * group, (kh + 1) * group)], axis=0)
            s4 = lax.dot_general(kb, qstack, _NT, preferred_element_type=F32)
            for g in range(group):
                ss[t, kh * group + g] = jnp.where(masks[t], s4[:, g * A_BLOCK:(g + 1) * A_BLOCK], MASK_NEG)
    ps, dens = {}, {}
    for t in range(2):
        for hq in range(A_Q_HEADS):
            sink = sink_ref[hq] * LOG2E
            m = jnp.maximum(jnp.max(ss[t, hq], axis=0, keepdims=True), sink)
            p = jnp.exp2(ss[t, hq] - m)
            dens[t, hq] = jnp.sum(p, axis=0, keepdims=True) + jnp.exp2(sink - m)
            ps[t, hq] = p.astype(BF16)
    for t in range(2):
        outs = []
        for hq in range(A_Q_HEADS):
            kh = hq // group
            vt = vtband[kh * HEAD_DIM:(kh + 1) * HEAD_DIM, t * A_BLOCK:(t + 2) * A_BLOCK]
            outs.append(jnp.dot(vt, ps[t, hq], preferred_element_type=F32) / dens[t, hq])
        o_ref[0, t * A_BLOCK:(t + 1) * A_BLOCK, :] = jnp.concatenate(outs, axis=0).T.astype(BF16)


def _swa_attention(sinks, aq, ak, avt):
    bsz, s, qw = aq.shape
    kvw = A_KV_HEADS * HEAD_DIM
    prev_blk = lambda i: jnp.maximum(2 * i - 1, 0)
    qspec = pl.BlockSpec((1, SWA_TQ, qw), lambda b, i: (b, i, 0))
    return pl.pallas_call(
        _swa_kernel,
        grid=(bsz, s // SWA_TQ),
        in_specs=[pl.BlockSpec(memory_space=pltpu.SMEM), qspec,
                  pl.BlockSpec((1, A_BLOCK, kvw), lambda b, i: (b, prev_blk(i), 0)),
                  pl.BlockSpec((1, SWA_TQ, kvw), lambda b, i: (b, i, 0)),
                  pl.BlockSpec((1, kvw, A_BLOCK), lambda b, i: (b, 0, prev_blk(i))),
                  pl.BlockSpec((1, kvw, SWA_TQ), lambda b, i: (b, 0, i))],
        out_specs=qspec,
        out_shape=jax.ShapeDtypeStruct(aq.shape, BF16),
        compiler_params=_params(("parallel", "parallel")),
        name="swa_attention",
    )(sinks, aq, ak, ak, avt, avt)


DSA_TQ = 256
DSA_KC = 256


def _dsa_kernel(q_ref, iq_ref, iwt_ref, misc_ref, vt_ref, o_ref, key_scr, hi_scr, lo_scr, acc_scr):
    i = pl.program_id(1)
    tq, kc = DSA_TQ, DSA_KC
    nkc = i + 1
    krow = lax.broadcasted_iota(jnp.int32, (kc, tq), 0)
    qcol = lax.broadcasted_iota(jnp.int32, (kc, tq), 1)
    on_or_below_diag = krow <= qcol
    iwt = iwt_ref[0]
    hsl = [slice(h * HEAD_DIM, (h + 1) * HEAD_DIM) for h in range(B_Q_HEADS)]
    iqs = [iq_ref[0, :, hsl[h]] for h in range(B_IDX_HEADS)]
    qs = [q_ref[0, :, hsl[h]] for h in range(B_Q_HEADS)]

    def score_body(c, carry):
        k0 = pl.multiple_of(c * kc, kc)
        ik = misc_ref[0, pl.ds(k0, kc), 128:192]
        sc = jnp.zeros((kc, tq), F32)
        for h in range(B_IDX_HEADS):
            raw = lax.dot_general(ik, iqs[h], _NT, preferred_element_type=F32)
            sc = sc + jnp.maximum(raw, 0.0) * iwt[h:h + 1, :]
        sc = jnp.where(sc == 0.0, 0.0, sc)
        bits = lax.bitcast_convert_type(sc, jnp.int32)
        key = jnp.where(bits >= 0, bits, bits ^ jnp.int32(0x7FFFFFFF))
        key = jnp.where(c < i, key, jnp.where(on_or_below_diag, key, INT_MIN))
        key_scr[c] = key
        hi_scr[c] = (key >> 16).astype(jnp.int16)
        lo_scr[c] = ((key & 0xFFFF) - HALF).astype(jnp.int16)
        return carry

    lax.fori_loop(0, nkc, score_body, 0)

    npairs = (nkc + 1) // 2

    @pl.when(2 * npairs > nkc)
    def _():
        key_scr[nkc] = jnp.full((kc, tq), INT_MIN, jnp.int32)
        hi_scr[nkc] = jnp.full((kc, tq), -HALF, jnp.int16)
        lo_scr[nkc] = jnp.full((kc, tq), -HALF, jnp.int16)

    def count_ge(ref, c, cand):
        ones = jnp.where(ref[c] >= cand.astype(jnp.int16), jnp.int16(1), jnp.int16(0))
        part = ones[0:16]
        for r in range(1, kc // 16):
            part = part + ones[16 * r:16 * (r + 1)]
        return jnp.sum(part.astype(jnp.int32), axis=0, keepdims=True)

    def half_search(ref, wanted):
        def step(it, thr):
            cand = thr + lax.shift_left(jnp.int32(1), 15 - it)
            cnt = lax.fori_loop(
                0, npairs, lambda p, acc: acc + count_ge(ref, 2 * p, cand) + count_ge(ref, 2 * p + 1, cand),
                jnp.zeros((1, tq), jnp.int32))
            return jnp.where(cnt >= wanted, cand, thr)

        return lax.fori_loop(0, 16, step, jnp.full((1, tq), -HALF, jnp.int32))

    thr_hi = half_search(hi_scr, B_TOPK)

    def mask_low(c, above):
        hi = hi_scr[c].astype(jnp.int32)
        lo_scr[c] = jnp.where(hi == thr_hi, lo_scr[c].astype(jnp.int32), -HALF).astype(jnp.int16)
        return above + jnp.sum(jnp.where(hi > thr_hi, 1, 0), axis=0, keepdims=True)

    above = lax.fori_loop(0, nkc, mask_low, jnp.zeros((1, tq), jnp.int32))
    thr_lo = half_search(lo_scr, B_TOPK - above)
    thr = (thr_hi << 16) | (thr_lo + HALF)

    def stats(c, carry):
        ngt, neq = carry
        k = key_scr[c]
        return (ngt + jnp.sum(jnp.where(k > thr, 1.0, 0.0), axis=0, keepdims=True),
                neq + jnp.sum(jnp.where(k == thr, 1.0, 0.0), axis=0, keepdims=True))

    zero_row = jnp.zeros((1, tq), F32)
    ngt, neq = lax.fori_loop(0, nkc, stats, (zero_row, zero_row))
    need = float(B_TOPK) - ngt

    @pl.when(jnp.max(neq - need) > 0.0)
    def _():
        ra = lax.broadcasted_iota(jnp.int32, (kc, kc), 0)
        rb = lax.broadcasted_iota(jnp.int32, (kc, kc), 1)
        lower = jnp.where(rb < ra, 1.0, 0.0).astype(BF16)

        def drop_late_ties(c, before):
            k = key_scr[c]
            eq = jnp.where(k == thr, 1.0, 0.0)
            prefix = jnp.dot(lower, eq.astype(BF16), preferred_element_type=F32) + before
            key_scr[c] = jnp.where(k == thr, jnp.where(prefix >= need, INT_MIN, k), k)
            return before + jnp.sum(eq, axis=0, keepdims=True)

        lax.fori_loop(0, nkc, drop_late_ties, zero_row)

    thr_sel = jnp.maximum(thr, INT_MIN + 1)

    acc_scr[...] = jnp.zeros_like(acc_scr)

    def attend(p, carry):
        ms, ls = carry
        k0 = pl.multiple_of(p * 2 * kc, 2 * kc)
        kk = misc_ref[0, pl.ds(k0, 2 * kc), 0:64]
        vt = vt_ref[0, :, pl.ds(k0, 2 * kc)]
        sel = jnp.concatenate([key_scr[2 * p], key_scr[2 * p + 1]], axis=0) >= thr_sel
        ss = [jnp.where(sel, lax.dot_general(kk, qs[h], _NT, preferred_element_type=F32), MASK_NEG)
              for h in range(B_Q_HEADS)]
        ms_new, ls_new, alphas, ps = [], [], [], []
        for h in range(B_Q_HEADS):
            m_new = jnp.maximum(ms[h], jnp.max(ss[h], axis=0, keepdims=True))
            alpha = jnp.exp2(ms[h] - m_new)
            p = jnp.exp2(ss[h] - m_new)
            ls_new.append(alpha * ls[h] + jnp.sum(p, axis=0, keepdims=True))
            ms_new.append(m_new)
            alphas.append(alpha)
            ps.append(p.astype(BF16))
        for h in range(B_Q_HEADS):
            acc_scr[h] = alphas[h] * acc_scr[h] + jnp.dot(vt, ps[h], preferred_element_type=F32)
        return tuple(ms_new), tuple(ls_new)

    init = (tuple(jnp.full((1, tq), MASK_NEG, F32) for _ in range(B_Q_HEADS)),
            tuple(jnp.zeros((1, tq), F32) for _ in range(B_Q_HEADS)))
    _, ls = lax.fori_loop(0, npairs, attend, init)

    outs = []
    for h in range(0, B_Q_HEADS, 2):
        pair = jnp.concatenate([acc_scr[h] / ls[h], acc_scr[h + 1] / ls[h + 1]], axis=0)
        outs.append(pair.T)
    o_ref[0] = jnp.concatenate(outs, axis=-1).astype(BF16)


def _dsa_attention(bq, biq, iwt, misc, vt):
    bsz, s, _ = bq.shape
    tq = DSA_TQ
    qspec = pl.BlockSpec((1, tq, B_Q_HEADS * HEAD_DIM), lambda b, i: (b, i, 0))
    return pl.pallas_call(
        _dsa_kernel,
        grid=(bsz, s // tq),
        in_specs=[qspec, qspec,
                  pl.BlockSpec((1, B_IDX_HEADS, tq), lambda b, i: (b, 0, i)),
                  pl.BlockSpec((1, s, misc.shape[2]), lambda b, i: (b, 0, 0)),
                  pl.BlockSpec((1, HEAD_DIM, s), lambda b, i: (b, 0, 0))],
        out_specs=qspec,
        out_shape=jax.ShapeDtypeStruct(bq.shape, BF16),
        scratch_shapes=[pltpu.VMEM((s // DSA_KC, DSA_KC, tq), jnp.int32),
                        pltpu.VMEM((s // DSA_KC, DSA_KC, tq), jnp.int16),
                        pltpu.VMEM((s // DSA_KC, DSA_KC, tq), jnp.int16),
                        pltpu.VMEM((B_Q_HEADS, HEAD_DIM, tq), F32)],
        compiler_params=_params(("parallel", "arbitrary")),
        name="dsa_attention",
    )(bq, biq, iwt, misc, vt)


POST_TM = 512
POST_TF = 1024
POST_VMEM_LIMIT = 56 * 1024 * 1024


def _post_attn_kernel(*refs, widths):
    n = len(widths)
    parts = refs[:n]
    (wout_ref, x_ref, gate1_ref, g1_ref, b1_ref, shift2_ref, scale2_ref, gate2_ref,
     w1_ref, w2_ref, g2_ref, b2_ref, o_ref) = refs[n:]
    y = None
    off = 0
    for p_ref, wd in zip(parts, widths):
        t = jnp.dot(p_ref[0], wout_ref[off:off + wd, :], preferred_element_type=F32)
        y = t if y is None else y + t
        off += wd
    x1 = _layer_norm(DN_ALPHA * x_ref[0] + gate1_ref[0] * y, g1_ref[...], b1_ref[...])
    h = (x1 * (1.0 + scale2_ref[0]) + shift2_ref[0]).astype(BF16)
    acc = None
    for f in range(w1_ref.shape[1] // POST_TF):
        cols = slice(f * POST_TF, (f + 1) * POST_TF)
        u = jnp.maximum(jnp.dot(h, w1_ref[:, cols], preferred_element_type=F32), 0.0)
        t = jnp.dot((u * u).astype(BF16), w2_ref[cols, :], preferred_element_type=F32)
        acc = t if acc is None else acc + t
    o_ref[0] = _layer_norm(DN_ALPHA * x1 + gate2_ref[0] * acc, g2_ref[...], b2_ref[...])


def _post_attn(parts, w_out, x, gate1, g1, b1, shift2, scale2, gate2, w1, w2, g2, b2):
    bsz, s, d = x.shape
    tm = POST_TM
    widths = tuple(p.shape[-1] for p in parts)
    row = lambda w: pl.BlockSpec((1, tm, w), lambda bi, i: (bi, i, 0))
    vec = pl.BlockSpec((1, 1, d), lambda bi, i: (bi, 0, 0))
    full = lambda a: pl.BlockSpec(a.shape, lambda bi, i: (0,) * a.ndim, pipeline_mode=pl.Buffered(1))
    return pl.pallas_call(
        functools.partial(_post_attn_kernel, widths=widths),
        grid=(bsz, s // tm),
        in_specs=([row(w) for w in widths]
                  + [full(w_out), row(d), vec, full(g1), full(b1), vec, vec, vec,
                     full(w1), full(w2), full(g2), full(b2)]),
        out_specs=row(d),
        out_shape=jax.ShapeDtypeStruct(x.shape, F32),
        compiler_params=pltpu.CompilerParams(dimension_semantics=("parallel", "parallel"),
                                             vmem_limit_bytes=POST_VMEM_LIMIT),
        name="post_attn",
    )(*parts, w_out, x, gate1, g1, b1, shift2, scale2, gate2, w1, w2, g2, b2)


def _proj_c_kernel(x_ref, shift_ref, scale_ref, tab_ref, win_ref, q_ref, k_ref, vt_ref, km_ref):
    h = x_ref[0] * (1.0 + scale_ref[0]) + shift_ref[0]
    proj = jnp.dot(h.astype(BF16), win_ref[...], preferred_element_type=F32)
    cw = C_HEADS * HEAD_DIM
    n = cw // 128
    cos, sa, sb = (_tile_lanes(t, n) for t in _rope_coeffs(tab_ref[0]))
    q_ref[0] = (_rope(proj[:, 0:cw], cos, sa, sb) * ATT_SCALE).astype(BF16)
    k = _rope(proj[:, cw:2 * cw], cos, sa, sb)
    k_ref[0] = k.astype(BF16)
    vt_ref[0] = proj[:, 2 * cw:3 * cw].T.astype(BF16)
    km_ref[0, 0] = jnp.mean(k, axis=0, keepdims=True)


def _proj_c(x, shift, scale, rope_tab, w_in):
    bsz, s, d = x.shape
    tm = C_BLOCK
    cw = C_HEADS * HEAD_DIM
    row = lambda w: pl.BlockSpec((1, tm, w), lambda b, i: (b, i, 0))
    vec = pl.BlockSpec((1, 1, d), lambda b, i: (b, 0, 0))
    full = lambda a: pl.BlockSpec(a.shape, lambda b, i: (0,) * a.ndim)
    qkv = jax.ShapeDtypeStruct((bsz, s, cw), BF16)
    return pl.pallas_call(
        _proj_c_kernel,
        grid=(bsz, s // tm),
        in_specs=[row(d), vec, vec, row(128), full(w_in)],
        out_specs=[row(cw), row(cw), pl.BlockSpec((1, cw, tm), lambda b, i: (b, 0, i)),
                   pl.BlockSpec((1, 1, 1, cw), lambda b, i: (b, i, 0, 0))],
        out_shape=[qkv, qkv, jax.ShapeDtypeStruct((bsz, cw, s), BF16),
                   jax.ShapeDtypeStruct((bsz, s // tm, 1, cw), F32)],
        compiler_params=_params(("parallel", "parallel")),
        name="proj_c",
    )(x, shift, scale, rope_tab, w_in)


MOBA_HEADS_PER_STEP = 16


def _moba_kernel(q_ref, k_ref, vt_ref, km_ref, o_ref, bias_scr, acc_scr):
    i = pl.program_id(2)
    tq = q_ref.shape[1]
    nb = k_ref.shape[1] // C_BLOCK
    nh = MOBA_HEADS_PER_STEP
    hsl = [slice(hh * HEAD_DIM, (hh + 1) * HEAD_DIM) for hh in range(nh)]
    qs = [q_ref[0, :, hsl[hh]] for hh in range(nh)]

    n_idx = lax.broadcasted_iota(jnp.int32, (nb, tq), 0)
    for hh in range(nh):
        gate = lax.dot_general(km_ref[0, hh].astype(BF16), qs[hh], _NT, preferred_element_type=F32)
        cnt = jnp.zeros((nb, tq), F32)
        for m_idx in range(nb - 1):
            other = gate[m_idx:m_idx + 1, :]
            tie = jnp.where(n_idx > m_idx, 1.0, 0.0)
            beats = jnp.where(other > gate, 1.0, jnp.where(other == gate, tie, 0.0))
            cnt = cnt + jnp.where(m_idx < i, beats, 0.0)
        bias_scr[hh] = jnp.where(n_idx < i, jnp.where(cnt < float(C_TOPK), 0.0, MASK_NEG), MASK_NEG)

    def block_update(blk, mask, bias, ms, ls):
        k0 = pl.multiple_of(blk * C_BLOCK, C_BLOCK)
        ss = []
        for hh in range(nh):
            kj = k_ref[0, pl.ds(k0, C_BLOCK), hsl[hh]]
            ss.append(mask(lax.dot_general(kj, qs[hh], _NT, preferred_element_type=F32)))
        ms_new, ls_new, alphas, ps = [], [], [], []
        for hh in range(nh):
            b = bias(hh)
            m_new = jnp.maximum(ms[hh], jnp.max(ss[hh], axis=0, keepdims=True) + b)
            alpha = jnp.exp2(ms[hh] - m_new)
            p = jnp.exp2(ss[hh] - (m_new - b))
            ls_new.append(alpha * ls[hh] + jnp.sum(p, axis=0, keepdims=True))
            ms_new.append(m_new)
            alphas.append(alpha)
            ps.append(p.astype(BF16))
        for hh in range(nh):
            vj = vt_ref[0, hsl[hh], pl.ds(k0, C_BLOCK)]
            acc_scr[hh] = alphas[hh] * acc_scr[hh] + jnp.dot(vj, ps[hh], preferred_element_type=F32)
        return tuple(ms_new), tuple(ls_new)

    krow = lax.broadcasted_iota(jnp.int32, (C_BLOCK, tq), 0)
    qcol = lax.broadcasted_iota(jnp.int32, (C_BLOCK, tq), 1)
    causal = krow <= qcol
    acc_scr[...] = jnp.zeros_like(acc_scr)
    ms0 = tuple(jnp.full((1, tq), MASK_NEG, F32) for _ in range(nh))
    ls0 = tuple(jnp.zeros((1, tq), F32) for _ in range(nh))
    carry = block_update(i, lambda s: jnp.where(causal, s, MASK_NEG), lambda hh: 0.0, ms0, ls0)

    def past_block(j, carry):
        return block_update(j, lambda s: s, lambda hh: bias_scr[hh, pl.ds(j, 1), :], carry[0], carry[1])

    _, ls = lax.fori_loop(0, i, past_block, carry)

    outs = []
    for hh in range(0, nh, 2):
        pair = jnp.concatenate([acc_scr[hh] / ls[hh], acc_scr[hh + 1] / ls[hh + 1]], axis=0)
        outs.append(pair.T)
    o_ref[0] = jnp.concatenate(outs, axis=-1).astype(BF16)


def _moba_attention(q, k, vt, kmean):
    bsz, s, cw = q.shape
    tq = C_BLOCK
    nh = MOBA_HEADS_PER_STEP
    w = nh * HEAD_DIM
    nb = s // C_BLOCK
    qspec = pl.BlockSpec((1, tq, w), lambda b, hg, i: (b, i, hg))
    return pl.pallas_call(
        _moba_kernel,
        grid=(bsz, cw // w, s // tq),
        in_specs=[qspec,
                  pl.BlockSpec((1, s, w), lambda b, hg, i: (b, 0, hg)),
                  pl.BlockSpec((1, w, s), lambda b, hg, i: (b, hg, 0)),
                  pl.BlockSpec((1, nh, nb, HEAD_DIM), lambda b, hg, i: (b, hg, 0, 0))],
        out_specs=qspec,
        out_shape=jax.ShapeDtypeStruct(q.shape, BF16),
        scratch_shapes=[pltpu.VMEM((nh, nb, tq), F32), pltpu.VMEM((nh, HEAD_DIM, tq), F32)],
        compiler_params=_params(("parallel", "parallel", "arbitrary")),
        name="moba_attention",
    )(q, k, vt, kmean)


def _rope_table(positions):
    bsz, s = positions.shape
    half = HEAD_DIM // 2
    inv = ROPE_THETA ** (-jnp.arange(0, HEAD_DIM, 2, dtype=F32) / HEAD_DIM)
    ang = (positions.astype(F32)[..., None] * inv).reshape(-1, 128)
    cos, sin = lax.optimization_barrier((jnp.cos(ang), jnp.sin(ang)))
    cos, sin = cos.reshape(bsz, s, half), sin.reshape(bsz, s, half)
    return jnp.concatenate([cos, sin, cos, sin], axis=-1)


def kernel(x, c, positions, ab_w_in, ab_q_norm, ab_w_uq, ab_w_uiq, ab_sinks, ab_w_out, c_w_in, c_w_out,
           ada_w, ada_b, ln_g, ln_b, mlp_w1, mlp_w2):
    bsz, s, d = x.shape
    rope_tab = _rope_table(positions)
    mod = _ada_modulation(c, ada_w, ada_b)

    def mods(idx):
        m = mod[idx]
        return m[:, None, 0:d], m[:, None, d:2 * d], m[:, None, 2 * d:3 * d] + 1.0

    for layer in range(DEPTH):
        shift, scale, gate = mods(2 * layer)
        shift2, scale2, gate2 = mods(2 * layer + 1)
        post = lambda parts, w_out: _post_attn(
            parts, w_out.astype(BF16), x, gate, ln_g[layer, 0][None], ln_b[layer, 0][None],
            shift2, scale2, gate2, mlp_w1[layer].astype(BF16), mlp_w2[layer].astype(BF16),
            ln_g[layer, 1][None], ln_b[layer, 1][None])
        if layer % 2 == 0:
            e = layer // 2
            w_in = jnp.pad(ab_w_in[e], ((0, 0), (0, AB_IN_PAD - AB_IN_WIDTH))).astype(BF16)
            aq, ak, avt, bq, biq, misc, bvt, iwt = _proj_ab(
                x, shift, scale, rope_tab, w_in, ab_q_norm[e][None],
                ab_w_uq[e].astype(BF16), ab_w_uiq[e].astype(BF16))
            ya = _swa_attention(ab_sinks[e], aq, ak, avt)
            yb = _dsa_attention(bq, biq, iwt, misc, bvt)
            x = post([ya, yb], ab_w_out[e])
        else:
            o = layer // 2
            q, k, vt, kmean = _proj_c(x, shift, scale, rope_tab, c_w_in[o].astype(BF16))
            km = kmean.reshape(bsz, s // C_BLOCK, C_HEADS, HEAD_DIM).transpose(0, 2, 1, 3)
            y = _moba_attention(q, k, vt, km)
            x = post([y], c_w_out[o])
    return x
```

```python
import functools

import jax
import jax.numpy as jnp
import numpy as np
from jax import lax
from jax.experimental import pallas as pl
from jax.experimental.pallas import tpu as pltpu

D_MODEL = 1024
HEAD_DIM = 64
ROPE_THETA = 10000.0
DEPTH = 2
A_Q_HEADS = 8
A_KV_HEADS = 2
A_BLOCK = 128
B_Q_HEADS = 8
B_Q_RANK = 256
B_IDX_HEADS = 8
B_TOPK = 256
B_QBLOCK = 128
C_HEADS = 16
C_BLOCK = 256
C_TOPK = 3
D_FF = 4 * D_MODEL
DN_ALPHA = (2 * DEPTH) ** 0.25
LN_EPS = 1e-5
RMS_EPS = 1e-6
AB_IN_PAD = 1408
LOG2E = 1.4426950408889634
ATT_SCALE = HEAD_DIM ** -0.5 * LOG2E
IDX_SCALE = B_IDX_HEADS ** -0.5 * HEAD_DIM ** -0.5

F32 = jnp.float32
BF16 = jnp.bfloat16
MASK_NEG = -1e30
INT_MIN = -(2 ** 31)
HALF = 2 ** 15
VMEM_LIMIT = 48 * 1024 * 1024

_NT = (((1,), (1,)), ((), ()))


def _params(sem):
    return pltpu.CompilerParams(dimension_semantics=sem, vmem_limit_bytes=VMEM_LIMIT)


HALF_DIM = HEAD_DIM // 2


def _pair_columns(heads_a, heads_b):
    cols = []
    for a, b in zip(heads_a, heads_b):
        for part in (0, 1):
            for h in (a, b):
                cols.extend(range(h * HEAD_DIM + part * HALF_DIM, h * HEAD_DIM + (part + 1) * HALF_DIM))
    return np.asarray(cols, dtype=np.int32)


def _adjacent_pair_columns(n_heads):
    return _pair_columns(range(0, n_heads, 2), range(1, n_heads, 2))


def _head_of_pair(q_blk, second):
    lane = lax.broadcasted_iota(jnp.int32, q_blk.shape, 1)
    is_second = (lane & HALF_DIM) != 0
    return jnp.where(is_second if second else ~is_second, q_blk, jnp.zeros_like(q_blk))


def _rope(t, cos, sin_signed):
    outs = []
    for k in range(t.shape[-1] // 128):
        blk = t[:, 128 * k:128 * (k + 1)]
        outs.append(blk * cos + pltpu.roll(blk, 2 * HALF_DIM, 1) * sin_signed)
    return outs[0] if len(outs) == 1 else jnp.concatenate(outs, axis=-1)


def _rope_coeffs(tab):
    lane = lax.broadcasted_iota(jnp.int32, tab.shape, 1)
    low = (lane & (HEAD_DIM - 1)) < HALF_DIM
    swapped = pltpu.roll(tab, HALF_DIM, 1)
    cos = jnp.where(low, tab, swapped)
    sin = jnp.where(low, swapped, tab)
    return cos, jnp.where(lane < HEAD_DIM, -sin, sin)


def _layer_norm(z, g, b):
    mu = jnp.mean(z, axis=-1, keepdims=True)
    zc = z - mu
    var = jnp.mean(zc * zc, axis=-1, keepdims=True)
    return zc * lax.rsqrt(var + LN_EPS) * g + b


def _ada_kernel(c_ref, w_ref, b_ref, o_ref):
    c = c_ref[...]
    sc = c / (1.0 + jnp.exp(-c))
    o_ref[0] = jnp.dot(sc, w_ref[0], preferred_element_type=F32) + b_ref[0]


def _ada_modulation(c, ada_w, ada_b):
    n = ada_w.shape[0] * ada_w.shape[1]
    bsz, d = c.shape
    w = ada_w.reshape(n, d, 3 * d)
    b = ada_b.reshape(n, 1, 3 * d)
    tn = 1024
    return pl.pallas_call(
        _ada_kernel,
        grid=(n, 3 * d // tn),
        in_specs=[
            pl.BlockSpec((bsz, d), lambda i, j: (0, 0)),
            pl.BlockSpec((1, d, tn), lambda i, j: (i, 0, j)),
            pl.BlockSpec((1, 1, tn), lambda i, j: (i, 0, j)),
        ],
        out_specs=pl.BlockSpec((1, bsz, tn), lambda i, j: (i, 0, j)),
        out_shape=jax.ShapeDtypeStruct((n, bsz, 3 * d), F32),
        compiler_params=_params(("arbitrary", "arbitrary")),
        name="ada_modulation",
    )(c, w, b)


def _proj_ab_kernel(x_ref, shift_ref, scale_ref, tab_ref, win_ref, qn_ref, wuq_ref, wuiq_ref,
                    aq_ref, ak_ref, avt_ref, bq_ref, biq_ref, misc_ref, bvt_ref, iwt_ref):
    h = x_ref[0] * (1.0 + scale_ref[0]) + shift_ref[0]
    proj = jnp.dot(h.astype(BF16), win_ref[...], preferred_element_type=F32)
    cos, sin = _rope_coeffs(tab_ref[0])

    aq_ref[0] = (_rope(proj[:, 0:512], cos, sin) * ATT_SCALE).astype(BF16)
    ak_ref[0] = _rope(proj[:, 512:640], cos, sin).astype(BF16)
    avt_ref[0] = proj[:, 640:768].T.astype(BF16)

    cq = proj[:, 768:1024]
    ms = jnp.mean(cq * cq, axis=-1, keepdims=True)
    cqn = (cq * lax.rsqrt(ms + RMS_EPS) * qn_ref[...]).astype(BF16)
    bq = jnp.dot(cqn, wuq_ref[...], preferred_element_type=F32)
    biq = jnp.dot(cqn, wuiq_ref[...], preferred_element_type=F32)
    bq_ref[0] = (_rope(bq, cos, sin) * ATT_SCALE).astype(BF16)
    biq_ref[0] = _rope(biq, cos, sin).astype(BF16)

    misc_ref[0] = _rope(proj[:, 1024:1280], cos, sin).astype(BF16)
    tail_t = proj[:, 1280:1408].T
    bvt_ref[0] = tail_t[0:HEAD_DIM].astype(BF16)
    iwt_ref[0] = tail_t[HEAD_DIM:HEAD_DIM + B_IDX_HEADS] * IDX_SCALE


def _proj_ab(x, shift, scale, rope_tab, w_in, q_norm, w_uq, w_uiq, tm=512):
    bsz, s, d = x.shape
    row = lambda w: pl.BlockSpec((1, tm, w), lambda b, i: (b, i, 0))
    vec = pl.BlockSpec((1, 1, d), lambda b, i: (b, 0, 0))
    full = lambda a: pl.BlockSpec(a.shape, lambda b, i: (0,) * a.ndim)
    col = lambda r: pl.BlockSpec((1, r, tm), lambda b, i: (b, 0, i))
    tok = lambda w: jax.ShapeDtypeStruct((bsz, s, w), BF16)
    kvw = A_KV_HEADS * HEAD_DIM
    out_shape = [tok(512), tok(kvw), jax.ShapeDtypeStruct((bsz, kvw, s), BF16), tok(512), tok(512), tok(256),
                 jax.ShapeDtypeStruct((bsz, HEAD_DIM, s), BF16), jax.ShapeDtypeStruct((bsz, B_IDX_HEADS, s), F32)]
    out_specs = [row(512), row(kvw), col(kvw), row(512), row(512), row(256), col(HEAD_DIM), col(B_IDX_HEADS)]
    return pl.pallas_call(
        _proj_ab_kernel,
        grid=(bsz, s // tm),
        in_specs=[row(d), vec, vec, row(128), full(w_in), full(q_norm), full(w_uq), full(w_uiq)],
        out_specs=out_specs,
        out_shape=out_shape,
        compiler_params=_params(("parallel", "parallel")),
        name="proj_ab",
    )(x, shift, scale, rope_tab, w_in, q_norm, w_uq, w_uiq)


SWA_TQ = 2 * A_BLOCK


def _swa_kernel(sink_ref, q_ref, kp_ref, kc_ref, vtp_ref, vtc_ref, o_ref):
    i = pl.program_id(1)
    group = A_Q_HEADS // A_KV_HEADS
    kband = jnp.concatenate([kp_ref[0], kc_ref[0]], axis=0)
    vtband = jnp.concatenate([vtp_ref[0], vtc_ref[0]], axis=1)
    c = lax.broadcasted_iota(jnp.int32, (2 * A_BLOCK, A_BLOCK), 0)
    qi = lax.broadcasted_iota(jnp.int32, (2 * A_BLOCK, A_BLOCK), 1)
    in_window = (c > qi) & (c <= qi + A_BLOCK)
    masks = [in_window & ((c >= A_BLOCK) | (i > 0)), in_window]

    ss = {}
    for t in range(2):
        kb = kband[t * A_BLOCK:(t + 2) * A_BLOCK, :]
        for kh in range(A_KV_HEADS):
            qstack = jnp.concatenate(
                [_head_of_pair(q_ref[0, t * A_BLOCK:(t + 1) * A_BLOCK, g * 128:(g + 1) * 128], kh == 1)
                 for g in range(group)], axis=0)
            s4 = lax.dot_general(kb, qstack, _NT, preferred_element_type=F32)
            for g in range(group):
                ss[t, kh * group + g] = jnp.where(masks[t], s4[:, g * A_BLOCK:(g + 1) * A_BLOCK], MASK_NEG)
    ps, dens = {}, {}
    for t in range(2):
        for hq in range(A_Q_HEADS):
            sink = sink_ref[hq] * LOG2E
            m = jnp.maximum(jnp.max(ss[t, hq], axis=0, keepdims=True), sink)
            p = jnp.exp2(ss[t, hq] - m)
            dens[t, hq] = jnp.sum(p, axis=0, keepdims=True) + jnp.exp2(sink - m)
            ps[t, hq] = p.astype(BF16)
    for t in range(2):
        outs = []
        for hq in range(A_Q_HEADS):
            kh = hq // group
            vt = vtband[kh * HEAD_DIM:(kh + 1) * HEAD_DIM, t * A_BLOCK:(t + 2) * A_BLOCK]
            outs.append(jnp.dot(vt, ps[t, hq], preferred_element_type=F32) / dens[t, hq])
        o_ref[0, t * A_BLOCK:(t + 1) * A_BLOCK, :] = jnp.concatenate(outs, axis=0).T.astype(BF16)


def _swa_attention(sinks, aq, ak, avt):
    bsz, s, qw = aq.shape
    kvw = A_KV_HEADS * HEAD_DIM
    prev_blk = lambda i: jnp.maximum(2 * i - 1, 0)
    qspec = pl.BlockSpec((1, SWA_TQ, qw), lambda b, i: (b, i, 0))
    return pl.pallas_call(
        _swa_kernel,
        grid=(bsz, s // SWA_TQ),
        in_specs=[pl.BlockSpec(memory_space=pltpu.SMEM), qspec,
                  pl.BlockSpec((1, A_BLOCK, kvw), lambda b, i: (b, prev_blk(i), 0)),
                  pl.BlockSpec((1, SWA_TQ, kvw), lambda b, i: (b, i, 0)),
                  pl.BlockSpec((1, kvw, A_BLOCK), lambda b, i: (b, 0, prev_blk(i))),
                  pl.BlockSpec((1, kvw, SWA_TQ), lambda b, i: (b, 0, i))],
        out_specs=qspec,
        out_shape=jax.ShapeDtypeStruct(aq.shape, BF16),
        compiler_params=_params(("parallel", "parallel")),
        name="swa_attention",
    )(sinks, aq, ak, ak, avt, avt)


DSA_TQ = 256
DSA_KC = 256


def _dsa_kernel(q_ref, iq_ref, iwt_ref, misc_ref, vt_ref, o_ref, key_scr, hi_scr, lo_scr, acc_scr):
    i = pl.program_id(1)
    tq, kc = DSA_TQ, DSA_KC
    nkc = i + 1
    krow = lax.broadcasted_iota(jnp.int32, (kc, tq), 0)
    qcol = lax.broadcasted_iota(jnp.int32, (kc, tq), 1)
    on_or_below_diag = krow <= qcol
    iwt = iwt_ref[0]
    blk = lambda h: slice(128 * (h // 2), 128 * (h // 2 + 1))
    iqs = [_head_of_pair(iq_ref[0, :, blk(h)], h % 2 == 1) for h in range(B_IDX_HEADS)]
    qs = [_head_of_pair(q_ref[0, :, blk(h)], h % 2 == 1) for h in range(B_Q_HEADS)]

    def score_body(c, carry):
        k0 = pl.multiple_of(c * kc, kc)
        ik = misc_ref[0, pl.ds(k0, kc), 128:256]
        sc = jnp.zeros((kc, tq), F32)
        for h in range(B_IDX_HEADS):
            raw = lax.dot_general(ik, iqs[h], _NT, preferred_element_type=F32)
            sc = sc + jnp.maximum(raw, 0.0) * iwt[h:h + 1, :]
        sc = jnp.where(sc == 0.0, 0.0, sc)
        bits = lax.bitcast_convert_type(sc, jnp.int32)
        key = jnp.where(bits >= 0, bits, bits ^ jnp.int32(0x7FFFFFFF))
        key = jnp.where(c < i, key, jnp.where(on_or_below_diag, key, INT_MIN))
        key_scr[c] = key
        hi_scr[c] = (key >> 16).astype(jnp.int16)
        lo_scr[c] = ((key & 0xFFFF) - HALF).astype(jnp.int16)
        return carry

    lax.fori_loop(0, nkc, score_body, 0)

    npairs = (nkc + 1) // 2

    @pl.when(2 * npairs > nkc)
    def _():
        key_scr[nkc] = jnp.full((kc, tq), INT_MIN, jnp.int32)
        hi_scr[nkc] = jnp.full((kc, tq), -HALF, jnp.int16)
        lo_scr[nkc] = jnp.full((kc, tq), -HALF, jnp.int16)

    def count_ge(ref, c, cand):
        ones = jnp.where(ref[c] >= cand.astype(jnp.int16), jnp.int16(1), jnp.int16(0))
        part = ones[0:16]
        for r in range(1, kc // 16):
            part = part + ones[16 * r:16 * (r + 1)]
        return jnp.sum(part.astype(jnp.int32), axis=0, keepdims=True)

    def half_search(ref, wanted):
        def step(it, thr):
            cand = thr + lax.shift_left(jnp.int32(1), 15 - it)
            cnt = lax.fori_loop(
                0, npairs, lambda p, acc: acc + count_ge(ref, 2 * p, cand) + count_ge(ref, 2 * p + 1, cand),
                jnp.zeros((1, tq), jnp.int32))
            return jnp.where(cnt >= wanted, cand, thr)

        return lax.fori_loop(0, 16, step, jnp.full((1, tq), -HALF, jnp.int32))

    thr_hi = half_search(hi_scr, B_TOPK)

    def mask_low(c, above):
        hi = hi_scr[c].astype(jnp.int32)
        lo_scr[c] = jnp.where(hi == thr_hi, lo_scr[c].astype(jnp.int32), -HALF).astype(jnp.int16)
        return above + jnp.sum(jnp.where(hi > thr_hi, 1, 0), axis=0, keepdims=True)

    above = lax.fori_loop(0, nkc, mask_low, jnp.zeros((1, tq), jnp.int32))
    thr_lo = half_search(lo_scr, B_TOPK - above)
    thr = (thr_hi << 16) | (thr_lo + HALF)

    def stats(c, carry):
        ngt, neq = carry
        k = key_scr[c]
        return (ngt + jnp.sum(jnp.where(k > thr, 1.0, 0.0), axis=0, keepdims=True),
                neq + jnp.sum(jnp.where(k == thr, 1.0, 0.0), axis=0, keepdims=True))

    zero_row = jnp.zeros((1, tq), F32)
    ngt, neq = lax.fori_loop(0, nkc, stats, (zero_row, zero_row))
    need = float(B_TOPK) - ngt

    @pl.when(jnp.max(neq - need) > 0.0)
    def _():
        ra = lax.broadcasted_iota(jnp.int32, (kc, kc), 0)
        rb = lax.broadcasted_iota(jnp.int32, (kc, kc), 1)
        lower = jnp.where(rb < ra, 1.0, 0.0).astype(BF16)

        def drop_late_ties(c, before):
            k = key_scr[c]
            eq = jnp.where(k == thr, 1.0, 0.0)
            prefix = jnp.dot(lower, eq.astype(BF16), preferred_element_type=F32) + before
            key_scr[c] = jnp.where(k == thr, jnp.where(prefix >= need, INT_MIN, k), k)
            return before + jnp.sum(eq, axis=0, keepdims=True)

        lax.fori_loop(0, nkc, drop_late_ties, zero_row)

    thr_sel = jnp.maximum(thr, INT_MIN + 1)

    acc_scr[...] = jnp.zeros_like(acc_scr)

    def attend(p, carry):
        ms, ls = carry
        k0 = pl.multiple_of(p * 2 * kc, 2 * kc)
        kk = misc_ref[0, pl.ds(k0, 2 * kc), 0:128]
        vt = vt_ref[0, :, pl.ds(k0, 2 * kc)]
        sel = jnp.concatenate([key_scr[2 * p], key_scr[2 * p + 1]], axis=0) >= thr_sel
        ss = [jnp.where(sel, lax.dot_general(kk, qs[h], _NT, preferred_element_type=F32), MASK_NEG)
              for h in range(B_Q_HEADS)]
        ms_new, ls_new, alphas, ps = [], [], [], []
        for h in range(B_Q_HEADS):
            m_new = jnp.maximum(ms[h], jnp.max(ss[h], axis=0, keepdims=True))
            alpha = jnp.exp2(ms[h] - m_new)
            p = jnp.exp2(ss[h] - m_new)
            ls_new.append(alpha * ls[h] + jnp.sum(p, axis=0, keepdims=True))
            ms_new.append(m_new)
            alphas.append(alpha)
            ps.append(p.astype(BF16))
        for h in range(B_Q_HEADS):
            acc_scr[h] = alphas[h] * acc_scr[h] + jnp.dot(vt, ps[h], preferred_element_type=F32)
        return tuple(ms_new), tuple(ls_new)

    init = (tuple(jnp.full((1, tq), MASK_NEG, F32) for _ in range(B_Q_HEADS)),
            tuple(jnp.zeros((1, tq), F32) for _ in range(B_Q_HEADS)))
    _, ls = lax.fori_loop(0, npairs, attend, init)

    outs = []
    for h in range(0, B_Q_HEADS, 2):
        pair = jnp.concatenate([acc_scr[h] / ls[h], acc_scr[h + 1] / ls[h + 1]], axis=0)
        outs.append(pair.T)
    o_ref[0] = jnp.concatenate(outs, axis=-1).astype(BF16)


def _dsa_attention(bq, biq, iwt, misc, vt):
    bsz, s, _ = bq.shape
    tq = DSA_TQ
    qspec = pl.BlockSpec((1, tq, B_Q_HEADS * HEAD_DIM), lambda b, i: (b, i, 0))
    return pl.pallas_call(
        _dsa_kernel,
        grid=(bsz, s // tq),
        in_specs=[qspec, qspec,
                  pl.BlockSpec((1, B_IDX_HEADS, tq), lambda b, i: (b, 0, i)),
                  pl.BlockSpec((1, s, misc.shape[2]), lambda b, i: (b, 0, 0)),
                  pl.BlockSpec((1, HEAD_DIM, s), lambda b, i: (b, 0, 0))],
        out_specs=qspec,
        out_shape=jax.ShapeDtypeStruct(bq.shape, BF16),
        scratch_shapes=[pltpu.VMEM((s // DSA_KC, DSA_KC, tq), jnp.int32),
                        pltpu.VMEM((s // DSA_KC, DSA_KC, tq), jnp.int16),
                        pltpu.VMEM((s // DSA_KC, DSA_KC, tq), jnp.int16),
                        pltpu.VMEM((B_Q_HEADS, HEAD_DIM, tq), F32)],
        compiler_params=_params(("parallel", "arbitrary")),
        name="dsa_attention",
    )(bq, biq, iwt, misc, vt)


POST_TM = 512
POST_TF = 1024
POST_VMEM_LIMIT = 56 * 1024 * 1024


def _post_attn_kernel(*refs, widths):
    n = len(widths)
    parts = refs[:n]
    (wout_ref, x_ref, gate1_ref, g1_ref, b1_ref, shift2_ref, scale2_ref, gate2_ref,
     w1_ref, w2_ref, g2_ref, b2_ref, o_ref) = refs[n:]
    y = None
    off = 0
    for p_ref, wd in zip(parts, widths):
        t = jnp.dot(p_ref[0], wout_ref[off:off + wd, :], preferred_element_type=F32)
        y = t if y is None else y + t
        off += wd
    x1 = _layer_norm(DN_ALPHA * x_ref[0] + gate1_ref[0] * y, g1_ref[...], b1_ref[...])
    h = (x1 * (1.0 + scale2_ref[0]) + shift2_ref[0]).astype(BF16)
    acc = None
    for f in range(w1_ref.shape[1] // POST_TF):
        cols = slice(f * POST_TF, (f + 1) * POST_TF)
        u = jnp.maximum(jnp.dot(h, w1_ref[:, cols], preferred_element_type=F32), 0.0)
        t = jnp.dot((u * u).astype(BF16), w2_ref[cols, :], preferred_element_type=F32)
        acc = t if acc is None else acc + t
    o_ref[0] = _layer_norm(DN_ALPHA * x1 + gate2_ref[0] * acc, g2_ref[...], b2_ref[...])


def _post_attn(parts, w_out, x, gate1, g1, b1, shift2, scale2, gate2, w1, w2, g2, b2):
    bsz, s, d = x.shape
    tm = POST_TM
    widths = tuple(p.shape[-1] for p in parts)
    row = lambda w: pl.BlockSpec((1, tm, w), lambda bi, i: (bi, i, 0))
    vec = pl.BlockSpec((1, 1, d), lambda bi, i: (bi, 0, 0))
    full = lambda a: pl.BlockSpec(a.shape, lambda bi, i: (0,) * a.ndim, pipeline_mode=pl.Buffered(1))
    return pl.pallas_call(
        functools.partial(_post_attn_kernel, widths=widths),
        grid=(bsz, s // tm),
        in_specs=([row(w) for w in widths]
                  + [full(w_out), row(d), vec, full(g1), full(b1), vec, vec, vec,
                     full(w1), full(w2), full(g2), full(b2)]),
        out_specs=row(d),
        out_shape=jax.ShapeDtypeStruct(x.shape, F32),
        compiler_params=pltpu.CompilerParams(dimension_semantics=("parallel", "parallel"),
                                             vmem_limit_bytes=POST_VMEM_LIMIT),
        name="post_attn",
    )(*parts, w_out, x, gate1, g1, b1, shift2, scale2, gate2, w1, w2, g2, b2)


def _proj_c_kernel(x_ref, shift_ref, scale_ref, tab_ref, win_ref, q_ref, k_ref, vt_ref, km_ref):
    h = x_ref[0] * (1.0 + scale_ref[0]) + shift_ref[0]
    proj = jnp.dot(h.astype(BF16), win_ref[...], preferred_element_type=F32)
    cw = C_HEADS * HEAD_DIM
    cos, sin = _rope_coeffs(tab_ref[0])
    q_ref[0] = (_rope(proj[:, 0:cw], cos, sin) * ATT_SCALE).astype(BF16)
    k = _rope(proj[:, cw:2 * cw], cos, sin)
    k_ref[0] = k.astype(BF16)
    vt_ref[0] = proj[:, 2 * cw:3 * cw].T.astype(BF16)
    km_ref[0, 0] = jnp.mean(k, axis=0, keepdims=True)


def _proj_c(x, shift, scale, rope_tab, w_in):
    bsz, s, d = x.shape
    tm = C_BLOCK
    cw = C_HEADS * HEAD_DIM
    row = lambda w: pl.BlockSpec((1, tm, w), lambda b, i: (b, i, 0))
    vec = pl.BlockSpec((1, 1, d), lambda b, i: (b, 0, 0))
    full = lambda a: pl.BlockSpec(a.shape, lambda b, i: (0,) * a.ndim)
    qkv = jax.ShapeDtypeStruct((bsz, s, cw), BF16)
    return pl.pallas_call(
        _proj_c_kernel,
        grid=(bsz, s // tm),
        in_specs=[row(d), vec, vec, row(128), full(w_in)],
        out_specs=[row(cw), row(cw), pl.BlockSpec((1, cw, tm), lambda b, i: (b, 0, i)),
                   pl.BlockSpec((1, 1, 1, cw), lambda b, i: (b, i, 0, 0))],
        out_shape=[qkv, qkv, jax.ShapeDtypeStruct((bsz, cw, s), BF16),
                   jax.ShapeDtypeStruct((bsz, s // tm, 1, cw), F32)],
        compiler_params=_params(("parallel", "parallel")),
        name="proj_c",
    )(x, shift, scale, rope_tab, w_in)


MOBA_HEADS_PER_STEP = 16


def _moba_kernel(q_ref, k_ref, vt_ref, km_ref, o_ref, bias_scr, acc_scr):
    i = pl.program_id(2)
    tq = q_ref.shape[1]
    nb = k_ref.shape[1] // C_BLOCK
    nh = MOBA_HEADS_PER_STEP
    hsl = [slice(hh * HEAD_DIM, (hh + 1) * HEAD_DIM) for hh in range(nh)]
    pair_lanes = lambda hh: slice(128 * (hh // 2), 128 * (hh // 2 + 1))
    qs = [_head_of_pair(q_ref[0, :, pair_lanes(hh)], hh % 2 == 1) for hh in range(nh)]

    n_idx = lax.broadcasted_iota(jnp.int32, (nb, tq), 0)
    for hh in range(nh):
        gate = lax.dot_general(km_ref[0, hh // 2].astype(BF16), qs[hh], _NT,
                               preferred_element_type=F32)
        cnt = jnp.zeros((nb, tq), F32)
        for m_idx in range(nb - 1):
            other = gate[m_idx:m_idx + 1, :]
            tie = jnp.where(n_idx > m_idx, 1.0, 0.0)
            beats = jnp.where(other > gate, 1.0, jnp.where(other == gate, tie, 0.0))
            cnt = cnt + jnp.where(m_idx < i, beats, 0.0)
        bias_scr[hh] = jnp.where(n_idx < i, jnp.where(cnt < float(C_TOPK), 0.0, MASK_NEG), MASK_NEG)

    def block_update(blk, mask, bias, ms, ls):
        k0 = pl.multiple_of(blk * C_BLOCK, C_BLOCK)
        ss = []
        for hh in range(nh):
            kj = k_ref[0, pl.ds(k0, C_BLOCK), pair_lanes(hh)]
            ss.append(mask(lax.dot_general(kj, qs[hh], _NT, preferred_element_type=F32)))
        ms_new, ls_new, alphas, ps = [], [], [], []
        for hh in range(nh):
            b = bias(hh)
            m_new = jnp.maximum(ms[hh], jnp.max(ss[hh], axis=0, keepdims=True) + b)
            alpha = jnp.exp2(ms[hh] - m_new)
            p = jnp.exp2(ss[hh] - (m_new - b))
            ls_new.append(alpha * ls[hh] + jnp.sum(p, axis=0, keepdims=True))
            ms_new.append(m_new)
            alphas.append(alpha)
            ps.append(p.astype(BF16))
        for hh in range(nh):
            vj = vt_ref[0, hsl[hh], pl.ds(k0, C_BLOCK)]
            acc_scr[hh] = alphas[hh] * acc_scr[hh] + jnp.dot(vj, ps[hh], preferred_element_type=F32)
        return tuple(ms_new), tuple(ls_new)

    krow = lax.broadcasted_iota(jnp.int32, (C_BLOCK, tq), 0)
    qcol = lax.broadcasted_iota(jnp.int32, (C_BLOCK, tq), 1)
    causal = krow <= qcol
    acc_scr[...] = jnp.zeros_like(acc_scr)
    ms0 = tuple(jnp.full((1, tq), MASK_NEG, F32) for _ in range(nh))
    ls0 = tuple(jnp.zeros((1, tq), F32) for _ in range(nh))
    carry = block_update(i, lambda s: jnp.where(causal, s, MASK_NEG), lambda hh: 0.0, ms0, ls0)

    def past_block(j, carry):
        return block_update(j, lambda s: s, lambda hh: bias_scr[hh, pl.ds(j, 1), :], carry[0], carry[1])

    _, ls = lax.fori_loop(0, i, past_block, carry)

    outs = []
    for hh in range(0, nh, 2):
        pair = jnp.concatenate([acc_scr[hh] / ls[hh], acc_scr[hh + 1] / ls[hh + 1]], axis=0)
        outs.append(pair.T)
    o_ref[0] = jnp.concatenate(outs, axis=-1).astype(BF16)


def _moba_attention(q, k, vt, kmean):
    bsz, s, cw = q.shape
    tq = C_BLOCK
    nh = MOBA_HEADS_PER_STEP
    w = nh * HEAD_DIM
    nb = s // C_BLOCK
    qspec = pl.BlockSpec((1, tq, w), lambda b, hg, i: (b, i, hg))
    return pl.pallas_call(
        _moba_kernel,
        grid=(bsz, cw // w, s // tq),
        in_specs=[qspec,
                  pl.BlockSpec((1, s, w), lambda b, hg, i: (b, 0, hg)),
                  pl.BlockSpec((1, w, s), lambda b, hg, i: (b, hg, 0)),
                  pl.BlockSpec((1, nh // 2, nb, 128), lambda b, hg, i: (b, hg, 0, 0))],
        out_specs=qspec,
        out_shape=jax.ShapeDtypeStruct(q.shape, BF16),
        scratch_shapes=[pltpu.VMEM((nh, nb, tq), F32), pltpu.VMEM((nh, HEAD_DIM, tq), F32)],
        compiler_params=_params(("parallel", "parallel", "arbitrary")),
        name="moba_attention",
    )(q, k, vt, kmean)


def _rope_table(positions):
    inv = ROPE_THETA ** (-jnp.arange(0, HEAD_DIM, 2, dtype=F32) / HEAD_DIM)
    ang = positions.astype(F32)[..., None] * inv
    cos, sin = lax.optimization_barrier((jnp.cos(ang), jnp.sin(ang)))
    return jnp.concatenate([cos, sin, cos, sin], axis=-1)


def _ab_in_columns():
    aq, ak, av, cq, bk, bv, bik, biw = np.cumsum([0, 512, 128, 128, 256, 64, 64, 64]).tolist()
    kv_group = A_Q_HEADS // A_KV_HEADS
    same_head_twice = _pair_columns([0], [0])
    return np.concatenate([
        aq + _pair_columns(range(kv_group), range(kv_group, A_Q_HEADS)),
        ak + _adjacent_pair_columns(A_KV_HEADS),
        np.arange(av, bk, dtype=np.int32),
        bk + same_head_twice, bik + same_head_twice,
        np.arange(bv, bv + HEAD_DIM, dtype=np.int32), np.arange(biw, biw + B_IDX_HEADS, dtype=np.int32)])


def kernel(x, c, positions, ab_w_in, ab_q_norm, ab_w_uq, ab_w_uiq, ab_sinks, ab_w_out, c_w_in, c_w_out,
           ada_w, ada_b, ln_g, ln_b, mlp_w1, mlp_w2):
    bsz, s, d = x.shape
    rope_tab = _rope_table(positions)
    mod = _ada_modulation(c, ada_w, ada_b)

    def mods(idx):
        m = mod[idx]
        return m[:, None, 0:d], m[:, None, d:2 * d], m[:, None, 2 * d:3 * d] + 1.0

    for layer in range(DEPTH):
        shift, scale, gate = mods(2 * layer)
        shift2, scale2, gate2 = mods(2 * layer + 1)
        post = lambda parts, w_out: _post_attn(
            parts, w_out.astype(BF16), x, gate, ln_g[layer, 0][None], ln_b[layer, 0][None],
            shift2, scale2, gate2, mlp_w1[layer].astype(BF16), mlp_w2[layer].astype(BF16),
            ln_g[layer, 1][None], ln_b[layer, 1][None])
        if layer % 2 == 0:
            e = layer // 2
            w_in = ab_w_in[e][:, _ab_in_columns()]
            w_in = jnp.pad(w_in, ((0, 0), (0, AB_IN_PAD - w_in.shape[1]))).astype(BF16)
            q_cols = _adjacent_pair_columns(B_Q_HEADS)
            aq, ak, avt, bq, biq, misc, bvt, iwt = _proj_ab(
                x, shift, scale, rope_tab, w_in, ab_q_norm[e][None],
                ab_w_uq[e][:, q_cols].astype(BF16), ab_w_uiq[e][:, q_cols].astype(BF16))
            ya = _swa_attention(ab_sinks[e], aq, ak, avt)
            yb = _dsa_attention(bq, biq, iwt, misc, bvt)
            x = post([ya, yb], ab_w_out[e])
        else:
            o = layer // 2
            cw = C_HEADS * HEAD_DIM
            qk_cols = _adjacent_pair_columns(C_HEADS)
            w_in = jnp.concatenate([c_w_in[o][:, qk_cols], c_w_in[o][:, cw + qk_cols], c_w_in[o][:, 2 * cw:]], axis=1)
            q, k, vt, kmean = _proj_c(x, shift, scale, rope_tab, w_in.astype(BF16))
            km = kmean.reshape(bsz, s // C_BLOCK, C_HEADS // 2, 128).transpose(0, 2, 1, 3)
            y = _moba_attention(q, k, vt, km)
            x = post([y], c_w_out[o])
    return x
```

```python
import functools

import jax
import jax.numpy as jnp
from jax import lax
from jax.experimental import pallas as pl
from jax.experimental.pallas import tpu as pltpu

HEAD_DIM = 64
ROPE_THETA = 10000.0
DEPTH = 2
A_Q_HEADS = 8
A_KV_HEADS = 2
A_BLOCK = 128
B_Q_HEADS = 8
B_IDX_HEADS = 8
B_TOPK = 256
C_HEADS = 16
C_BLOCK = 256
C_TOPK = 3
DN_ALPHA = (2 * DEPTH) ** 0.25
LN_EPS = 1e-5
RMS_EPS = 1e-6
AB_IN_WIDTH = 1224
AB_IN_PAD = 1280
LOG2E = 1.4426950408889634
ATT_SCALE = HEAD_DIM ** -0.5 * LOG2E
IDX_SCALE = B_IDX_HEADS ** -0.5 * HEAD_DIM ** -0.5

F32 = jnp.float32
BF16 = jnp.bfloat16
MASK_NEG = -1e30
INT_MIN = -(2 ** 31)
HALF = 2 ** 15
VMEM_LIMIT = 48 * 1024 * 1024

_NT = (((1,), (1,)), ((), ()))


def _params(sem):
    return pltpu.CompilerParams(dimension_semantics=sem, vmem_limit_bytes=VMEM_LIMIT)


def _rope(t, cos, sin_a, sin_b):
    outs = []
    for k in range(t.shape[-1] // 128):
        blk = t[:, 128 * k:128 * (k + 1)]
        outs.append(blk * cos + pltpu.roll(blk, 32, 1) * sin_a + pltpu.roll(blk, 96, 1) * sin_b)
    return outs[0] if len(outs) == 1 else jnp.concatenate(outs, axis=-1)


def _rope_coeffs(tab_t):
    tab = tab_t.T
    low = (lax.broadcasted_iota(jnp.int32, tab.shape, 1) & (HEAD_DIM - 1)) < HEAD_DIM // 2
    swapped = pltpu.roll(tab, HEAD_DIM // 2, 1)
    cos = jnp.where(low, tab, swapped)
    sin = jnp.where(low, swapped, tab)
    return cos, jnp.where(low, 0.0, sin), jnp.where(low, -sin, 0.0)


def _layer_norm(z, g, b):
    mu = jnp.mean(z, axis=-1, keepdims=True)
    zc = z - mu
    var = jnp.mean(zc * zc, axis=-1, keepdims=True)
    return zc * lax.rsqrt(var + LN_EPS) * g + b


def _ada_kernel(c_ref, w_ref, b_ref, o_ref):
    c = c_ref[...]
    sc = c / (1.0 + jnp.exp(-c))
    o_ref[0] = jnp.dot(sc, w_ref[0], preferred_element_type=F32) + b_ref[0]


def _ada_modulation(c, ada_w, ada_b):
    n = ada_w.shape[0] * ada_w.shape[1]
    bsz, d = c.shape
    w = ada_w.reshape(n, d, 3 * d)
    b = ada_b.reshape(n, 1, 3 * d)
    tn = 1024
    return pl.pallas_call(
        _ada_kernel,
        grid=(n, 3 * d // tn),
        in_specs=[
            pl.BlockSpec((bsz, d), lambda i, j: (0, 0)),
            pl.BlockSpec((1, d, tn), lambda i, j: (i, 0, j)),
            pl.BlockSpec((1, 1, tn), lambda i, j: (i, 0, j)),
        ],
        out_specs=pl.BlockSpec((1, bsz, tn), lambda i, j: (i, 0, j)),
        out_shape=jax.ShapeDtypeStruct((n, bsz, 3 * d), F32),
        compiler_params=_params(("arbitrary", "arbitrary")),
        name="ada_modulation",
    )(c, w, b)


def _proj_ab_kernel(x_ref, shift_ref, scale_ref, tab_ref, win_ref, qn_ref, wuq_ref, wuiq_ref,
                    aq_ref, ak_ref, avt_ref, bq_ref, biq_ref, misc_ref, bvt_ref, iwt_ref):
    h = x_ref[0] * (1.0 + scale_ref[0]) + shift_ref[0]
    proj = jnp.dot(h.astype(BF16), win_ref[...], preferred_element_type=F32)
    cos, sa, sb = _rope_coeffs(tab_ref[0])

    aq_ref[0] = (_rope(proj[:, 0:512], cos, sa, sb) * ATT_SCALE).astype(BF16)
    ak_ref[0] = _rope(proj[:, 512:640], cos, sa, sb).astype(BF16)
    avt_ref[0] = proj[:, 640:768].T.astype(BF16)

    cq = proj[:, 768:1024]
    ms = jnp.mean(cq * cq, axis=-1, keepdims=True)
    cqn = (cq * lax.rsqrt(ms + RMS_EPS) * qn_ref[...]).astype(BF16)
    bq = jnp.dot(cqn, wuq_ref[...], preferred_element_type=F32)
    biq = jnp.dot(cqn, wuiq_ref[...], preferred_element_type=F32)
    bq_ref[0] = (_rope(bq, cos, sa, sb) * ATT_SCALE).astype(BF16)
    biq_ref[0] = _rope(biq, cos, sa, sb).astype(BF16)

    lane = lax.broadcasted_iota(jnp.int32, cos.shape, 1)
    roped = lane < HEAD_DIM
    c0 = jnp.where(roped, cos, 1.0)
    a0 = jnp.where(roped, sa, 0.0)
    b0 = jnp.where(roped, sb, 0.0)
    m0 = _rope(proj[:, 1024:1152], c0, a0, b0)
    m1 = _rope(proj[:, 1152:1280], c0, a0, b0)
    misc_ref[0] = jnp.concatenate([m0, m1], axis=-1).astype(BF16)
    bvt_ref[0] = m0.T[HEAD_DIM:2 * HEAD_DIM].astype(BF16)
    iwt_ref[0] = m1.T[HEAD_DIM:HEAD_DIM + B_IDX_HEADS] * IDX_SCALE


def _proj_ab(x, shift, scale, rope_tab, w_in, q_norm, w_uq, w_uiq, tm=512):
    bsz, s, d = x.shape
    row = lambda w: pl.BlockSpec((1, tm, w), lambda b, i: (b, i, 0))
    vec = pl.BlockSpec((1, 1, d), lambda b, i: (b, 0, 0))
    full = lambda a: pl.BlockSpec(a.shape, lambda b, i: (0,) * a.ndim)
    col = lambda r: pl.BlockSpec((1, r, tm), lambda b, i: (b, 0, i))
    tok = lambda w: jax.ShapeDtypeStruct((bsz, s, w), BF16)
    kvw = A_KV_HEADS * HEAD_DIM
    out_shape = [tok(512), tok(kvw), jax.ShapeDtypeStruct((bsz, kvw, s), BF16), tok(512), tok(512), tok(256),
                 jax.ShapeDtypeStruct((bsz, HEAD_DIM, s), BF16), jax.ShapeDtypeStruct((bsz, B_IDX_HEADS, s), F32)]
    out_specs = [row(512), row(kvw), col(kvw), row(512), row(512), row(256), col(HEAD_DIM), col(B_IDX_HEADS)]
    return pl.pallas_call(
        _proj_ab_kernel,
        grid=(bsz, s // tm),
        in_specs=[row(d), vec, vec, col(128), full(w_in), full(q_norm), full(w_uq), full(w_uiq)],
        out_specs=out_specs,
        out_shape=out_shape,
        compiler_params=_params(("parallel", "parallel")),
        name="proj_ab",
    )(x, shift, scale, rope_tab, w_in, q_norm, w_uq, w_uiq)


SWA_TQ = 2 * A_BLOCK


def _swa_kernel(sink_ref, q_ref, kp_ref, kc_ref, vtp_ref, vtc_ref, o_ref):
    i = pl.program_id(1)
    group = A_Q_HEADS // A_KV_HEADS
    kband = jnp.concatenate([kp_ref[0], kc_ref[0]], axis=0)
    vtband = jnp.concatenate([vtp_ref[0], vtc_ref[0]], axis=1)
    c = lax.broadcasted_iota(jnp.int32, (2 * A_BLOCK, A_BLOCK), 0)
    qi = lax.broadcasted_iota(jnp.int32, (2 * A_BLOCK, A_BLOCK), 1)
    in_window = (c > qi) & (c <= qi + A_BLOCK)
    masks = [in_window & ((c >= A_BLOCK) | (i > 0)), in_window]

    ss = {}
    for t in range(2):
        for kh in range(A_KV_HEADS):
            kb = kband[t * A_BLOCK:(t + 2) * A_BLOCK, kh * HEAD_DIM:(kh + 1) * HEAD_DIM]
            qstack = jnp.concatenate(
                [q_ref[0, t * A_BLOCK:(t + 1) * A_BLOCK, hq * HEAD_DIM:(hq + 1) * HEAD_DIM]
                 for hq in range(kh * group, (kh + 1) * group)], axis=0)
            s4 = lax.dot_general(kb, qstack, _NT, preferred_element_type=F32)
            for g in range(group):
                ss[t, kh * group + g] = jnp.where(masks[t], s4[:, g * A_BLOCK:(g + 1) * A_BLOCK], MASK_NEG)
    ps, dens = {}, {}
    for t in range(2):
        for hq in range(A_Q_HEADS):
            sink = sink_ref[hq] * LOG2E
            m = jnp.maximum(jnp.max(ss[t, hq], axis=0, keepdims=True), sink)
            p = jnp.exp2(ss[t, hq] - m)
            dens[t, hq] = jnp.sum(p, axis=0, keepdims=True) + jnp.exp2(sink - m)
            ps[t, hq] = p.astype(BF16)
    for t in range(2):
        outs = []
        for hq in range(A_Q_HEADS):
            kh = hq // group
            vt = vtband[kh * HEAD_DIM:(kh + 1) * HEAD_DIM, t * A_BLOCK:(t + 2) * A_BLOCK]
            outs.append(jnp.dot(vt, ps[t, hq], preferred_element_type=F32) / dens[t, hq])
        o_ref[0, t * A_BLOCK:(t + 1) * A_BLOCK, :] = jnp.concatenate(outs, axis=0).T.astype(BF16)


def _swa_attention(sinks, aq, ak, avt):
    bsz, s, qw = aq.shape
    kvw = A_KV_HEADS * HEAD_DIM
    prev_blk = lambda i: jnp.maximum(2 * i - 1, 0)
    qspec = pl.BlockSpec((1, SWA_TQ, qw), lambda b, i: (b, i, 0))
    return pl.pallas_call(
        _swa_kernel,
        grid=(bsz, s // SWA_TQ),
        in_specs=[pl.BlockSpec(memory_space=pltpu.SMEM), qspec,
                  pl.BlockSpec((1, A_BLOCK, kvw), lambda b, i: (b, prev_blk(i), 0)),
                  pl.BlockSpec((1, SWA_TQ, kvw), lambda b, i: (b, i, 0)),
                  pl.BlockSpec((1, kvw, A_BLOCK), lambda b, i: (b, 0, prev_blk(i))),
                  pl.BlockSpec((1, kvw, SWA_TQ), lambda b, i: (b, 0, i))],
        out_specs=qspec,
        out_shape=jax.ShapeDtypeStruct(aq.shape, BF16),
        compiler_params=_params(("parallel", "parallel")),
        name="swa_attention",
    )(sinks, aq, ak, ak, avt, avt)


DSA_TQ = 256
DSA_KC = 256


def _dsa_kernel(q_ref, iq_ref, iwt_ref, misc_ref, vt_ref, o_ref, key_scr, hi_scr, lo_scr, acc_scr):
    i = pl.program_id(1)
    tq, kc = DSA_TQ, DSA_KC
    nkc = i + 1
    krow = lax.broadcasted_iota(jnp.int32, (kc, tq), 0)
    qcol = lax.broadcasted_iota(jnp.int32, (kc, tq), 1)
    on_or_below_diag = krow <= qcol
    iwt = iwt_ref[0]
    hsl = [slice(h * HEAD_DIM, (h + 1) * HEAD_DIM) for h in range(B_Q_HEADS)]
    iqs = [iq_ref[0, :, hsl[h]] for h in range(B_IDX_HEADS)]
    qs = [q_ref[0, :, hsl[h]] for h in range(B_Q_HEADS)]

    def score_body(c, carry):
        k0 = pl.multiple_of(c * kc, kc)
        ik = misc_ref[0, pl.ds(k0, kc), 128:192]
        sc = jnp.zeros((kc, tq), F32)
        for h in range(B_IDX_HEADS):
            raw = lax.dot_general(ik, iqs[h], _NT, preferred_element_type=F32)
            sc = sc + jnp.maximum(raw, 0.0) * iwt[h:h + 1, :]
        sc = jnp.where(sc == 0.0, 0.0, sc)
        bits = lax.bitcast_convert_type(sc, jnp.int32)
        key = jnp.where(bits >= 0, bits, bits ^ jnp.int32(0x7FFFFFFF))
        key = jnp.where(c < i, key, jnp.where(on_or_below_diag, key, INT_MIN))
        key_scr[c] = key
        hi_scr[c] = (key >> 16).astype(jnp.int16)
        lo_scr[c] = ((key & 0xFFFF) - HALF).astype(jnp.int16)
        return carry

    lax.fori_loop(0, nkc, score_body, 0)

    npairs = (nkc + 1) // 2

    @pl.when(2 * npairs > nkc)
    def _():
        key_scr[nkc] = jnp.full((kc, tq), INT_MIN, jnp.int32)
        hi_scr[nkc] = jnp.full((kc, tq), -HALF, jnp.int16)
        lo_scr[nkc] = jnp.full((kc, tq), -HALF, jnp.int16)

    def count_ge(ref, c, cand):
        ones = jnp.where(ref[c] >= cand.astype(jnp.int16), jnp.int16(1), jnp.int16(0))
        part = ones[0:16]
        for r in range(1, kc // 16):
            part = part + ones[16 * r:16 * (r + 1)]
        return jnp.sum(part.astype(jnp.int32), axis=0, keepdims=True)

    def half_search(ref, wanted):
        def step(it, thr):
            cand = thr + lax.shift_left(jnp.int32(1), 15 - it)
            cnt = lax.fori_loop(
                0, npairs, lambda p, acc: acc + count_ge(ref, 2 * p, cand) + count_ge(ref, 2 * p + 1, cand),
                jnp.zeros((1, tq), jnp.int32))
            return jnp.where(cnt >= wanted, cand, thr)

        return lax.fori_loop(0, 16, step, jnp.full((1, tq), -HALF, jnp.int32))

    thr_hi = half_search(hi_scr, B_TOPK)

    def mask_low(c, above):
        hi = hi_scr[c].astype(jnp.int32)
        lo_scr[c] = jnp.where(hi == thr_hi, lo_scr[c].astype(jnp.int32), -HALF).astype(jnp.int16)
        return above + jnp.sum(jnp.where(hi > thr_hi, 1, 0), axis=0, keepdims=True)

    above = lax.fori_loop(0, nkc, mask_low, jnp.zeros((1, tq), jnp.int32))
    thr_lo = half_search(lo_scr, B_TOPK - above)
    thr = (thr_hi << 16) | (thr_lo + HALF)

    def stats(c, carry):
        ngt, neq = carry
        k = key_scr[c]
        return (ngt + jnp.sum(jnp.where(k > thr, 1.0, 0.0), axis=0, keepdims=True),
                neq + jnp.sum(jnp.where(k == thr, 1.0, 0.0), axis=0, keepdims=True))

    zero_row = jnp.zeros((1, tq), F32)
    ngt, neq = lax.fori_loop(0, nkc, stats, (zero_row, zero_row))
    need = float(B_TOPK) - ngt

    @pl.when(jnp.max(neq - need) > 0.0)
    def _():
        ra = lax.broadcasted_iota(jnp.int32, (kc, kc), 0)
        rb = lax.broadcasted_iota(jnp.int32, (kc, kc), 1)
        lower = jnp.where(rb < ra, 1.0, 0.0).astype(BF16)

        def drop_late_ties(c, before):
            k = key_scr[c]
            eq = jnp.where(k == thr, 1.0, 0.0)
            prefix = jnp.dot(lower, eq.astype(BF16), preferred_element_type=F32) + before
            key_scr[c] = jnp.where(k == thr, jnp.where(prefix >= need, INT_MIN, k), k)
            return before + jnp.sum(eq, axis=0, keepdims=True)

        lax.fori_loop(0, nkc, drop_late_ties, zero_row)

    thr_sel = jnp.maximum(thr, INT_MIN + 1)

    acc_scr[...] = jnp.zeros_like(acc_scr)

    def attend(p, carry):
        ms, ls = carry
        k0 = pl.multiple_of(p * 2 * kc, 2 * kc)
        kk = misc_ref[0, pl.ds(k0, 2 * kc), 0:64]
        vt = vt_ref[0, :, pl.ds(k0, 2 * kc)]
        sel = jnp.concatenate([key_scr[2 * p], key_scr[2 * p + 1]], axis=0) >= thr_sel
        ss = [jnp.where(sel, lax.dot_general(kk, qs[h], _NT, preferred_element_type=F32), MASK_NEG)
              for h in range(B_Q_HEADS)]
        ms_new, ls_new, alphas, ps = [], [], [], []
        for h in range(B_Q_HEADS):
            m_new = jnp.maximum(ms[h], jnp.max(ss[h], axis=0, keepdims=True))
            alpha = jnp.exp2(ms[h] - m_new)
            p = jnp.exp2(ss[h] - m_new)
            ls_new.append(alpha * ls[h] + jnp.sum(p, axis=0, keepdims=True))
            ms_new.append(m_new)
            alphas.append(alpha)
            ps.append(p.astype(BF16))
        for h in range(B_Q_HEADS):
            acc_scr[h] = alphas[h] * acc_scr[h] + jnp.dot(vt, ps[h], preferred_element_type=F32)
        return tuple(ms_new), tuple(ls_new)

    init = (tuple(jnp.full((1, tq), MASK_NEG, F32) for _ in range(B_Q_HEADS)),
            tuple(jnp.zeros((1, tq), F32) for _ in range(B_Q_HEADS)))
    _, ls = lax.fori_loop(0, npairs, attend, init)

    outs = []
    for h in range(0, B_Q_HEADS, 2):
        pair = jnp.concatenate([acc_scr[h] / ls[h], acc_scr[h + 1] / ls[h + 1]], axis=0)
        outs.append(pair.T)
    o_ref[0] = jnp.concatenate(outs, axis=-1).astype(BF16)


def _dsa_attention(bq, biq, iwt, misc, vt):
    bsz, s, _ = bq.shape
    tq = DSA_TQ
    qspec = pl.BlockSpec((1, tq, B_Q_HEADS * HEAD_DIM), lambda b, i: (b, i, 0))
    return pl.pallas_call(
        _dsa_kernel,
        grid=(bsz, s // tq),
        in_specs=[qspec, qspec,
                  pl.BlockSpec((1, B_IDX_HEADS, tq), lambda b, i: (b, 0, i)),
                  pl.BlockSpec((1, s, misc.shape[2]), lambda b, i: (b, 0, 0)),
                  pl.BlockSpec((1, HEAD_DIM, s), lambda b, i: (b, 0, 0))],
        out_specs=qspec,
        out_shape=jax.ShapeDtypeStruct(bq.shape, BF16),
        scratch_shapes=[pltpu.VMEM((s // DSA_KC, DSA_KC, tq), jnp.int32),
                        pltpu.VMEM((s // DSA_KC, DSA_KC, tq), jnp.int16),
                        pltpu.VMEM((s // DSA_KC, DSA_KC, tq), jnp.int16),
                        pltpu.VMEM((B_Q_HEADS, HEAD_DIM, tq), F32)],
        compiler_params=_params(("parallel", "arbitrary")),
        name="dsa_attention",
    )(bq, biq, iwt, misc, vt)


POST_TM = 512
POST_TF = 1024
POST_VMEM_LIMIT = 56 * 1024 * 1024


def _post_attn_kernel(*refs, widths):
    n = len(widths)
    parts = refs[:n]
    (wout_ref, x_ref, gate1_ref, g1_ref, b1_ref, shift2_ref, scale2_ref, gate2_ref,
     w1_ref, w2_ref, g2_ref, b2_ref, o_ref) = refs[n:]
    y = None
    off = 0
    for p_ref, wd in zip(parts, widths):
        t = jnp.dot(p_ref[0], wout_ref[off:off + wd, :], preferred_element_type=F32)
        y = t if y is None else y + t
        off += wd
    x1 = _layer_norm(DN_ALPHA * x_ref[0] + gate1_ref[0] * y, g1_ref[...], b1_ref[...])
    h = (x1 * (1.0 + scale2_ref[0]) + shift2_ref[0]).astype(BF16)
    acc = None
    for f in range(w1_ref.shape[1] // POST_TF):
        cols = slice(f * POST_TF, (f + 1) * POST_TF)
        u = jnp.maximum(jnp.dot(h, w1_ref[:, cols], preferred_element_type=F32), 0.0)
        t = jnp.dot((u * u).astype(BF16), w2_ref[cols, :], preferred_element_type=F32)
        acc = t if acc is None else acc + t
    o_ref[0] = _layer_norm(DN_ALPHA * x1 + gate2_ref[0] * acc, g2_ref[...], b2_ref[...])


def _post_attn(parts, w_out, x, gate1, g1, b1, shift2, scale2, gate2, w1, w2, g2, b2):
    bsz, s, d = x.shape
    tm = POST_TM
    widths = tuple(p.shape[-1] for p in parts)
    row = lambda w: pl.BlockSpec((1, tm, w), lambda bi, i: (bi, i, 0))
    vec = pl.BlockSpec((1, 1, d), lambda bi, i: (bi, 0, 0))
    full = lambda a: pl.BlockSpec(a.shape, lambda bi, i: (0,) * a.ndim, pipeline_mode=pl.Buffered(1))
    return pl.pallas_call(
        functools.partial(_post_attn_kernel, widths=widths),
        grid=(bsz, s // tm),
        in_specs=([row(w) for w in widths]
                  + [full(w_out), row(d), vec, full(g1), full(b1), vec, vec, vec,
                     full(w1), full(w2), full(g2), full(b2)]),
        out_specs=row(d),
        out_shape=jax.ShapeDtypeStruct(x.shape, F32),
        compiler_params=pltpu.CompilerParams(dimension_semantics=("parallel", "parallel"),
                                             vmem_limit_bytes=POST_VMEM_LIMIT),
        name="post_attn",
    )(*parts, w_out, x, gate1, g1, b1, shift2, scale2, gate2, w1, w2, g2, b2)


def _proj_c_kernel(x_ref, shift_ref, scale_ref, tab_ref, win_ref, q_ref, k_ref, vt_ref, km_ref):
    h = x_ref[0] * (1.0 + scale_ref[0]) + shift_ref[0]
    proj = jnp.dot(h.astype(BF16), win_ref[...], preferred_element_type=F32)
    cw = C_HEADS * HEAD_DIM
    cos, sa, sb = _rope_coeffs(tab_ref[0])
    q_ref[0] = (_rope(proj[:, 0:cw], cos, sa, sb) * ATT_SCALE).astype(BF16)
    k = _rope(proj[:, cw:2 * cw], cos, sa, sb)
    k_ref[0] = k.astype(BF16)
    vt_ref[0] = proj[:, 2 * cw:3 * cw].T.astype(BF16)
    km_ref[0, 0] = jnp.mean(k, axis=0, keepdims=True)


def _proj_c(x, shift, scale, rope_tab, w_in):
    bsz, s, d = x.shape
    tm = C_BLOCK
    cw = C_HEADS * HEAD_DIM
    row = lambda w: pl.BlockSpec((1, tm, w), lambda b, i: (b, i, 0))
    vec = pl.BlockSpec((1, 1, d), lambda b, i: (b, 0, 0))
    full = lambda a: pl.BlockSpec(a.shape, lambda b, i: (0,) * a.ndim)
    qkv = jax.ShapeDtypeStruct((bsz, s, cw), BF16)
    return pl.pallas_call(
        _proj_c_kernel,
        grid=(bsz, s // tm),
        in_specs=[row(d), vec, vec, pl.BlockSpec((1, 128, tm), lambda b, i: (b, 0, i)), full(w_in)],
        out_specs=[row(cw), row(cw), pl.BlockSpec((1, cw, tm), lambda b, i: (b, 0, i)),
                   pl.BlockSpec((1, 1, 1, cw), lambda b, i: (b, i, 0, 0))],
        out_shape=[qkv, qkv, jax.ShapeDtypeStruct((bsz, cw, s), BF16),
                   jax.ShapeDtypeStruct((bsz, s // tm, 1, cw), F32)],
        compiler_params=_params(("parallel", "parallel")),
        name="proj_c",
    )(x, shift, scale, rope_tab, w_in)


MOBA_HEADS_PER_STEP = 16


def _moba_kernel(q_ref, k_ref, vt_ref, km_ref, o_ref, bias_scr, acc_scr):
    i = pl.program_id(2)
    tq = q_ref.shape[1]
    nb = k_ref.shape[1] // C_BLOCK
    nh = MOBA_HEADS_PER_STEP
    hsl = [slice(hh * HEAD_DIM, (hh + 1) * HEAD_DIM) for hh in range(nh)]
    qs = [q_ref[0, :, hsl[hh]] for hh in range(nh)]

    n_idx = lax.broadcasted_iota(jnp.int32, (nb, tq), 0)
    for hh in range(nh):
        gate = lax.dot_general(km_ref[0, hh].astype(BF16), qs[hh], _NT, preferred_element_type=F32)
        cnt = jnp.zeros((nb, tq), F32)
        for m_idx in range(nb - 1):
            other = gate[m_idx:m_idx + 1, :]
            tie = jnp.where(n_idx > m_idx, 1.0, 0.0)
            beats = jnp.where(other > gate, 1.0, jnp.where(other == gate, tie, 0.0))
            cnt = cnt + jnp.where(m_idx < i, beats, 0.0)
        bias_scr[hh] = jnp.where(n_idx < i, jnp.where(cnt < float(C_TOPK), 0.0, MASK_NEG), MASK_NEG)

    def block_update(blk, mask, bias, ms, ls):
        k0 = pl.multiple_of(blk * C_BLOCK, C_BLOCK)
        ss = []
        for hh in range(nh):
            kj = k_ref[0, pl.ds(k0, C_BLOCK), hsl[hh]]
            ss.append(mask(lax.dot_general(kj, qs[hh], _NT, preferred_element_type=F32)))
        ms_new, ls_new, alphas, ps = [], [], [], []
        for hh in range(nh):
            b = bias(hh)
            m_new = jnp.maximum(ms[hh], jnp.max(ss[hh], axis=0, keepdims=True) + b)
            alpha = jnp.exp2(ms[hh] - m_new)
            p = jnp.exp2(ss[hh] - (m_new - b))
            ls_new.append(alpha * ls[hh] + jnp.sum(p, axis=0, keepdims=True))
            ms_new.append(m_new)
            alphas.append(alpha)
            ps.append(p.astype(BF16))
        for hh in range(nh):
            vj = vt_ref[0, hsl[hh], pl.ds(k0, C_BLOCK)]
            acc_scr[hh] = alphas[hh] * acc_scr[hh] + jnp.dot(vj, ps[hh], preferred_element_type=F32)
        return tuple(ms_new), tuple(ls_new)

    krow = lax.broadcasted_iota(jnp.int32, (C_BLOCK, tq), 0)
    qcol = lax.broadcasted_iota(jnp.int32, (C_BLOCK, tq), 1)
    causal = krow <= qcol
    acc_scr[...] = jnp.zeros_like(acc_scr)
    ms0 = tuple(jnp.full((1, tq), MASK_NEG, F32) for _ in range(nh))
    ls0 = tuple(jnp.zeros((1, tq), F32) for _ in range(nh))
    carry = block_update(i, lambda s: jnp.where(causal, s, MASK_NEG), lambda hh: 0.0, ms0, ls0)

    def past_block(j, carry):
        return block_update(j, lambda s: s, lambda hh: bias_scr[hh, pl.ds(j, 1), :], carry[0], carry[1])

    _, ls = lax.fori_loop(0, i, past_block, carry)

    outs = []
    for hh in range(0, nh, 2):
        pair = jnp.concatenate([acc_scr[hh] / ls[hh], acc_scr[hh + 1] / ls[hh + 1]], axis=0)
        outs.append(pair.T)
    o_ref[0] = jnp.concatenate(outs, axis=-1).astype(BF16)


def _moba_attention(q, k, vt, kmean):
    bsz, s, cw = q.shape
    tq = C_BLOCK
    nh = MOBA_HEADS_PER_STEP
    w = nh * HEAD_DIM
    nb = s // C_BLOCK
    qspec = pl.BlockSpec((1, tq, w), lambda b, hg, i: (b, i, hg))
    return pl.pallas_call(
        _moba_kernel,
        grid=(bsz, cw // w, s // tq),
        in_specs=[qspec,
                  pl.BlockSpec((1, s, w), lambda b, hg, i: (b, 0, hg)),
                  pl.BlockSpec((1, w, s), lambda b, hg, i: (b, hg, 0)),
                  pl.BlockSpec((1, nh, nb, HEAD_DIM), lambda b, hg, i: (b, hg, 0, 0))],
        out_specs=qspec,
        out_shape=jax.ShapeDtypeStruct(q.shape, BF16),
        scratch_shapes=[pltpu.VMEM((nh, nb, tq), F32), pltpu.VMEM((nh, HEAD_DIM, tq), F32)],
        compiler_params=_params(("parallel", "parallel", "arbitrary")),
        name="moba_attention",
    )(q, k, vt, kmean)


def _rope_table(positions):
    inv = ROPE_THETA ** (-jnp.arange(0, HEAD_DIM, 2, dtype=F32) / HEAD_DIM)
    ang = positions.astype(F32)[:, None, :] * inv[None, :, None]
    cos, sin = lax.optimization_barrier((jnp.cos(ang), jnp.sin(ang)))
    return jnp.concatenate([cos, sin, cos, sin], axis=1)


def kernel(x, c, positions, ab_w_in, ab_q_norm, ab_w_uq, ab_w_uiq, ab_sinks, ab_w_out, c_w_in, c_w_out,
           ada_w, ada_b, ln_g, ln_b, mlp_w1, mlp_w2):
    bsz, s, d = x.shape
    rope_tab = _rope_table(positions)
    mod = _ada_modulation(c, ada_w, ada_b)

    def mods(idx):
        m = mod[idx]
        return m[:, None, 0:d], m[:, None, d:2 * d], m[:, None, 2 * d:3 * d] + 1.0

    for layer in range(DEPTH):
        shift, scale, gate = mods(2 * layer)
        shift2, scale2, gate2 = mods(2 * layer + 1)
        post = lambda parts, w_out: _post_attn(
            parts, w_out.astype(BF16), x, gate, ln_g[layer, 0][None], ln_b[layer, 0][None],
            shift2, scale2, gate2, mlp_w1[layer].astype(BF16), mlp_w2[layer].astype(BF16),
            ln_g[layer, 1][None], ln_b[layer, 1][None])
        if layer % 2 == 0:
            e = layer // 2
            w_in = jnp.pad(ab_w_in[e], ((0, 0), (0, AB_IN_PAD - AB_IN_WIDTH))).astype(BF16)
            aq, ak, avt, bq, biq, misc, bvt, iwt = _proj_ab(
                x, shift, scale, rope_tab, w_in, ab_q_norm[e][None],
                ab_w_uq[e].astype(BF16), ab_w_uiq[e].astype(BF16))
            ya = _swa_attention(ab_sinks[e], aq, ak, avt)
            yb = _dsa_attention(bq, biq, iwt, misc, bvt)
            x = post([ya, yb], ab_w_out[e])
        else:
            o = layer // 2
            q, k, vt, kmean = _proj_c(x, shift, scale, rope_tab, c_w_in[o].astype(BF16))
            km = kmean.reshape(bsz, s // C_BLOCK, C_HEADS, HEAD_DIM).transpose(0, 2, 1, 3)
            y = _moba_attention(q, k, vt, km)
            x = post([y], c_w_out[o])
    return x
```

```python
import functools

import jax
import jax.numpy as jnp
from jax import lax
from jax.experimental import pallas as pl
from jax.experimental.pallas import tpu as pltpu

HEAD_DIM = 64
ROPE_THETA = 10000.0
DEPTH = 2
A_Q_HEADS = 8
A_KV_HEADS = 2
A_BLOCK = 128
B_Q_HEADS = 8
B_IDX_HEADS = 8
B_TOPK = 256
C_HEADS = 16
C_BLOCK = 256
C_TOPK = 3
DN_ALPHA = (2 * DEPTH) ** 0.25
LN_EPS = 1e-5
RMS_EPS = 1e-6
AB_IN_WIDTH = 1224
AB_IN_PAD = 1280
LOG2E = 1.4426950408889634
ATT_SCALE = HEAD_DIM ** -0.5 * LOG2E
IDX_SCALE = B_IDX_HEADS ** -0.5 * HEAD_DIM ** -0.5

F32 = jnp.float32
BF16 = jnp.bfloat16
MASK_NEG = -1e30
INT_MIN = -(2 ** 31)
HALF = 2 ** 15
VMEM_LIMIT = 48 * 1024 * 1024

_NT = (((1,), (1,)), ((), ()))


def _params(sem):
    return pltpu.CompilerParams(dimension_semantics=sem, vmem_limit_bytes=VMEM_LIMIT)


def _rope(t, cos, sin_a, sin_b):
    outs = []
    for k in range(t.shape[-1] // 128):
        blk = t[:, 128 * k:128 * (k + 1)]
        outs.append(blk * cos + pltpu.roll(blk, 32, 1) * sin_a + pltpu.roll(blk, 96, 1) * sin_b)
    return outs[0] if len(outs) == 1 else jnp.concatenate(outs, axis=-1)


def _rope_coeffs(tab_t):
    tab = tab_t.T
    low = (lax.broadcasted_iota(jnp.int32, tab.shape, 1) & (HEAD_DIM - 1)) < HEAD_DIM // 2
    swapped = pltpu.roll(tab, HEAD_DIM // 2, 1)
    cos = jnp.where(low, tab, swapped)
    sin = jnp.where(low, swapped, tab)
    return cos, jnp.where(low, 0.0, sin), jnp.where(low, -sin, 0.0)


def _layer_norm(z, g, b):
    mu = jnp.mean(z, axis=-1, keepdims=True)
    zc = z - mu
    var = jnp.mean(zc * zc, axis=-1, keepdims=True)
    return zc * lax.rsqrt(var + LN_EPS) * g + b


def _ada_kernel(c_ref, w_ref, b_ref, o_ref):
    c = c_ref[...]
    sc = c / (1.0 + jnp.exp(-c))
    o_ref[0] = jnp.dot(sc, w_ref[0], preferred_element_type=F32) + b_ref[0]


def _ada_modulation(c, ada_w, ada_b):
    n = ada_w.shape[0] * ada_w.shape[1]
    bsz, d = c.shape
    w = ada_w.reshape(n, d, 3 * d)
    b = ada_b.reshape(n, 1, 3 * d)
    tn = 1024
    return pl.pallas_call(
        _ada_kernel,
        grid=(n, 3 * d // tn),
        in_specs=[
            pl.BlockSpec((bsz, d), lambda i, j: (0, 0)),
            pl.BlockSpec((1, d, tn), lambda i, j: (i, 0, j)),
            pl.BlockSpec((1, 1, tn), lambda i, j: (i, 0, j)),
        ],
        out_specs=pl.BlockSpec((1, bsz, tn), lambda i, j: (i, 0, j)),
        out_shape=jax.ShapeDtypeStruct((n, bsz, 3 * d), F32),
        compiler_params=_params(("arbitrary", "arbitrary")),
        name="ada_modulation",
    )(c, w, b)


def _proj_ab_kernel(x_ref, shift_ref, scale_ref, tab_ref, win_ref, qn_ref, wuq_ref, wuiq_ref,
                    aq_ref, ak_ref, avt_ref, bq_ref, biq_ref, misc_ref, bvt_ref, iwt_ref):
    h = x_ref[0] * (1.0 + scale_ref[0]) + shift_ref[0]
    proj = jnp.dot(h.astype(BF16), win_ref[...], preferred_element_type=F32)
    cos, sa, sb = _rope_coeffs(tab_ref[0])

    aq_ref[0] = (_rope(proj[:, 0:512], cos, sa, sb) * ATT_SCALE).astype(BF16)
    ak_ref[0] = _rope(proj[:, 512:640], cos, sa, sb).astype(BF16)
    avt_ref[0] = proj[:, 640:768].T.astype(BF16)

    cq = proj[:, 768:1024]
    ms = jnp.mean(cq * cq, axis=-1, keepdims=True)
    cqn = (cq * lax.rsqrt(ms + RMS_EPS) * qn_ref[...]).astype(BF16)
    bq = jnp.dot(cqn, wuq_ref[...], preferred_element_type=F32)
    biq = jnp.dot(cqn, wuiq_ref[...], preferred_element_type=F32)
    bq_ref[0] = (_rope(bq, cos, sa, sb) * ATT_SCALE).astype(BF16)
    biq_ref[0] = _rope(biq, cos, sa, sb).astype(BF16)

    lane = lax.broadcasted_iota(jnp.int32, cos.shape, 1)
    roped = lane < HEAD_DIM
    c0 = jnp.where(roped, cos, 1.0)
    a0 = jnp.where(roped, sa, 0.0)
    b0 = jnp.where(roped, sb, 0.0)
    m0 = _rope(proj[:, 1024:1152], c0, a0, b0)
    m1 = _rope(proj[:, 1152:1280], c0, a0, b0)
    misc_ref[0] = jnp.concatenate([m0, m1], axis=-1).astype(BF16)
    bvt_ref[0] = m0.T[HEAD_DIM:2 * HEAD_DIM].astype(BF16)
    iwt_ref[0] = m1.T[HEAD_DIM:HEAD_DIM + B_IDX_HEADS] * IDX_SCALE


def _proj_ab(x, shift, scale, rope_tab, w_in, q_norm, w_uq, w_uiq, tm=512):
    bsz, s, d = x.shape
    row = lambda w: pl.BlockSpec((1, tm, w), lambda b, i: (b, i, 0))
    vec = pl.BlockSpec((1, 1, d), lambda b, i: (b, 0, 0))
    full = lambda a: pl.BlockSpec(a.shape, lambda b, i: (0,) * a.ndim)
    col = lambda r: pl.BlockSpec((1, r, tm), lambda b, i: (b, 0, i))
    tok = lambda w: jax.ShapeDtypeStruct((bsz, s, w), BF16)
    kvw = A_KV_HEADS * HEAD_DIM
    out_shape = [tok(512), tok(kvw), jax.ShapeDtypeStruct((bsz, kvw, s), BF16), tok(512), tok(512), tok(256),
                 jax.ShapeDtypeStruct((bsz, HEAD_DIM, s), BF16), jax.ShapeDtypeStruct((bsz, B_IDX_HEADS, s), F32)]
    out_specs = [row(512), row(kvw), col(kvw), row(512), row(512), row(256), col(HEAD_DIM), col(B_IDX_HEADS)]
    return pl.pallas_call(
        _proj_ab_kernel,
        grid=(bsz, s // tm),
        in_specs=[row(d), vec, vec, col(128), full(w_in), full(q_norm), full(w_uq), full(w_uiq)],
        out_specs=out_specs,
        out_shape=out_shape,
        compiler_params=_params(("parallel", "parallel")),
        name="proj_ab",
    )(x, shift, scale, rope_tab, w_in, q_norm, w_uq, w_uiq)


SWA_TQ = 2 * A_BLOCK


def _swa_kernel(sink_ref, q_ref, kp_ref, kc_ref, vtp_ref, vtc_ref, o_ref):
    i = pl.program_id(1)
    group = A_Q_HEADS // A_KV_HEADS
    kband = jnp.concatenate([kp_ref[0], kc_ref[0]], axis=0)
    vtband = jnp.concatenate([vtp_ref[0], vtc_ref[0]], axis=1)
    c = lax.broadcasted_iota(jnp.int32, (2 * A_BLOCK, A_BLOCK), 0)
    qi = lax.broadcasted_iota(jnp.int32, (2 * A_BLOCK, A_BLOCK), 1)
    in_window = (c > qi) & (c <= qi + A_BLOCK)
    masks = [jnp.where(in_window & ((c >= A_BLOCK) | (i > 0)), 0.0, MASK_NEG), jnp.where(in_window, 0.0, MASK_NEG)]

    ss = {}
    for t in range(2):
        for kh in range(A_KV_HEADS):
            kb = kband[t * A_BLOCK:(t + 2) * A_BLOCK, kh * HEAD_DIM:(kh + 1) * HEAD_DIM]
            qstack = jnp.concatenate(
                [q_ref[0, t * A_BLOCK:(t + 1) * A_BLOCK, hq * HEAD_DIM:(hq + 1) * HEAD_DIM]
                 for hq in range(kh * group, (kh + 1) * group)], axis=0)
            s4 = lax.dot_general(kb, qstack, _NT, preferred_element_type=F32)
            for g in range(group):
                ss[t, kh * group + g] = s4[:, g * A_BLOCK:(g + 1) * A_BLOCK] + masks[t]
    ps, dens = {}, {}
    for t in range(2):
        for hq in range(A_Q_HEADS):
            sink = sink_ref[hq] * LOG2E
            m = jnp.maximum(jnp.max(ss[t, hq], axis=0, keepdims=True), sink)
            p = jnp.exp2(ss[t, hq] - m)
            dens[t, hq] = jnp.sum(p, axis=0, keepdims=True) + jnp.exp2(sink - m)
            ps[t, hq] = p.astype(BF16)
    for t in range(2):
        outs = []
        for hq in range(A_Q_HEADS):
            kh = hq // group
            vt = vtband[kh * HEAD_DIM:(kh + 1) * HEAD_DIM, t * A_BLOCK:(t + 2) * A_BLOCK]
            outs.append(jnp.dot(vt, ps[t, hq], preferred_element_type=F32) / dens[t, hq])
        o_ref[0, t * A_BLOCK:(t + 1) * A_BLOCK, :] = jnp.concatenate(outs, axis=0).T.astype(BF16)


def _swa_attention(sinks, aq, ak, avt):
    bsz, s, qw = aq.shape
    kvw = A_KV_HEADS * HEAD_DIM
    prev_blk = lambda i: jnp.maximum(2 * i - 1, 0)
    qspec = pl.BlockSpec((1, SWA_TQ, qw), lambda b, i: (b, i, 0))
    return pl.pallas_call(
        _swa_kernel,
        grid=(bsz, s // SWA_TQ),
        in_specs=[pl.BlockSpec(memory_space=pltpu.SMEM), qspec,
                  pl.BlockSpec((1, A_BLOCK, kvw), lambda b, i: (b, prev_blk(i), 0)),
                  pl.BlockSpec((1, SWA_TQ, kvw), lambda b, i: (b, i, 0)),
                  pl.BlockSpec((1, kvw, A_BLOCK), lambda b, i: (b, 0, prev_blk(i))),
                  pl.BlockSpec((1, kvw, SWA_TQ), lambda b, i: (b, 0, i))],
        out_specs=qspec,
        out_shape=jax.ShapeDtypeStruct(aq.shape, BF16),
        compiler_params=_params(("parallel", "parallel")),
        name="swa_attention",
    )(sinks, aq, ak, ak, avt, avt)


DSA_TQ = 256
DSA_KC = 256


def _dsa_kernel(q_ref, iq_ref, iwt_ref, misc_ref, vt_ref, o_ref, key_scr, hi_scr, lo_scr, acc_scr):
    i = pl.program_id(1)
    tq, kc = DSA_TQ, DSA_KC
    nkc = i + 1
    krow = lax.broadcasted_iota(jnp.int32, (kc, tq), 0)
    qcol = lax.broadcasted_iota(jnp.int32, (kc, tq), 1)
    on_or_below_diag = krow <= qcol
    iwt = iwt_ref[0]
    hsl = [slice(h * HEAD_DIM, (h + 1) * HEAD_DIM) for h in range(B_Q_HEADS)]
    iqs = [iq_ref[0, :, hsl[h]] for h in range(B_IDX_HEADS)]
    qs = [q_ref[0, :, hsl[h]] for h in range(B_Q_HEADS)]

    def score_body(c, carry):
        k0 = pl.multiple_of(c * kc, kc)
        ik = misc_ref[0, pl.ds(k0, kc), 128:192]
        sc = jnp.zeros((kc, tq), F32)
        for h in range(B_IDX_HEADS):
            raw = lax.dot_general(ik, iqs[h], _NT, preferred_element_type=F32)
            sc = sc + jnp.maximum(raw, 0.0) * iwt[h:h + 1, :]
        sc = jnp.where(sc == 0.0, 0.0, sc)
        bits = lax.bitcast_convert_type(sc, jnp.int32)
        key = jnp.where(bits >= 0, bits, bits ^ jnp.int32(0x7FFFFFFF))
        key = jnp.where(c < i, key, jnp.where(on_or_below_diag, key, INT_MIN))
        key_scr[c] = key
        hi_scr[c] = (key >> 16).astype(jnp.int16)
        lo_scr[c] = ((key & 0xFFFF) - HALF).astype(jnp.int16)
        return carry

    lax.fori_loop(0, nkc, score_body, 0)

    npairs = (nkc + 1) // 2

    @pl.when(2 * npairs > nkc)
    def _():
        key_scr[nkc] = jnp.full((kc, tq), INT_MIN, jnp.int32)
        hi_scr[nkc] = jnp.full((kc, tq), -HALF, jnp.int16)
        lo_scr[nkc] = jnp.full((kc, tq), -HALF, jnp.int16)

    def count_ge(ref, c, cand):
        ones = jnp.where(ref[c] >= cand.astype(jnp.int16), jnp.int16(1), jnp.int16(0))
        part = ones[0:16]
        for r in range(1, kc // 16):
            part = part + ones[16 * r:16 * (r + 1)]
        return jnp.sum(part.astype(jnp.int32), axis=0, keepdims=True)

    def half_search(ref, wanted):
        def step(it, thr):
            cand = thr + lax.shift_left(jnp.int32(1), 15 - it)
            cnt = lax.fori_loop(
                0, npairs, lambda p, acc: acc + count_ge(ref, 2 * p, cand) + count_ge(ref, 2 * p + 1, cand),
                jnp.zeros((1, tq), jnp.int32))
            return jnp.where(cnt >= wanted, cand, thr)

        return lax.fori_loop(0, 16, step, jnp.full((1, tq), -HALF, jnp.int32))

    thr_hi = half_search(hi_scr, B_TOPK)

    def mask_low(c, above):
        hi = hi_scr[c].astype(jnp.int32)
        lo_scr[c] = jnp.where(hi == thr_hi, lo_scr[c].astype(jnp.int32), -HALF).astype(jnp.int16)
        return above + jnp.sum(jnp.where(hi > thr_hi, 1, 0), axis=0, keepdims=True)

    above = lax.fori_loop(0, nkc, mask_low, jnp.zeros((1, tq), jnp.int32))
    thr_lo = half_search(lo_scr, B_TOPK - above)
    thr = (thr_hi << 16) | (thr_lo + HALF)

    def stats(c, carry):
        ngt, neq = carry
        k = key_scr[c]
        return (ngt + jnp.sum(jnp.where(k > thr, 1.0, 0.0), axis=0, keepdims=True),
                neq + jnp.sum(jnp.where(k == thr, 1.0, 0.0), axis=0, keepdims=True))

    zero_row = jnp.zeros((1, tq), F32)
    ngt, neq = lax.fori_loop(0, nkc, stats, (zero_row, zero_row))
    need = float(B_TOPK) - ngt

    @pl.when(jnp.max(neq - need) > 0.0)
    def _():
        ra = lax.broadcasted_iota(jnp.int32, (kc, kc), 0)
        rb = lax.broadcasted_iota(jnp.int32, (kc, kc), 1)
        lower = jnp.where(rb < ra, 1.0, 0.0).astype(BF16)

        def drop_late_ties(c, before):
            k = key_scr[c]
            eq = jnp.where(k == thr, 1.0, 0.0)
            prefix = jnp.dot(lower, eq.astype(BF16), preferred_element_type=F32) + before
            key_scr[c] = jnp.where(k == thr, jnp.where(prefix >= need, INT_MIN, k), k)
            return before + jnp.sum(eq, axis=0, keepdims=True)

        lax.fori_loop(0, nkc, drop_late_ties, zero_row)

    thr_sel = jnp.maximum(thr, INT_MIN + 1)

    acc_scr[...] = jnp.zeros_like(acc_scr)

    def attend(p, carry):
        ms, ls = carry
        k0 = pl.multiple_of(p * 2 * kc, 2 * kc)
        kk = misc_ref[0, pl.ds(k0, 2 * kc), 0:64]
        vt = vt_ref[0, :, pl.ds(k0, 2 * kc)]
        sel = jnp.concatenate([key_scr[2 * p], key_scr[2 * p + 1]], axis=0) >= thr_sel
        sel_bias = jnp.where(sel, 0.0, MASK_NEG)
        ss = [lax.dot_general(kk, qs[h], _NT, preferred_element_type=F32) + sel_bias
              for h in range(B_Q_HEADS)]
        ms_new, ls_new, alphas, ps = [], [], [], []
        for h in range(B_Q_HEADS):
            m_new = jnp.maximum(ms[h], jnp.max(ss[h], axis=0, keepdims=True))
            alpha = jnp.exp2(ms[h] - m_new)
            p = jnp.exp2(ss[h] - m_new)
            ls_new.append(alpha * ls[h] + jnp.sum(p, axis=0, keepdims=True))
            ms_new.append(m_new)
            alphas.append(alpha)
            ps.append(p.astype(BF16))
        for h in range(B_Q_HEADS):
            acc_scr[h] = alphas[h] * acc_scr[h] + jnp.dot(vt, ps[h], preferred_element_type=F32)
        return tuple(ms_new), tuple(ls_new)

    init = (tuple(jnp.full((1, tq), MASK_NEG, F32) for _ in range(B_Q_HEADS)),
            tuple(jnp.zeros((1, tq), F32) for _ in range(B_Q_HEADS)))
    _, ls = lax.fori_loop(0, npairs, attend, init)

    outs = []
    for h in range(0, B_Q_HEADS, 2):
        pair = jnp.concatenate([acc_scr[h] / ls[h], acc_scr[h + 1] / ls[h + 1]], axis=0)
        outs.append(pair.T)
    o_ref[0] = jnp.concatenate(outs, axis=-1).astype(BF16)


def _dsa_attention(bq, biq, iwt, misc, vt):
    bsz, s, _ = bq.shape
    tq = DSA_TQ
    qspec = pl.BlockSpec((1, tq, B_Q_HEADS * HEAD_DIM), lambda b, i: (b, i, 0))
    return pl.pallas_call(
        _dsa_kernel,
        grid=(bsz, s // tq),
        in_specs=[qspec, qspec,
                  pl.BlockSpec((1, B_IDX_HEADS, tq), lambda b, i: (b, 0, i)),
                  pl.BlockSpec((1, s, misc.shape[2]), lambda b, i: (b, 0, 0)),
                  pl.BlockSpec((1, HEAD_DIM, s), lambda b, i: (b, 0, 0))],
        out_specs=qspec,
        out_shape=jax.ShapeDtypeStruct(bq.shape, BF16),
        scratch_shapes=[pltpu.VMEM((s // DSA_KC, DSA_KC, tq), jnp.int32),
                        pltpu.VMEM((s // DSA_KC, DSA_KC, tq), jnp.int16),
                        pltpu.VMEM((s // DSA_KC, DSA_KC, tq), jnp.int16),
                        pltpu.VMEM((B_Q_HEADS, HEAD_DIM, tq), F32)],
        compiler_params=_params(("parallel", "arbitrary")),
        name="dsa_attention",
    )(bq, biq, iwt, misc, vt)


POST_TM = 1024
POST_ROWS = 512
POST_TF = 1024
POST_VMEM_LIMIT = 60 * 1024 * 1024


def _post_attn_kernel(*refs, widths):
    n = len(widths)
    parts = refs[:n]
    (wout_ref, x_ref, gate1_ref, g1_ref, b1_ref, shift2_ref, scale2_ref, gate2_ref,
     w1_ref, w2_ref, g2_ref, b2_ref, o_ref) = refs[n:]
    for r in range(x_ref.shape[1] // POST_ROWS):
        rows = slice(r * POST_ROWS, (r + 1) * POST_ROWS)
        y = None
        off = 0
        for p_ref, wd in zip(parts, widths):
            t = jnp.dot(p_ref[0, rows, :], wout_ref[off:off + wd, :], preferred_element_type=F32)
            y = t if y is None else y + t
            off += wd
        x1 = _layer_norm(DN_ALPHA * x_ref[0, rows, :] + gate1_ref[0] * y, g1_ref[...], b1_ref[...])
        h = (x1 * (1.0 + scale2_ref[0]) + shift2_ref[0]).astype(BF16)
        acc = None
        for f in range(w1_ref.shape[1] // POST_TF):
            cols = slice(f * POST_TF, (f + 1) * POST_TF)
            u = jnp.maximum(jnp.dot(h, w1_ref[:, cols], preferred_element_type=F32), 0.0)
            t = jnp.dot((u * u).astype(BF16), w2_ref[cols, :], preferred_element_type=F32)
            acc = t if acc is None else acc + t
        o_ref[0, rows, :] = _layer_norm(DN_ALPHA * x1 + gate2_ref[0] * acc, g2_ref[...], b2_ref[...])


def _post_attn(parts, w_out, x, gate1, g1, b1, shift2, scale2, gate2, w1, w2, g2, b2):
    bsz, s, d = x.shape
    tm = POST_TM
    widths = tuple(p.shape[-1] for p in parts)
    row = lambda w: pl.BlockSpec((1, tm, w), lambda bi, i: (bi, i, 0))
    vec = pl.BlockSpec((1, 1, d), lambda bi, i: (bi, 0, 0))
    full = lambda a: pl.BlockSpec(a.shape, lambda bi, i: (0,) * a.ndim, pipeline_mode=pl.Buffered(1))
    return pl.pallas_call(
        functools.partial(_post_attn_kernel, widths=widths),
        grid=(bsz, s // tm),
        in_specs=([row(w) for w in widths]
                  + [full(w_out), row(d), vec, full(g1), full(b1), vec, vec, vec,
                     full(w1), full(w2), full(g2), full(b2)]),
        out_specs=row(d),
        out_shape=jax.ShapeDtypeStruct(x.shape, F32),
        compiler_params=pltpu.CompilerParams(dimension_semantics=("parallel", "parallel"),
                                             vmem_limit_bytes=POST_VMEM_LIMIT),
        name="post_attn",
    )(*parts, w_out, x, gate1, g1, b1, shift2, scale2, gate2, w1, w2, g2, b2)


def _proj_c_kernel(x_ref, shift_ref, scale_ref, tab_ref, win_ref, q_ref, k_ref, vt_ref, km_ref):
    h = x_ref[0] * (1.0 + scale_ref[0]) + shift_ref[0]
    proj = jnp.dot(h.astype(BF16), win_ref[...], preferred_element_type=F32)
    cw = C_HEADS * HEAD_DIM
    cos, sa, sb = _rope_coeffs(tab_ref[0])
    q_ref[0] = (_rope(proj[:, 0:cw], cos, sa, sb) * ATT_SCALE).astype(BF16)
    k = _rope(proj[:, cw:2 * cw], cos, sa, sb)
    k_ref[0] = k.astype(BF16)
    vt_ref[0] = proj[:, 2 * cw:3 * cw].T.astype(BF16)
    km_ref[0, 0] = jnp.mean(k, axis=0, keepdims=True)


def _proj_c(x, shift, scale, rope_tab, w_in):
    bsz, s, d = x.shape
    tm = C_BLOCK
    cw = C_HEADS * HEAD_DIM
    row = lambda w: pl.BlockSpec((1, tm, w), lambda b, i: (b, i, 0))
    vec = pl.BlockSpec((1, 1, d), lambda b, i: (b, 0, 0))
    full = lambda a: pl.BlockSpec(a.shape, lambda b, i: (0,) * a.ndim)
    qkv = jax.ShapeDtypeStruct((bsz, s, cw), BF16)
    return pl.pallas_call(
        _proj_c_kernel,
        grid=(bsz, s // tm),
        in_specs=[row(d), vec, vec, pl.BlockSpec((1, 128, tm), lambda b, i: (b, 0, i)), full(w_in)],
        out_specs=[row(cw), row(cw), pl.BlockSpec((1, cw, tm), lambda b, i: (b, 0, i)),
                   pl.BlockSpec((1, 1, 1, cw), lambda b, i: (b, i, 0, 0))],
        out_shape=[qkv, qkv, jax.ShapeDtypeStruct((bsz, cw, s), BF16),
                   jax.ShapeDtypeStruct((bsz, s // tm, 1, cw), F32)],
        compiler_params=_params(("parallel", "parallel")),
        name="proj_c",
    )(x, shift, scale, rope_tab, w_in)


MOBA_HEADS_PER_STEP = 16


def _moba_kernel(q_ref, k_ref, vt_ref, km_ref, o_ref, bias_scr, acc_scr):
    i = pl.program_id(2)
    tq = q_ref.shape[1]
    nb = k_ref.shape[1] // C_BLOCK
    nh = MOBA_HEADS_PER_STEP
    hsl = [slice(hh * HEAD_DIM, (hh + 1) * HEAD_DIM) for hh in range(nh)]
    qs = [q_ref[0, :, hsl[hh]] for hh in range(nh)]

    n_idx = lax.broadcasted_iota(jnp.int32, (nb, tq), 0)
    for hh in range(nh):
        gate = lax.dot_general(km_ref[0, hh].astype(BF16), qs[hh], _NT, preferred_element_type=F32)
        cnt = jnp.zeros((nb, tq), F32)
        for m_idx in range(nb - 1):
            other = gate[m_idx:m_idx + 1, :]
            tie = jnp.where(n_idx > m_idx, 1.0, 0.0)
            beats = jnp.where(other > gate, 1.0, jnp.where(other == gate, tie, 0.0))
            cnt = cnt + jnp.where(m_idx < i, beats, 0.0)
        bias_scr[hh] = jnp.where(n_idx < i, jnp.where(cnt < float(C_TOPK), 0.0, MASK_NEG), MASK_NEG)

    def block_update(blk, mask, bias, ms, ls):
        k0 = pl.multiple_of(blk * C_BLOCK, C_BLOCK)
        ss = []
        for hh in range(nh):
            kj = k_ref[0, pl.ds(k0, C_BLOCK), hsl[hh]]
            ss.append(mask(lax.dot_general(kj, qs[hh], _NT, preferred_element_type=F32)))
        ms_new, ls_new, alphas, ps = [], [], [], []
        for hh in range(nh):
            b = bias(hh)
            m_new = jnp.maximum(ms[hh], jnp.max(ss[hh], axis=0, keepdims=True) + b)
            alpha = jnp.exp2(ms[hh] - m_new)
            p = jnp.exp2(ss[hh] - (m_new - b))
            ls_new.append(alpha * ls[hh] + jnp.sum(p, axis=0, keepdims=True))
            ms_new.append(m_new)
            alphas.append(alpha)
            ps.append(p.astype(BF16))
        for hh in range(nh):
            vj = vt_ref[0, hsl[hh], pl.ds(k0, C_BLOCK)]
            acc_scr[hh] = alphas[hh] * acc_scr[hh] + jnp.dot(vj, ps[hh], preferred_element_type=F32)
        return tuple(ms_new), tuple(ls_new)

    krow = lax.broadcasted_iota(jnp.int32, (C_BLOCK, tq), 0)
    qcol = lax.broadcasted_iota(jnp.int32, (C_BLOCK, tq), 1)
    causal = krow <= qcol
    acc_scr[...] = jnp.zeros_like(acc_scr)
    ms0 = tuple(jnp.full((1, tq), MASK_NEG, F32) for _ in range(nh))
    ls0 = tuple(jnp.zeros((1, tq), F32) for _ in range(nh))
    carry = block_update(i, lambda s: jnp.where(causal, s, MASK_NEG), lambda hh: 0.0, ms0, ls0)

    def past_block(j, carry):
        return block_update(j, lambda s: s, lambda hh: bias_scr[hh, pl.ds(j, 1), :], carry[0], carry[1])

    _, ls = lax.fori_loop(0, i, past_block, carry)

    outs = []
    for hh in range(0, nh, 2):
        pair = jnp.concatenate([acc_scr[hh] / ls[hh], acc_scr[hh + 1] / ls[hh + 1]], axis=0)
        outs.append(pair.T)
    o_ref[0] = jnp.concatenate(outs, axis=-1).astype(BF16)


def _moba_attention(q, k, vt, kmean):
    bsz, s, cw = q.shape
    tq = C_BLOCK
    nh = MOBA_HEADS_PER_STEP
    w = nh * HEAD_DIM
    nb = s // C_BLOCK
    qspec = pl.BlockSpec((1, tq, w), lambda b, hg, i: (b, i, hg))
    return pl.pallas_call(
        _moba_kernel,
        grid=(bsz, cw // w, s // tq),
        in_specs=[qspec,
                  pl.BlockSpec((1, s, w), lambda b, hg, i: (b, 0, hg)),
                  pl.BlockSpec((1, w, s), lambda b, hg, i: (b, hg, 0)),
                  pl.BlockSpec((1, nh, nb, HEAD_DIM), lambda b, hg, i: (b, hg, 0, 0))],
        out_specs=qspec,
        out_shape=jax.ShapeDtypeStruct(q.shape, BF16),
        scratch_shapes=[pltpu.VMEM((nh, nb, tq), F32), pltpu.VMEM((nh, HEAD_DIM, tq), F32)],
        compiler_params=_params(("parallel", "parallel", "arbitrary")),
        name="moba_attention",
    )(q, k, vt, kmean)


def _rope_table(positions):
    inv = ROPE_THETA ** (-jnp.arange(0, HEAD_DIM, 2, dtype=F32) / HEAD_DIM)
    ang = positions.astype(F32)[:, None, :] * inv[None, :, None]
    cos, sin = lax.optimization_barrier((jnp.cos(ang), jnp.sin(ang)))
    return jnp.concatenate([cos, sin, cos, sin], axis=1)


def kernel(x, c, positions, ab_w_in, ab_q_norm, ab_w_uq, ab_w_uiq, ab_sinks, ab_w_out, c_w_in, c_w_out,
           ada_w, ada_b, ln_g, ln_b, mlp_w1, mlp_w2):
    bsz, s, d = x.shape
    rope_tab = _rope_table(positions)
    mod = _ada_modulation(c, ada_w, ada_b)

    def mods(idx):
        m = mod[idx]
        return m[:, None, 0:d], m[:, None, d:2 * d], m[:, None, 2 * d:3 * d] + 1.0

    for layer in range(DEPTH):
        shift, scale, gate = mods(2 * layer)
        shift2, scale2, gate2 = mods(2 * layer + 1)
        post = lambda parts, w_out: _post_attn(
            parts, w_out.astype(BF16), x, gate, ln_g[layer, 0][None], ln_b[layer, 0][None],
            shift2, scale2, gate2, mlp_w1[layer].astype(BF16), mlp_w2[layer].astype(BF16),
            ln_g[layer, 1][None], ln_b[layer, 1][None])
        if layer % 2 == 0:
            e = layer // 2
            w_in = jnp.pad(ab_w_in[e], ((0, 0), (0, AB_IN_PAD - AB_IN_WIDTH))).astype(BF16)
            aq, ak, avt, bq, biq, misc, bvt, iwt = _proj_ab(
                x, shift, scale, rope_tab, w_in, ab_q_norm[e][None],
                ab_w_uq[e].astype(BF16), ab_w_uiq[e].astype(BF16))
            ya = _swa_attention(ab_sinks[e], aq, ak, avt)
            yb = _dsa_attention(bq, biq, iwt, misc, bvt)
            x = post([ya, yb], ab_w_out[e])
        else:
            o = layer // 2
            q, k, vt, kmean = _proj_c(x, shift, scale, rope_tab, c_w_in[o].astype(BF16))
            km = kmean.reshape(bsz, s // C_BLOCK, C_HEADS, HEAD_DIM).transpose(0, 2, 1, 3)
            y = _moba_attention(q, k, vt, km)
            x = post([y], c_w_out[o])
    return x
```

```python
import functools

import jax
import jax.numpy as jnp
from jax import lax
from jax.experimental import pallas as pl
from jax.experimental.pallas import tpu as pltpu

HEAD_DIM = 64
ROPE_THETA = 10000.0
DEPTH = 2
A_Q_HEADS = 8
A_KV_HEADS = 2
A_BLOCK = 128
B_Q_HEADS = 8
B_IDX_HEADS = 8
B_TOPK = 256
C_HEADS = 16
C_BLOCK = 256
C_TOPK = 3
DN_ALPHA = (2 * DEPTH) ** 0.25
LN_EPS = 1e-5
RMS_EPS = 1e-6
AB_IN_WIDTH = 1224
AB_IN_PAD = 1280
LOG2E = 1.4426950408889634
ATT_SCALE = HEAD_DIM ** -0.5 * LOG2E
IDX_SCALE = B_IDX_HEADS ** -0.5 * HEAD_DIM ** -0.5

F32 = jnp.float32
BF16 = jnp.bfloat16
MASK_NEG = -1e30
INT_MIN = -(2 ** 31)
HALF = 2 ** 15
VMEM_LIMIT = 48 * 1024 * 1024

_NT = (((1,), (1,)), ((), ()))


def _params(sem):
    return pltpu.CompilerParams(dimension_semantics=sem, vmem_limit_bytes=VMEM_LIMIT)


def _rope(t, cos, sin_a, sin_b):
    outs = []
    for k in range(t.shape[-1] // 128):
        blk = t[:, 128 * k:128 * (k + 1)]
        outs.append(blk * cos + pltpu.roll(blk, 32, 1) * sin_a + pltpu.roll(blk, 96, 1) * sin_b)
    return outs[0] if len(outs) == 1 else jnp.concatenate(outs, axis=-1)


def _rope_coeffs(tab_t):
    tab = tab_t.T
    low = (lax.broadcasted_iota(jnp.int32, tab.shape, 1) & (HEAD_DIM - 1)) < HEAD_DIM // 2
    swapped = pltpu.roll(tab, HEAD_DIM // 2, 1)
    cos = jnp.where(low, tab, swapped)
    sin = jnp.where(low, swapped, tab)
    return cos, jnp.where(low, 0.0, sin), jnp.where(low, -sin, 0.0)


def _layer_norm(z, g, b):
    mu = jnp.mean(z, axis=-1, keepdims=True)
    zc = z - mu
    var = jnp.mean(zc * zc, axis=-1, keepdims=True)
    return zc * lax.rsqrt(var + LN_EPS) * g + b


def _ada_kernel(c_ref, w_ref, b_ref, o_ref):
    c = c_ref[...]
    sc = c / (1.0 + jnp.exp(-c))
    o_ref[0] = jnp.dot(sc, w_ref[0], preferred_element_type=F32) + b_ref[0]


def _ada_modulation(c, ada_w, ada_b):
    n = ada_w.shape[0] * ada_w.shape[1]
    bsz, d = c.shape
    w = ada_w.reshape(n, d, 3 * d)
    b = ada_b.reshape(n, 1, 3 * d)
    tn = 1024
    return pl.pallas_call(
        _ada_kernel,
        grid=(n, 3 * d // tn),
        in_specs=[
            pl.BlockSpec((bsz, d), lambda i, j: (0, 0)),
            pl.BlockSpec((1, d, tn), lambda i, j: (i, 0, j)),
            pl.BlockSpec((1, 1, tn), lambda i, j: (i, 0, j)),
        ],
        out_specs=pl.BlockSpec((1, bsz, tn), lambda i, j: (i, 0, j)),
        out_shape=jax.ShapeDtypeStruct((n, bsz, 3 * d), F32),
        compiler_params=_params(("arbitrary", "arbitrary")),
        name="ada_modulation",
    )(c, w, b)


def _proj_ab_kernel(x_ref, shift_ref, scale_ref, tab_ref, win_ref, qn_ref, wuq_ref, wuiq_ref,
                    aq_ref, ak_ref, avt_ref, bq_ref, biq_ref, misc_ref, bvt_ref, iwt_ref):
    h = x_ref[0] * (1.0 + scale_ref[0]) + shift_ref[0]
    proj = jnp.dot(h.astype(BF16), win_ref[...], preferred_element_type=F32)
    cos, sa, sb = _rope_coeffs(tab_ref[0])

    aq_ref[0] = (_rope(proj[:, 0:512], cos, sa, sb) * ATT_SCALE).astype(BF16)
    ak_ref[0] = _rope(proj[:, 512:640], cos, sa, sb).astype(BF16)
    avt_ref[0] = proj[:, 640:768].T.astype(BF16)

    cq = proj[:, 768:1024]
    ms = jnp.mean(cq * cq, axis=-1, keepdims=True)
    cqn = (cq * lax.rsqrt(ms + RMS_EPS) * qn_ref[...]).astype(BF16)
    bq = jnp.dot(cqn, wuq_ref[...], preferred_element_type=F32)
    biq = jnp.dot(cqn, wuiq_ref[...], preferred_element_type=F32)
    bq_ref[0] = (_rope(bq, cos, sa, sb) * ATT_SCALE).astype(BF16)
    biq_ref[0] = _rope(biq, cos, sa, sb).astype(BF16)

    lane = lax.broadcasted_iota(jnp.int32, cos.shape, 1)
    roped = lane < HEAD_DIM
    c0 = jnp.where(roped, cos, 1.0)
    a0 = jnp.where(roped, sa, 0.0)
    b0 = jnp.where(roped, sb, 0.0)
    m0 = _rope(proj[:, 1024:1152], c0, a0, b0)
    m1 = _rope(proj[:, 1152:1280], c0, a0, b0)
    misc_ref[0] = jnp.concatenate([m0, m1], axis=-1).astype(BF16)
    bvt_ref[0] = m0.T[HEAD_DIM:2 * HEAD_DIM].astype(BF16)
    iwt_ref[0] = m1.T[HEAD_DIM:HEAD_DIM + B_IDX_HEADS] * IDX_SCALE


def _proj_ab(x, shift, scale, rope_tab, w_in, q_norm, w_uq, w_uiq, tm=512):
    bsz, s, d = x.shape
    row = lambda w: pl.BlockSpec((1, tm, w), lambda b, i: (b, i, 0))
    vec = pl.BlockSpec((1, 1, d), lambda b, i: (b, 0, 0))
    full = lambda a: pl.BlockSpec(a.shape, lambda b, i: (0,) * a.ndim)
    col = lambda r: pl.BlockSpec((1, r, tm), lambda b, i: (b, 0, i))
    tok = lambda w: jax.ShapeDtypeStruct((bsz, s, w), BF16)
    kvw = A_KV_HEADS * HEAD_DIM
    out_shape = [tok(512), tok(kvw), jax.ShapeDtypeStruct((bsz, kvw, s), BF16), tok(512), tok(512), tok(256),
                 jax.ShapeDtypeStruct((bsz, HEAD_DIM, s), BF16), jax.ShapeDtypeStruct((bsz, B_IDX_HEADS, s), F32)]
    out_specs = [row(512), row(kvw), col(kvw), row(512), row(512), row(256), col(HEAD_DIM), col(B_IDX_HEADS)]
    return pl.pallas_call(
        _proj_ab_kernel,
        grid=(bsz, s // tm),
        in_specs=[row(d), vec, vec, col(128), full(w_in), full(q_norm), full(w_uq), full(w_uiq)],
        out_specs=out_specs,
        out_shape=out_shape,
        compiler_params=_params(("parallel", "parallel")),
        name="proj_ab",
    )(x, shift, scale, rope_tab, w_in, q_norm, w_uq, w_uiq)


SWA_TQ = 2 * A_BLOCK


def _swa_kernel(sink_ref, q_ref, kp_ref, kc_ref, vtp_ref, vtc_ref, o_ref):
    i = pl.program_id(1)
    group = A_Q_HEADS // A_KV_HEADS
    kband = jnp.concatenate([kp_ref[0], kc_ref[0]], axis=0)
    vtband = jnp.concatenate([vtp_ref[0], vtc_ref[0]], axis=1)
    c = lax.broadcasted_iota(jnp.int32, (2 * A_BLOCK, A_BLOCK), 0)
    qi = lax.broadcasted_iota(jnp.int32, (2 * A_BLOCK, A_BLOCK), 1)
    in_window = (c > qi) & (c <= qi + A_BLOCK)
    masks = [jnp.where(in_window & ((c >= A_BLOCK) | (i > 0)), 0.0, MASK_NEG), jnp.where(in_window, 0.0, MASK_NEG)]

    ss = {}
    for t in range(2):
        for kh in range(A_KV_HEADS):
            kb = kband[t * A_BLOCK:(t + 2) * A_BLOCK, kh * HEAD_DIM:(kh + 1) * HEAD_DIM]
            qstack = jnp.concatenate(
                [q_ref[0, t * A_BLOCK:(t + 1) * A_BLOCK, hq * HEAD_DIM:(hq + 1) * HEAD_DIM]
                 for hq in range(kh * group, (kh + 1) * group)], axis=0)
            s4 = lax.dot_general(kb, qstack, _NT, preferred_element_type=F32)
            for g in range(group):
                ss[t, kh * group + g] = s4[:, g * A_BLOCK:(g + 1) * A_BLOCK] + masks[t]
    ps, dens = {}, {}
    for t in range(2):
        for hq in range(A_Q_HEADS):
            sink = sink_ref[hq] * LOG2E
            m = jnp.maximum(jnp.max(ss[t, hq], axis=0, keepdims=True), sink)
            p = jnp.exp2(ss[t, hq] - m)
            dens[t, hq] = jnp.sum(p, axis=0, keepdims=True) + jnp.exp2(sink - m)
            ps[t, hq] = p.astype(BF16)
    for t in range(2):
        outs = []
        for hq in range(A_Q_HEADS):
            kh = hq // group
            vt = vtband[kh * HEAD_DIM:(kh + 1) * HEAD_DIM, t * A_BLOCK:(t + 2) * A_BLOCK]
            outs.append(jnp.dot(vt, ps[t, hq], preferred_element_type=F32) / dens[t, hq])
        o_ref[0, t * A_BLOCK:(t + 1) * A_BLOCK, :] = jnp.concatenate(outs, axis=0).T.astype(BF16)


def _swa_attention(sinks, aq, ak, avt):
    bsz, s, qw = aq.shape
    kvw = A_KV_HEADS * HEAD_DIM
    prev_blk = lambda i: jnp.maximum(2 * i - 1, 0)
    qspec = pl.BlockSpec((1, SWA_TQ, qw), lambda b, i: (b, i, 0))
    return pl.pallas_call(
        _swa_kernel,
        grid=(bsz, s // SWA_TQ),
        in_specs=[pl.BlockSpec(memory_space=pltpu.SMEM), qspec,
                  pl.BlockSpec((1, A_BLOCK, kvw), lambda b, i: (b, prev_blk(i), 0)),
                  pl.BlockSpec((1, SWA_TQ, kvw), lambda b, i: (b, i, 0)),
                  pl.BlockSpec((1, kvw, A_BLOCK), lambda b, i: (b, 0, prev_blk(i))),
                  pl.BlockSpec((1, kvw, SWA_TQ), lambda b, i: (b, 0, i))],
        out_specs=qspec,
        out_shape=jax.ShapeDtypeStruct(aq.shape, BF16),
        compiler_params=_params(("parallel", "parallel")),
        name="swa_attention",
    )(sinks, aq, ak, ak, avt, avt)


DSA_TQ = 256
DSA_KC = 256


def _dsa_kernel(q_ref, iq_ref, iwt_ref, misc_ref, vt_ref, o_ref, key_scr, hi_scr, lo_scr, acc_scr):
    i = pl.program_id(1)
    tq, kc = DSA_TQ, DSA_KC
    nkc = i + 1
    krow = lax.broadcasted_iota(jnp.int32, (kc, tq), 0)
    qcol = lax.broadcasted_iota(jnp.int32, (kc, tq), 1)
    on_or_below_diag = krow <= qcol
    iwt = iwt_ref[0]
    hsl = [slice(h * HEAD_DIM, (h + 1) * HEAD_DIM) for h in range(B_Q_HEADS)]
    iqs = [iq_ref[0, :, hsl[h]] for h in range(B_IDX_HEADS)]
    qs = [q_ref[0, :, hsl[h]] for h in range(B_Q_HEADS)]

    npairs = (nkc + 1) // 2

    def score_chunk(c):
        k0 = pl.multiple_of(c * kc, kc)
        ik = misc_ref[0, pl.ds(k0, kc), 128:192]
        sc = jnp.zeros((kc, tq), F32)
        for h in range(B_IDX_HEADS):
            raw = lax.dot_general(ik, iqs[h], _NT, preferred_element_type=F32)
            sc = sc + jnp.maximum(raw, 0.0) * iwt[h:h + 1, :]
        sc = jnp.where(sc == 0.0, 0.0, sc)
        bits = lax.bitcast_convert_type(sc, jnp.int32)
        key = jnp.where(bits >= 0, bits, bits ^ jnp.int32(0x7FFFFFFF))
        diag_or_beyond = jnp.where(c == i, jnp.where(on_or_below_diag, key, INT_MIN), INT_MIN)
        key = jnp.where(c < i, key, diag_or_beyond)
        key_scr[c] = key
        hi_scr[c] = (key >> 16).astype(jnp.int16)
        lo_scr[c] = ((key & 0xFFFF) - HALF).astype(jnp.int16)

    def score_pair(p, carry):
        score_chunk(2 * p)
        score_chunk(2 * p + 1)
        return carry

    lax.fori_loop(0, npairs, score_pair, 0)

    def count_ge(ref, c, cand):
        ones = jnp.where(ref[c] >= cand.astype(jnp.int16), jnp.int16(1), jnp.int16(0))
        part = ones[0:16]
        for r in range(1, kc // 16):
            part = part + ones[16 * r:16 * (r + 1)]
        return jnp.sum(part.astype(jnp.int32), axis=0, keepdims=True)

    def half_search(ref, wanted):
        def step(it, thr):
            cand = thr + lax.shift_left(jnp.int32(1), 15 - it)
            cnt = lax.fori_loop(
                0, npairs, lambda p, acc: acc + count_ge(ref, 2 * p, cand) + count_ge(ref, 2 * p + 1, cand),
                jnp.zeros((1, tq), jnp.int32))
            return jnp.where(cnt >= wanted, cand, thr)

        return lax.fori_loop(0, 16, step, jnp.full((1, tq), -HALF, jnp.int32))

    thr_hi = half_search(hi_scr, B_TOPK)

    def mask_low(c, above):
        hi = hi_scr[c].astype(jnp.int32)
        lo_scr[c] = jnp.where(hi == thr_hi, lo_scr[c].astype(jnp.int32), -HALF).astype(jnp.int16)
        return above + jnp.sum(jnp.where(hi > thr_hi, 1, 0), axis=0, keepdims=True)

    above = lax.fori_loop(0, nkc, mask_low, jnp.zeros((1, tq), jnp.int32))
    thr_lo = half_search(lo_scr, B_TOPK - above)
    thr = (thr_hi << 16) | (thr_lo + HALF)

    def stats(c, carry):
        ngt, neq = carry
        k = key_scr[c]
        return (ngt + jnp.sum(jnp.where(k > thr, 1.0, 0.0), axis=0, keepdims=True),
                neq + jnp.sum(jnp.where(k == thr, 1.0, 0.0), axis=0, keepdims=True))

    zero_row = jnp.zeros((1, tq), F32)
    ngt, neq = lax.fori_loop(0, nkc, stats, (zero_row, zero_row))
    need = float(B_TOPK) - ngt

    @pl.when(jnp.max(neq - need) > 0.0)
    def _():
        ra = lax.broadcasted_iota(jnp.int32, (kc, kc), 0)
        rb = lax.broadcasted_iota(jnp.int32, (kc, kc), 1)
        lower = jnp.where(rb < ra, 1.0, 0.0).astype(BF16)

        def drop_late_ties(c, before):
            k = key_scr[c]
            eq = jnp.where(k == thr, 1.0, 0.0)
            prefix = jnp.dot(lower, eq.astype(BF16), preferred_element_type=F32) + before
            key_scr[c] = jnp.where(k == thr, jnp.where(prefix >= need, INT_MIN, k), k)
            return before + jnp.sum(eq, axis=0, keepdims=True)

        lax.fori_loop(0, nkc, drop_late_ties, zero_row)

    thr_sel = jnp.maximum(thr, INT_MIN + 1)

    acc_scr[...] = jnp.zeros_like(acc_scr)

    def attend(p, carry):
        ms, ls = carry
        k0 = pl.multiple_of(p * 2 * kc, 2 * kc)
        kk = misc_ref[0, pl.ds(k0, 2 * kc), 0:64]
        vt = vt_ref[0, :, pl.ds(k0, 2 * kc)]
        sel = jnp.concatenate([key_scr[2 * p], key_scr[2 * p + 1]], axis=0) >= thr_sel
        sel_bias = jnp.where(sel, 0.0, MASK_NEG)
        ss = [lax.dot_general(kk, qs[h], _NT, preferred_element_type=F32) + sel_bias
              for h in range(B_Q_HEADS)]
        ms_new, ls_new, alphas, ps = [], [], [], []
        for h in range(B_Q_HEADS):
            m_new = jnp.maximum(ms[h], jnp.max(ss[h], axis=0, keepdims=True))
            alpha = jnp.exp2(ms[h] - m_new)
            p = jnp.exp2(ss[h] - m_new)
            ls_new.append(alpha * ls[h] + jnp.sum(p, axis=0, keepdims=True))
            ms_new.append(m_new)
            alphas.append(alpha)
            ps.append(p.astype(BF16))
        for h in range(B_Q_HEADS):
            acc_scr[h] = alphas[h] * acc_scr[h] + jnp.dot(vt, ps[h], preferred_element_type=F32)
        return tuple(ms_new), tuple(ls_new)

    init = (tuple(jnp.full((1, tq), MASK_NEG, F32) for _ in range(B_Q_HEADS)),
            tuple(jnp.zeros((1, tq), F32) for _ in range(B_Q_HEADS)))
    _, ls = lax.fori_loop(0, npairs, attend, init)

    outs = []
    for h in range(0, B_Q_HEADS, 2):
        pair = jnp.concatenate([acc_scr[h] / ls[h], acc_scr[h + 1] / ls[h + 1]], axis=0)
        outs.append(pair.T)
    o_ref[0] = jnp.concatenate(outs, axis=-1).astype(BF16)


def _dsa_attention(bq, biq, iwt, misc, vt):
    bsz, s, _ = bq.shape
    tq = DSA_TQ
    qspec = pl.BlockSpec((1, tq, B_Q_HEADS * HEAD_DIM), lambda b, i: (b, i, 0))
    return pl.pallas_call(
        _dsa_kernel,
        grid=(bsz, s // tq),
        in_specs=[qspec, qspec,
                  pl.BlockSpec((1, B_IDX_HEADS, tq), lambda b, i: (b, 0, i)),
                  pl.BlockSpec((1, s, misc.shape[2]), lambda b, i: (b, 0, 0)),
                  pl.BlockSpec((1, HEAD_DIM, s), lambda b, i: (b, 0, 0))],
        out_specs=qspec,
        out_shape=jax.ShapeDtypeStruct(bq.shape, BF16),
        scratch_shapes=[pltpu.VMEM((s // DSA_KC, DSA_KC, tq), jnp.int32),
                        pltpu.VMEM((s // DSA_KC, DSA_KC, tq), jnp.int16),
                        pltpu.VMEM((s // DSA_KC, DSA_KC, tq), jnp.int16),
                        pltpu.VMEM((B_Q_HEADS, HEAD_DIM, tq), F32)],
        compiler_params=_params(("parallel", "arbitrary")),
        name="dsa_attention",
    )(bq, biq, iwt, misc, vt)


POST_TM = 512
POST_ROWS = 256
POST_TF = 1024
POST_VMEM_LIMIT = 56 * 1024 * 1024


def _post_attn_kernel(*refs, widths):
    n = len(widths)
    parts = refs[:n]
    (wout_ref, x_ref, gate1_ref, g1_ref, b1_ref, shift2_ref, scale2_ref, gate2_ref,
     w1_ref, w2_ref, g2_ref, b2_ref, o_ref) = refs[n:]
    groups = [slice(r * POST_ROWS, (r + 1) * POST_ROWS) for r in range(x_ref.shape[1] // POST_ROWS)]

    def out_proj(rows):
        y = None
        off = 0
        for p_ref, wd in zip(parts, widths):
            t = jnp.dot(p_ref[0, rows, :], wout_ref[off:off + wd, :], preferred_element_type=F32)
            y = t if y is None else y + t
            off += wd
        return y

    def norm1(rows, y):
        x1 = _layer_norm(DN_ALPHA * x_ref[0, rows, :] + gate1_ref[0] * y, g1_ref[...], b1_ref[...])
        return x1, (x1 * (1.0 + scale2_ref[0]) + shift2_ref[0]).astype(BF16)

    def mlp(h):
        acc = None
        for f in range(w1_ref.shape[1] // POST_TF):
            cols = slice(f * POST_TF, (f + 1) * POST_TF)
            u = jnp.maximum(jnp.dot(h, w1_ref[:, cols], preferred_element_type=F32), 0.0)
            t = jnp.dot((u * u).astype(BF16), w2_ref[cols, :], preferred_element_type=F32)
            acc = t if acc is None else acc + t
        return acc

    ys = [out_proj(rows) for rows in groups]
    x1s, accs = [], []
    for rows, y in zip(groups, ys):
        x1, h = norm1(rows, y)
        x1s.append(x1)
        accs.append(mlp(h))
    for rows, x1, acc in zip(groups, x1s, accs):
        o_ref[0, rows, :] = _layer_norm(DN_ALPHA * x1 + gate2_ref[0] * acc, g2_ref[...], b2_ref[...])


def _post_attn(parts, w_out, x, gate1, g1, b1, shift2, scale2, gate2, w1, w2, g2, b2):
    bsz, s, d = x.shape
    tm = POST_TM
    widths = tuple(p.shape[-1] for p in parts)
    row = lambda w: pl.BlockSpec((1, tm, w), lambda bi, i: (bi, i, 0))
    vec = pl.BlockSpec((1, 1, d), lambda bi, i: (bi, 0, 0))
    full = lambda a: pl.BlockSpec(a.shape, lambda bi, i: (0,) * a.ndim, pipeline_mode=pl.Buffered(1))
    return pl.pallas_call(
        functools.partial(_post_attn_kernel, widths=widths),
        grid=(bsz, s // tm),
        in_specs=([row(w) for w in widths]
                  + [full(w_out), row(d), vec, full(g1), full(b1), vec, vec, vec,
                     full(w1), full(w2), full(g2), full(b2)]),
        out_specs=row(d),
        out_shape=jax.ShapeDtypeStruct(x.shape, F32),
        compiler_params=pltpu.CompilerParams(dimension_semantics=("parallel", "parallel"),
                                             vmem_limit_bytes=POST_VMEM_LIMIT),
        name="post_attn",
    )(*parts, w_out, x, gate1, g1, b1, shift2, scale2, gate2, w1, w2, g2, b2)


def _proj_c_kernel(x_ref, shift_ref, scale_ref, tab_ref, win_ref, q_ref, k_ref, vt_ref, km_ref):
    h = x_ref[0] * (1.0 + scale_ref[0]) + shift_ref[0]
    proj = jnp.dot(h.astype(BF16), win_ref[...], preferred_element_type=F32)
    cw = C_HEADS * HEAD_DIM
    cos, sa, sb = _rope_coeffs(tab_ref[0])
    q_ref[0] = (_rope(proj[:, 0:cw], cos, sa, sb) * ATT_SCALE).astype(BF16)
    k = _rope(proj[:, cw:2 * cw], cos, sa, sb)
    k_ref[0] = k.astype(BF16)
    vt_ref[0] = proj[:, 2 * cw:3 * cw].T.astype(BF16)
    km_ref[0, 0] = jnp.mean(k, axis=0, keepdims=True)


def _proj_c(x, shift, scale, rope_tab, w_in):
    bsz, s, d = x.shape
    tm = C_BLOCK
    cw = C_HEADS * HEAD_DIM
    row = lambda w: pl.BlockSpec((1, tm, w), lambda b, i: (b, i, 0))
    vec = pl.BlockSpec((1, 1, d), lambda b, i: (b, 0, 0))
    full = lambda a: pl.BlockSpec(a.shape, lambda b, i: (0,) * a.ndim)
    qkv = jax.ShapeDtypeStruct((bsz, s, cw), BF16)
    return pl.pallas_call(
        _proj_c_kernel,
        grid=(bsz, s // tm),
        in_specs=[row(d), vec, vec, pl.BlockSpec((1, 128, tm), lambda b, i: (b, 0, i)), full(w_in)],
        out_specs=[row(cw), row(cw), pl.BlockSpec((1, cw, tm), lambda b, i: (b, 0, i)),
                   pl.BlockSpec((1, 1, 1, cw), lambda b, i: (b, i, 0, 0))],
        out_shape=[qkv, qkv, jax.ShapeDtypeStruct((bsz, cw, s), BF16),
                   jax.ShapeDtypeStruct((bsz, s // tm, 1, cw), F32)],
        compiler_params=_params(("parallel", "parallel")),
        name="proj_c",
    )(x, shift, scale, rope_tab, w_in)


MOBA_HEADS_PER_STEP = 16


def _moba_kernel(q_ref, k_ref, vt_ref, km_ref, o_ref, bias_scr, acc_scr):
    i = pl.program_id(2)
    tq = q_ref.shape[1]
    nb = k_ref.shape[1] // C_BLOCK
    nh = MOBA_HEADS_PER_STEP
    hsl = [slice(hh * HEAD_DIM, (hh + 1) * HEAD_DIM) for hh in range(nh)]
    qs = [q_ref[0, :, hsl[hh]] for hh in range(nh)]

    n_idx = lax.broadcasted_iota(jnp.int32, (nb, tq), 0)
    for hh in range(nh):
        gate = lax.dot_general(km_ref[0, hh].astype(BF16), qs[hh], _NT, preferred_element_type=F32)
        cnt = jnp.zeros((nb, tq), F32)
        for m_idx in range(nb - 1):
            other = gate[m_idx:m_idx + 1, :]
            tie = jnp.where(n_idx > m_idx, 1.0, 0.0)
            beats = jnp.where(other > gate, 1.0, jnp.where(other == gate, tie, 0.0))
            cnt = cnt + jnp.where(m_idx < i, beats, 0.0)
        bias_scr[hh] = jnp.where(n_idx < i, jnp.where(cnt < float(C_TOPK), 0.0, MASK_NEG), MASK_NEG)

    def block_update(blk, mask, bias, ms, ls):
        k0 = pl.multiple_of(blk * C_BLOCK, C_BLOCK)
        ss = []
        for hh in range(nh):
            kj = k_ref[0, pl.ds(k0, C_BLOCK), hsl[hh]]
            ss.append(mask(lax.dot_general(kj, qs[hh], _NT, preferred_element_type=F32)))
        ms_new, ls_new, alphas, ps = [], [], [], []
        for hh in range(nh):
            b = bias(hh)
            m_new = jnp.maximum(ms[hh], jnp.max(ss[hh], axis=0, keepdims=True) + b)
            alpha = jnp.exp2(ms[hh] - m_new)
            p = jnp.exp2(ss[hh] - (m_new - b))
            ls_new.append(alpha * ls[hh] + jnp.sum(p, axis=0, keepdims=True))
            ms_new.append(m_new)
            alphas.append(alpha)
            ps.append(p.astype(BF16))
        for hh in range(nh):
            vj = vt_ref[0, hsl[hh], pl.ds(k0, C_BLOCK)]
            acc_scr[hh] = alphas[hh] * acc_scr[hh] + jnp.dot(vj, ps[hh], preferred_element_type=F32)
        return tuple(ms_new), tuple(ls_new)

    krow = lax.broadcasted_iota(jnp.int32, (C_BLOCK, tq), 0)
    qcol = lax.broadcasted_iota(jnp.int32, (C_BLOCK, tq), 1)
    causal = krow <= qcol
    acc_scr[...] = jnp.zeros_like(acc_scr)
    ms0 = tuple(jnp.full((1, tq), MASK_NEG, F32) for _ in range(nh))
    ls0 = tuple(jnp.zeros((1, tq), F32) for _ in range(nh))
    carry = block_update(i, lambda s: jnp.where(causal, s, MASK_NEG), lambda hh: 0.0, ms0, ls0)

    def past_block(j, carry):
        return block_update(j, lambda s: s, lambda hh: bias_scr[hh, pl.ds(j, 1), :], carry[0], carry[1])

    _, ls = lax.fori_loop(0, i, past_block, carry)

    outs = []
    for hh in range(0, nh, 2):
        pair = jnp.concatenate([acc_scr[hh] / ls[hh], acc_scr[hh + 1] / ls[hh + 1]], axis=0)
        outs.append(pair.T)
    o_ref[0] = jnp.concatenate(outs, axis=-1).astype(BF16)


def _moba_attention(q, k, vt, kmean):
    bsz, s, cw = q.shape
    tq = C_BLOCK
    nh = MOBA_HEADS_PER_STEP
    w = nh * HEAD_DIM
    nb = s // C_BLOCK
    qspec = pl.BlockSpec((1, tq, w), lambda b, hg, i: (b, i, hg))
    return pl.pallas_call(
        _moba_kernel,
        grid=(bsz, cw // w, s // tq),
        in_specs=[qspec,
                  pl.BlockSpec((1, s, w), lambda b, hg, i: (b, 0, hg)),
                  pl.BlockSpec((1, w, s), lambda b, hg, i: (b, hg, 0)),
                  pl.BlockSpec((1, nh, nb, HEAD_DIM), lambda b, hg, i: (b, hg, 0, 0))],
        out_specs=qspec,
        out_shape=jax.ShapeDtypeStruct(q.shape, BF16),
        scratch_shapes=[pltpu.VMEM((nh, nb, tq), F32), pltpu.VMEM((nh, HEAD_DIM, tq), F32)],
        compiler_params=_params(("parallel", "parallel", "arbitrary")),
        name="moba_attention",
    )(q, k, vt, kmean)


def _rope_table(positions):
    inv = ROPE_THETA ** (-jnp.arange(0, HEAD_DIM, 2, dtype=F32) / HEAD_DIM)
    ang = positions.astype(F32)[:, None, :] * inv[None, :, None]
    cos, sin = lax.optimization_barrier((jnp.cos(ang), jnp.sin(ang)))
    return jnp.concatenate([cos, sin, cos, sin], axis=1)


def kernel(x, c, positions, ab_w_in, ab_q_norm, ab_w_uq, ab_w_uiq, ab_sinks, ab_w_out, c_w_in, c_w_out,
           ada_w, ada_b, ln_g, ln_b, mlp_w1, mlp_w2):
    bsz, s, d = x.shape
    rope_tab = _rope_table(positions)
    mod = _ada_modulation(c, ada_w, ada_b)

    def mods(idx):
        m = mod[idx]
        return m[:, None, 0:d], m[:, None, d:2 * d], m[:, None, 2 * d:3 * d] + 1.0

    for layer in range(DEPTH):
        shift, scale, gate = mods(2 * layer)
        shift2, scale2, gate2 = mods(2 * layer + 1)
        post = lambda parts, w_out: _post_attn(
            parts, w_out.astype(BF16), x, gate, ln_g[layer, 0][None], ln_b[layer, 0][None],
            shift2, scale2, gate2, mlp_w1[layer].astype(BF16), mlp_w2[layer].astype(BF16),
            ln_g[layer, 1][None], ln_b[layer, 1][None])
        if layer % 2 == 0:
            e = layer // 2
            w_in = jnp.pad(ab_w_in[e], ((0, 0), (0, AB_IN_PAD - AB_IN_WIDTH))).astype(BF16)
            aq, ak, avt, bq, biq, misc, bvt, iwt = _proj_ab(
                x, shift, scale, rope_tab, w_in, ab_q_norm[e][None],
                ab_w_uq[e].astype(BF16), ab_w_uiq[e].astype(BF16))
            ya = _swa_attention(ab_sinks[e], aq, ak, avt)
            yb = _dsa_attention(bq, biq, iwt, misc, bvt)
            x = post([ya, yb], ab_w_out[e])
        else:
            o = layer // 2
            q, k, vt, kmean = _proj_c(x, shift, scale, rope_tab, c_w_in[o].astype(BF16))
            km = kmean.reshape(bsz, s // C_BLOCK, C_HEADS, HEAD_DIM).transpose(0, 2, 1, 3)
            y = _moba_attention(q, k, vt, km)
            x = post([y], c_w_out[o])
    return x
```

```python
import functools

import jax
import jax.numpy as jnp
from jax import lax
from jax.experimental import pallas as pl
from jax.experimental.pallas import tpu as pltpu

HEAD_DIM = 64
ROPE_THETA = 10000.0
DEPTH = 2
A_Q_HEADS = 8
A_KV_HEADS = 2
A_BLOCK = 128
B_Q_HEADS = 8
B_IDX_HEADS = 8
B_TOPK = 256
C_HEADS = 16
C_BLOCK = 256
C_TOPK = 3
DN_ALPHA = (2 * DEPTH) ** 0.25
LN_EPS = 1e-5
RMS_EPS = 1e-6
AB_IN_WIDTH = 1224
AB_IN_PAD = 1280
LOG2E = 1.4426950408889634
ATT_SCALE = HEAD_DIM ** -0.5 * LOG2E
IDX_SCALE = B_IDX_HEADS ** -0.5 * HEAD_DIM ** -0.5

F32 = jnp.float32
BF16 = jnp.bfloat16
MASK_NEG = -1e30
INT_MIN = -(2 ** 31)
HALF = 2 ** 15
VMEM_LIMIT = 48 * 1024 * 1024

_NT = (((1,), (1,)), ((), ()))


def _params(sem):
    return pltpu.CompilerParams(dimension_semantics=sem, vmem_limit_bytes=VMEM_LIMIT)


def _rope(t, cos, sin_a, sin_b):
    outs = []
    for k in range(t.shape[-1] // 128):
        blk = t[:, 128 * k:128 * (k + 1)]
        outs.append(blk * cos + pltpu.roll(blk, 32, 1) * sin_a + pltpu.roll(blk, 96, 1) * sin_b)
    return outs[0] if len(outs) == 1 else jnp.concatenate(outs, axis=-1)


def _rope_coeffs(tab_t):
    tab = tab_t.T
    low = (lax.broadcasted_iota(jnp.int32, tab.shape, 1) & (HEAD_DIM - 1)) < HEAD_DIM // 2
    swapped = pltpu.roll(tab, HEAD_DIM // 2, 1)
    cos = jnp.where(low, tab, swapped)
    sin = jnp.where(low, swapped, tab)
    return cos, jnp.where(low, 0.0, sin), jnp.where(low, -sin, 0.0)


def _layer_norm(z, g, b):
    mu = jnp.mean(z, axis=-1, keepdims=True)
    zc = z - mu
    var = jnp.mean(zc * zc, axis=-1, keepdims=True)
    return zc * lax.rsqrt(var + LN_EPS) * g + b


def _ada_kernel(c_ref, w_ref, b_ref, o_ref):
    c = c_ref[...]
    sc = c / (1.0 + jnp.exp(-c))
    o_ref[0] = jnp.dot(sc, w_ref[0], preferred_element_type=F32) + b_ref[0]


def _ada_modulation(c, ada_w, ada_b):
    n = ada_w.shape[0] * ada_w.shape[1]
    bsz, d = c.shape
    w = ada_w.reshape(n, d, 3 * d)
    b = ada_b.reshape(n, 1, 3 * d)
    tn = 1024
    return pl.pallas_call(
        _ada_kernel,
        grid=(n, 3 * d // tn),
        in_specs=[
            pl.BlockSpec((bsz, d), lambda i, j: (0, 0)),
            pl.BlockSpec((1, d, tn), lambda i, j: (i, 0, j)),
            pl.BlockSpec((1, 1, tn), lambda i, j: (i, 0, j)),
        ],
        out_specs=pl.BlockSpec((1, bsz, tn), lambda i, j: (i, 0, j)),
        out_shape=jax.ShapeDtypeStruct((n, bsz, 3 * d), F32),
        compiler_params=_params(("arbitrary", "arbitrary")),
        name="ada_modulation",
    )(c, w, b)


def _proj_ab_kernel(x_ref, shift_ref, scale_ref, tab_ref, win_ref, qn_ref, wuq_ref, wuiq_ref,
                    aq_ref, ak_ref, avt_ref, bq_ref, biq_ref, misc_ref, bvt_ref, iwt_ref):
    h = x_ref[0] * (1.0 + scale_ref[0]) + shift_ref[0]
    proj = jnp.dot(h.astype(BF16), win_ref[...], preferred_element_type=F32)
    cos, sa, sb = _rope_coeffs(tab_ref[0])

    aq_ref[0] = (_rope(proj[:, 0:512], cos, sa, sb) * ATT_SCALE).astype(BF16)
    ak_ref[0] = _rope(proj[:, 512:640], cos, sa, sb).astype(BF16)
    avt_ref[0] = proj[:, 640:768].T.astype(BF16)

    cq = proj[:, 768:1024]
    ms = jnp.mean(cq * cq, axis=-1, keepdims=True)
    cqn = (cq * lax.rsqrt(ms + RMS_EPS) * qn_ref[...]).astype(BF16)
    bq = jnp.dot(cqn, wuq_ref[...], preferred_element_type=F32)
    biq = jnp.dot(cqn, wuiq_ref[...], preferred_element_type=F32)
    bq_ref[0] = (_rope(bq, cos, sa, sb) * ATT_SCALE).astype(BF16)
    biq_ref[0] = _rope(biq, cos, sa, sb).astype(BF16)

    lane = lax.broadcasted_iota(jnp.int32, cos.shape, 1)
    roped = lane < HEAD_DIM
    c0 = jnp.where(roped, cos, 1.0)
    a0 = jnp.where(roped, sa, 0.0)
    b0 = jnp.where(roped, sb, 0.0)
    m0 = _rope(proj[:, 1024:1152], c0, a0, b0)
    m1 = _rope(proj[:, 1152:1280], c0, a0, b0)
    misc_ref[0] = jnp.concatenate([m0, m1], axis=-1).astype(BF16)
    bvt_ref[0] = m0.T[HEAD_DIM:2 * HEAD_DIM].astype(BF16)
    iwt_ref[0] = m1.T[HEAD_DIM:HEAD_DIM + B_IDX_HEADS] * IDX_SCALE


def _proj_ab(x, shift, scale, rope_tab, w_in, q_norm, w_uq, w_uiq, tm=512):
    bsz, s, d = x.shape
    row = lambda w: pl.BlockSpec((1, tm, w), lambda b, i: (b, i, 0))
    vec = pl.BlockSpec((1, 1, d), lambda b, i: (b, 0, 0))
    full = lambda a: pl.BlockSpec(a.shape, lambda b, i: (0,) * a.ndim)
    col = lambda r: pl.BlockSpec((1, r, tm), lambda b, i: (b, 0, i))
    tok = lambda w: jax.ShapeDtypeStruct((bsz, s, w), BF16)
    kvw = A_KV_HEADS * HEAD_DIM
    out_shape = [tok(512), tok(kvw), jax.ShapeDtypeStruct((bsz, kvw, s), BF16), tok(512), tok(512), tok(256),
                 jax.ShapeDtypeStruct((bsz, HEAD_DIM, s), BF16), jax.ShapeDtypeStruct((bsz, B_IDX_HEADS, s), F32)]
    out_specs = [row(512), row(kvw), col(kvw), row(512), row(512), row(256), col(HEAD_DIM), col(B_IDX_HEADS)]
    return pl.pallas_call(
        _proj_ab_kernel,
        grid=(bsz, s // tm),
        in_specs=[row(d), vec, vec, col(128), full(w_in), full(q_norm), full(w_uq), full(w_uiq)],
        out_specs=out_specs,
        out_shape=out_shape,
        compiler_params=_params(("parallel", "parallel")),
        name="proj_ab",
    )(x, shift, scale, rope_tab, w_in, q_norm, w_uq, w_uiq)


SWA_BLOCKS = 4
SWA_TQ = SWA_BLOCKS * A_BLOCK


def _swa_kernel(sink_ref, q_ref, kp_ref, kc_ref, vtp_ref, vtc_ref, o_ref):
    i = pl.program_id(1)
    group = A_Q_HEADS // A_KV_HEADS
    kband = jnp.concatenate([kp_ref[0], kc_ref[0]], axis=0)
    vtband = jnp.concatenate([vtp_ref[0], vtc_ref[0]], axis=1)
    c = lax.broadcasted_iota(jnp.int32, (2 * A_BLOCK, A_BLOCK), 0)
    qi = lax.broadcasted_iota(jnp.int32, (2 * A_BLOCK, A_BLOCK), 1)
    in_window = (c > qi) & (c <= qi + A_BLOCK)
    later = jnp.where(in_window, 0.0, MASK_NEG)
    masks = [jnp.where(in_window & ((c >= A_BLOCK) | (i > 0)), 0.0, MASK_NEG)] + [later] * (SWA_BLOCKS - 1)

    ss = {}
    for t in range(SWA_BLOCKS):
        for kh in range(A_KV_HEADS):
            kb = kband[t * A_BLOCK:(t + 2) * A_BLOCK, kh * HEAD_DIM:(kh + 1) * HEAD_DIM]
            qstack = jnp.concatenate(
                [q_ref[0, t * A_BLOCK:(t + 1) * A_BLOCK, hq * HEAD_DIM:(hq + 1) * HEAD_DIM]
                 for hq in range(kh * group, (kh + 1) * group)], axis=0)
            s4 = lax.dot_general(kb, qstack, _NT, preferred_element_type=F32)
            for g in range(group):
                ss[t, kh * group + g] = s4[:, g * A_BLOCK:(g + 1) * A_BLOCK] + masks[t]
    ps, dens = {}, {}
    for t in range(SWA_BLOCKS):
        for hq in range(A_Q_HEADS):
            sink = sink_ref[hq] * LOG2E
            m = jnp.maximum(jnp.max(ss[t, hq], axis=0, keepdims=True), sink)
            p = jnp.exp2(ss[t, hq] - m)
            dens[t, hq] = jnp.sum(p, axis=0, keepdims=True) + jnp.exp2(sink - m)
            ps[t, hq] = p.astype(BF16)
    for t in range(SWA_BLOCKS):
        outs = []
        for hq in range(A_Q_HEADS):
            kh = hq // group
            vt = vtband[kh * HEAD_DIM:(kh + 1) * HEAD_DIM, t * A_BLOCK:(t + 2) * A_BLOCK]
            outs.append(jnp.dot(vt, ps[t, hq], preferred_element_type=F32) / dens[t, hq])
        o_ref[0, t * A_BLOCK:(t + 1) * A_BLOCK, :] = jnp.concatenate(outs, axis=0).T.astype(BF16)


def _swa_attention(sinks, aq, ak, avt):
    bsz, s, qw = aq.shape
    kvw = A_KV_HEADS * HEAD_DIM
    prev_blk = lambda i: jnp.maximum(SWA_BLOCKS * i - 1, 0)
    qspec = pl.BlockSpec((1, SWA_TQ, qw), lambda b, i: (b, i, 0))
    return pl.pallas_call(
        _swa_kernel,
        grid=(bsz, s // SWA_TQ),
        in_specs=[pl.BlockSpec(memory_space=pltpu.SMEM), qspec,
                  pl.BlockSpec((1, A_BLOCK, kvw), lambda b, i: (b, prev_blk(i), 0)),
                  pl.BlockSpec((1, SWA_TQ, kvw), lambda b, i: (b, i, 0)),
                  pl.BlockSpec((1, kvw, A_BLOCK), lambda b, i: (b, 0, prev_blk(i))),
                  pl.BlockSpec((1, kvw, SWA_TQ), lambda b, i: (b, 0, i))],
        out_specs=qspec,
        out_shape=jax.ShapeDtypeStruct(aq.shape, BF16),
        compiler_params=_params(("parallel", "parallel")),
        name="swa_attention",
    )(sinks, aq, ak, ak, avt, avt)


DSA_TQ = 256
DSA_KC = 256


def _dsa_kernel(q_ref, iq_ref, iwt_ref, misc_ref, vt_ref, o_ref, key_scr, hi_scr, lo_scr, acc_scr):
    i = pl.program_id(1)
    tq, kc = DSA_TQ, DSA_KC
    nkc = i + 1
    krow = lax.broadcasted_iota(jnp.int32, (kc, tq), 0)
    qcol = lax.broadcasted_iota(jnp.int32, (kc, tq), 1)
    on_or_below_diag = krow <= qcol
    iwt = iwt_ref[0]
    hsl = [slice(h * HEAD_DIM, (h + 1) * HEAD_DIM) for h in range(B_Q_HEADS)]
    iqs = [iq_ref[0, :, hsl[h]] for h in range(B_IDX_HEADS)]
    qs = [q_ref[0, :, hsl[h]] for h in range(B_Q_HEADS)]

    npairs = (nkc + 1) // 2

    def score_chunk(c):
        k0 = pl.multiple_of(c * kc, kc)
        ik = misc_ref[0, pl.ds(k0, kc), 128:192]
        sc = jnp.zeros((kc, tq), F32)
        for h in range(B_IDX_HEADS):
            raw = lax.dot_general(ik, iqs[h], _NT, preferred_element_type=F32)
            sc = sc + jnp.maximum(raw, 0.0) * iwt[h:h + 1, :]
        sc = jnp.where(sc == 0.0, 0.0, sc)
        bits = lax.bitcast_convert_type(sc, jnp.int32)
        key = jnp.where(bits >= 0, bits, bits ^ jnp.int32(0x7FFFFFFF))
        diag_or_beyond = jnp.where(c == i, jnp.where(on_or_below_diag, key, INT_MIN), INT_MIN)
        key = jnp.where(c < i, key, diag_or_beyond)
        key_scr[c] = key
        hi_scr[c] = (key >> 16).astype(jnp.int16)
        lo_scr[c] = ((key & 0xFFFF) - HALF).astype(jnp.int16)

    def score_pair(p, carry):
        score_chunk(2 * p)
        score_chunk(2 * p + 1)
        return carry

    lax.fori_loop(0, npairs, score_pair, 0)

    def count_ge(ref, c, cand):
        ones = jnp.where(ref[c] >= cand.astype(jnp.int16), jnp.int16(1), jnp.int16(0))
        part = ones[0:16]
        for r in range(1, kc // 16):
            part = part + ones[16 * r:16 * (r + 1)]
        return jnp.sum(part.astype(jnp.int32), axis=0, keepdims=True)

    def half_search(ref, wanted):
        def step(it, thr):
            cand = thr + lax.shift_left(jnp.int32(1), 15 - it)
            cnt = lax.fori_loop(
                0, npairs, lambda p, acc: acc + count_ge(ref, 2 * p, cand) + count_ge(ref, 2 * p + 1, cand),
                jnp.zeros((1, tq), jnp.int32))
            return jnp.where(cnt >= wanted, cand, thr)

        return lax.fori_loop(0, 16, step, jnp.full((1, tq), -HALF, jnp.int32))

    thr_hi = half_search(hi_scr, B_TOPK)

    def mask_low(c, above):
        hi = hi_scr[c].astype(jnp.int32)
        lo_scr[c] = jnp.where(hi == thr_hi, lo_scr[c].astype(jnp.int32), -HALF).astype(jnp.int16)
        return above + jnp.sum(jnp.where(hi > thr_hi, 1, 0), axis=0, keepdims=True)

    above = lax.fori_loop(0, nkc, mask_low, jnp.zeros((1, tq), jnp.int32))
    thr_lo = half_search(lo_scr, B_TOPK - above)
    thr = (thr_hi << 16) | (thr_lo + HALF)

    def stats(c, carry):
        ngt, neq = carry
        k = key_scr[c]
        return (ngt + jnp.sum(jnp.where(k > thr, 1.0, 0.0), axis=0, keepdims=True),
                neq + jnp.sum(jnp.where(k == thr, 1.0, 0.0), axis=0, keepdims=True))

    zero_row = jnp.zeros((1, tq), F32)
    ngt, neq = lax.fori_loop(0, nkc, stats, (zero_row, zero_row))
    need = float(B_TOPK) - ngt

    @pl.when(jnp.max(neq - need) > 0.0)
    def _():
        ra = lax.broadcasted_iota(jnp.int32, (kc, kc), 0)
        rb = lax.broadcasted_iota(jnp.int32, (kc, kc), 1)
        lower = jnp.where(rb < ra, 1.0, 0.0).astype(BF16)

        def drop_late_ties(c, before):
            k = key_scr[c]
            eq = jnp.where(k == thr, 1.0, 0.0)
            prefix = jnp.dot(lower, eq.astype(BF16), preferred_element_type=F32) + before
            key_scr[c] = jnp.where(k == thr, jnp.where(prefix >= need, INT_MIN, k), k)
            return before + jnp.sum(eq, axis=0, keepdims=True)

        lax.fori_loop(0, nkc, drop_late_ties, zero_row)

    thr_sel = jnp.maximum(thr, INT_MIN + 1)

    acc_scr[...] = jnp.zeros_like(acc_scr)

    def attend(p, carry):
        ms, ls = carry
        k0 = pl.multiple_of(p * 2 * kc, 2 * kc)
        kk = misc_ref[0, pl.ds(k0, 2 * kc), 0:64]
        vt = vt_ref[0, :, pl.ds(k0, 2 * kc)]
        sel = jnp.concatenate([key_scr[2 * p], key_scr[2 * p + 1]], axis=0) >= thr_sel
        sel_bias = jnp.where(sel, 0.0, MASK_NEG)
        ss = [lax.dot_general(kk, qs[h], _NT, preferred_element_type=F32) + sel_bias
              for h in range(B_Q_HEADS)]
        ms_new, ls_new, alphas, ps = [], [], [], []
        for h in range(B_Q_HEADS):
            m_new = jnp.maximum(ms[h], jnp.max(ss[h], axis=0, keepdims=True))
            alpha = jnp.exp2(ms[h] - m_new)
            p = jnp.exp2(ss[h] - m_new)
            ls_new.append(alpha * ls[h] + jnp.sum(p, axis=0, keepdims=True))
            ms_new.append(m_new)
            alphas.append(alpha)
            ps.append(p.astype(BF16))
        for h in range(B_Q_HEADS):
            acc_scr[h] = alphas[h] * acc_scr[h] + jnp.dot(vt, ps[h], preferred_element_type=F32)
        return tuple(ms_new), tuple(ls_new)

    init = (tuple(jnp.full((1, tq), MASK_NEG, F32) for _ in range(B_Q_HEADS)),
            tuple(jnp.zeros((1, tq), F32) for _ in range(B_Q_HEADS)))
    _, ls = lax.fori_loop(0, npairs, attend, init)

    outs = []
    for h in range(0, B_Q_HEADS, 2):
        pair = jnp.concatenate([acc_scr[h] / ls[h], acc_scr[h + 1] / ls[h + 1]], axis=0)
        outs.append(pair.T)
    o_ref[0] = jnp.concatenate(outs, axis=-1).astype(BF16)


def _dsa_attention(bq, biq, iwt, misc, vt):
    bsz, s, _ = bq.shape
    tq = DSA_TQ
    qspec = pl.BlockSpec((1, tq, B_Q_HEADS * HEAD_DIM), lambda b, i: (b, i, 0))
    return pl.pallas_call(
        _dsa_kernel,
        grid=(bsz, s // tq),
        in_specs=[qspec, qspec,
                  pl.BlockSpec((1, B_IDX_HEADS, tq), lambda b, i: (b, 0, i)),
                  pl.BlockSpec((1, s, misc.shape[2]), lambda b, i: (b, 0, 0)),
                  pl.BlockSpec((1, HEAD_DIM, s), lambda b, i: (b, 0, 0))],
        out_specs=qspec,
        out_shape=jax.ShapeDtypeStruct(bq.shape, BF16),
        scratch_shapes=[pltpu.VMEM((s // DSA_KC, DSA_KC, tq), jnp.int32),
                        pltpu.VMEM((s // DSA_KC, DSA_KC, tq), jnp.int16),
                        pltpu.VMEM((s // DSA_KC, DSA_KC, tq), jnp.int16),
                        pltpu.VMEM((B_Q_HEADS, HEAD_DIM, tq), F32)],
        compiler_params=_params(("parallel", "arbitrary")),
        name="dsa_attention",
    )(bq, biq, iwt, misc, vt)


POST_TM = 512
POST_ROWS = 256
POST_TF = 1024
POST_VMEM_LIMIT = 56 * 1024 * 1024


def _post_attn_kernel(*refs, widths):
    n = len(widths)
    parts = refs[:n]
    (wout_ref, x_ref, gate1_ref, g1_ref, b1_ref, shift2_ref, scale2_ref, gate2_ref,
     w1_ref, w2_ref, g2_ref, b2_ref, o_ref) = refs[n:]
    groups = [slice(r * POST_ROWS, (r + 1) * POST_ROWS) for r in range(x_ref.shape[1] // POST_ROWS)]

    def out_proj(rows):
        y = None
        off = 0
        for p_ref, wd in zip(parts, widths):
            t = jnp.dot(p_ref[0, rows, :], wout_ref[off:off + wd, :], preferred_element_type=F32)
            y = t if y is None else y + t
            off += wd
        return y

    def norm1(rows, y):
        x1 = _layer_norm(DN_ALPHA * x_ref[0, rows, :] + gate1_ref[0] * y, g1_ref[...], b1_ref[...])
        return x1, (x1 * (1.0 + scale2_ref[0]) + shift2_ref[0]).astype(BF16)

    def mlp(h):
        acc = None
        for f in range(w1_ref.shape[1] // POST_TF):
            cols = slice(f * POST_TF, (f + 1) * POST_TF)
            u = jnp.maximum(jnp.dot(h, w1_ref[:, cols], preferred_element_type=F32), 0.0)
            t = jnp.dot((u * u).astype(BF16), w2_ref[cols, :], preferred_element_type=F32)
            acc = t if acc is None else acc + t
        return acc

    ys = [out_proj(rows) for rows in groups]
    x1s, accs = [], []
    for rows, y in zip(groups, ys):
        x1, h = norm1(rows, y)
        x1s.append(x1)
        accs.append(mlp(h))
    for rows, x1, acc in zip(groups, x1s, accs):
        o_ref[0, rows, :] = _layer_norm(DN_ALPHA * x1 + gate2_ref[0] * acc, g2_ref[...], b2_ref[...])


def _post_attn(parts, w_out, x, gate1, g1, b1, shift2, scale2, gate2, w1, w2, g2, b2):
    bsz, s, d = x.shape
    tm = POST_TM
    widths = tuple(p.shape[-1] for p in parts)
    row = lambda w: pl.BlockSpec((1, tm, w), lambda bi, i: (bi, i, 0))
    vec = pl.BlockSpec((1, 1, d), lambda bi, i: (bi, 0, 0))
    full = lambda a: pl.BlockSpec(a.shape, lambda bi, i: (0,) * a.ndim, pipeline_mode=pl.Buffered(1))
    return pl.pallas_call(
        functools.partial(_post_attn_kernel, widths=widths),
        grid=(bsz, s // tm),
        in_specs=([row(w) for w in widths]
                  + [full(w_out), row(d), vec, full(g1), full(b1), vec, vec, vec,
                     full(w1), full(w2), full(g2), full(b2)]),
        out_specs=row(d),
        out_shape=jax.ShapeDtypeStruct(x.shape, F32),
        compiler_params=pltpu.CompilerParams(dimension_semantics=("parallel", "parallel"),
                                             vmem_limit_bytes=POST_VMEM_LIMIT),
        name="post_attn",
    )(*parts, w_out, x, gate1, g1, b1, shift2, scale2, gate2, w1, w2, g2, b2)


def _proj_c_kernel(x_ref, shift_ref, scale_ref, tab_ref, win_ref, q_ref, k_ref, vt_ref, km_ref):
    h = x_ref[0] * (1.0 + scale_ref[0]) + shift_ref[0]
    proj = jnp.dot(h.astype(BF16), win_ref[...], preferred_element_type=F32)
    cw = C_HEADS * HEAD_DIM
    cos, sa, sb = _rope_coeffs(tab_ref[0])
    q_ref[0] = (_rope(proj[:, 0:cw], cos, sa, sb) * ATT_SCALE).astype(BF16)
    k = _rope(proj[:, cw:2 * cw], cos, sa, sb)
    k_ref[0] = k.astype(BF16)
    vt_ref[0] = proj[:, 2 * cw:3 * cw].T.astype(BF16)
    for t in range(k.shape[0] // C_BLOCK):
        km_ref[0, t] = jnp.mean(k[t * C_BLOCK:(t + 1) * C_BLOCK], axis=0, keepdims=True)


PROJ_C_TM = 2 * C_BLOCK


def _proj_c(x, shift, scale, rope_tab, w_in):
    bsz, s, d = x.shape
    tm = PROJ_C_TM
    nbt = tm // C_BLOCK
    cw = C_HEADS * HEAD_DIM
    row = lambda w: pl.BlockSpec((1, tm, w), lambda b, i: (b, i, 0))
    vec = pl.BlockSpec((1, 1, d), lambda b, i: (b, 0, 0))
    full = lambda a: pl.BlockSpec(a.shape, lambda b, i: (0,) * a.ndim)
    qkv = jax.ShapeDtypeStruct((bsz, s, cw), BF16)
    return pl.pallas_call(
        _proj_c_kernel,
        grid=(bsz, s // tm),
        in_specs=[row(d), vec, vec, pl.BlockSpec((1, 128, tm), lambda b, i: (b, 0, i)), full(w_in)],
        out_specs=[row(cw), row(cw), pl.BlockSpec((1, cw, tm), lambda b, i: (b, 0, i)),
                   pl.BlockSpec((1, nbt, 1, cw), lambda b, i: (b, i, 0, 0))],
        out_shape=[qkv, qkv, jax.ShapeDtypeStruct((bsz, cw, s), BF16),
                   jax.ShapeDtypeStruct((bsz, s // C_BLOCK, 1, cw), F32)],
        compiler_params=_params(("parallel", "parallel")),
        name="proj_c",
    )(x, shift, scale, rope_tab, w_in)


MOBA_HEADS_PER_STEP = 16


def _moba_kernel(q_ref, k_ref, vt_ref, km_ref, o_ref, bias_scr, acc_scr):
    i = pl.program_id(2)
    tq = q_ref.shape[1]
    nb = k_ref.shape[1] // C_BLOCK
    nh = MOBA_HEADS_PER_STEP
    hsl = [slice(hh * HEAD_DIM, (hh + 1) * HEAD_DIM) for hh in range(nh)]
    qs = [q_ref[0, :, hsl[hh]] for hh in range(nh)]

    n_idx = lax.broadcasted_iota(jnp.int32, (nb, tq), 0)
    for hh in range(nh):
        gate = lax.dot_general(km_ref[0, hh].astype(BF16), qs[hh], _NT, preferred_element_type=F32)
        cnt = jnp.zeros((nb, tq), F32)
        for m_idx in range(nb - 1):
            other = gate[m_idx:m_idx + 1, :]
            tie = jnp.where(n_idx > m_idx, 1.0, 0.0)
            beats = jnp.where(other > gate, 1.0, jnp.where(other == gate, tie, 0.0))
            cnt = cnt + jnp.where(m_idx < i, beats, 0.0)
        bias_scr[hh] = jnp.where(n_idx < i, jnp.where(cnt < float(C_TOPK), 0.0, MASK_NEG), MASK_NEG)

    def block_update(blk, mask, bias, ms, ls):
        k0 = pl.multiple_of(blk * C_BLOCK, C_BLOCK)
        ss = []
        for hh in range(nh):
            kj = k_ref[0, pl.ds(k0, C_BLOCK), hsl[hh]]
            ss.append(mask(lax.dot_general(kj, qs[hh], _NT, preferred_element_type=F32)))
        ms_new, ls_new, alphas, ps = [], [], [], []
        for hh in range(nh):
            b = bias(hh)
            m_new = jnp.maximum(ms[hh], jnp.max(ss[hh], axis=0, keepdims=True) + b)
            alpha = jnp.exp2(ms[hh] - m_new)
            p = jnp.exp2(ss[hh] - (m_new - b))
            ls_new.append(alpha * ls[hh] + jnp.sum(p, axis=0, keepdims=True))
            ms_new.append(m_new)
            alphas.append(alpha)
            ps.append(p.astype(BF16))
        for hh in range(nh):
            vj = vt_ref[0, hsl[hh], pl.ds(k0, C_BLOCK)]
            acc_scr[hh] = alphas[hh] * acc_scr[hh] + jnp.dot(vj, ps[hh], preferred_element_type=F32)
        return tuple(ms_new), tuple(ls_new)

    krow = lax.broadcasted_iota(jnp.int32, (C_BLOCK, tq), 0)
    qcol = lax.broadcasted_iota(jnp.int32, (C_BLOCK, tq), 1)
    causal = krow <= qcol
    acc_scr[...] = jnp.zeros_like(acc_scr)
    ms0 = tuple(jnp.full((1, tq), MASK_NEG, F32) for _ in range(nh))
    ls0 = tuple(jnp.zeros((1, tq), F32) for _ in range(nh))
    carry = block_update(i, lambda s: jnp.where(causal, s, MASK_NEG), lambda hh: 0.0, ms0, ls0)

    def past_block(j, carry):
        return block_update(j, lambda s: s, lambda hh: bias_scr[hh, pl.ds(j, 1), :], carry[0], carry[1])

    _, ls = lax.fori_loop(0, i, past_block, carry)

    outs = []
    for hh in range(0, nh, 2):
        pair = jnp.concatenate([acc_scr[hh] / ls[hh], acc_scr[hh + 1] / ls[hh + 1]], axis=0)
        outs.append(pair.T)
    o_ref[0] = jnp.concatenate(outs, axis=-1).astype(BF16)


def _moba_attention(q, k, vt, kmean):
    bsz, s, cw = q.shape
    tq = C_BLOCK
    nh = MOBA_HEADS_PER_STEP
    w = nh * HEAD_DIM
    nb = s // C_BLOCK
    qspec = pl.BlockSpec((1, tq, w), lambda b, hg, i: (b, i, hg))
    return pl.pallas_call(
        _moba_kernel,
        grid=(bsz, cw // w, s // tq),
        in_specs=[qspec,
                  pl.BlockSpec((1, s, w), lambda b, hg, i: (b, 0, hg)),
                  pl.BlockSpec((1, w, s), lambda b, hg, i: (b, hg, 0)),
                  pl.BlockSpec((1, nh, nb, HEAD_DIM), lambda b, hg, i: (b, hg, 0, 0))],
        out_specs=qspec,
        out_shape=jax.ShapeDtypeStruct(q.shape, BF16),
        scratch_shapes=[pltpu.VMEM((nh, nb, tq), F32), pltpu.VMEM((nh, HEAD_DIM, tq), F32)],
        compiler_params=_params(("parallel", "parallel", "arbitrary")),
        name="moba_attention",
    )(q, k, vt, kmean)


def _rope_table(positions):
    inv = ROPE_THETA ** (-jnp.arange(0, HEAD_DIM, 2, dtype=F32) / HEAD_DIM)
    ang = positions.astype(F32)[:, None, :] * inv[None, :, None]
    cos, sin = lax.optimization_barrier((jnp.cos(ang), jnp.sin(ang)))
    return jnp.concatenate([cos, sin, cos, sin], axis=1)


def kernel(x, c, positions, ab_w_in, ab_q_norm, ab_w_uq, ab_w_uiq, ab_sinks, ab_w_out, c_w_in, c_w_out,
           ada_w, ada_b, ln_g, ln_b, mlp_w1, mlp_w2):
    bsz, s, d = x.shape
    rope_tab = _rope_table(positions)
    mod = _ada_modulation(c, ada_w, ada_b)

    def mods(idx):
        m = mod[idx]
        return m[:, None, 0:d], m[:, None, d:2 * d], m[:, None, 2 * d:3 * d] + 1.0

    for layer in range(DEPTH):
        shift, scale, gate = mods(2 * layer)
        shift2, scale2, gate2 = mods(2 * layer + 1)
        post = lambda parts, w_out: _post_attn(
            parts, w_out.astype(BF16), x, gate, ln_g[layer, 0][None], ln_b[layer, 0][None],
            shift2, scale2, gate2, mlp_w1[layer].astype(BF16), mlp_w2[layer].astype(BF16),
            ln_g[layer, 1][None], ln_b[layer, 1][None])
        if layer % 2 == 0:
            e = layer // 2
            w_in = jnp.pad(ab_w_in[e], ((0, 0), (0, AB_IN_PAD - AB_IN_WIDTH))).astype(BF16)
            aq, ak, avt, bq, biq, misc, bvt, iwt = _proj_ab(
                x, shift, scale, rope_tab, w_in, ab_q_norm[e][None],
                ab_w_uq[e].astype(BF16), ab_w_uiq[e].astype(BF16))
            ya = _swa_attention(ab_sinks[e], aq, ak, avt)
            yb = _dsa_attention(bq, biq, iwt, misc, bvt)
            x = post([ya, yb], ab_w_out[e])
        else:
            o = layer // 2
            q, k, vt, kmean = _proj_c(x, shift, scale, rope_tab, c_w_in[o].astype(BF16))
            km = kmean.reshape(bsz, s // C_BLOCK, C_HEADS, HEAD_DIM).transpose(0, 2, 1, 3)
            y = _moba_attention(q, k, vt, km)
            x = post([y], c_w_out[o])
    return x
```

```python
import functools

import jax
import jax.numpy as jnp
from jax import lax
from jax.experimental import pallas as pl
from jax.experimental.pallas import tpu as pltpu

HEAD_DIM = 64
ROPE_THETA = 10000.0
DEPTH = 2
A_Q_HEADS = 8
A_KV_HEADS = 2
A_BLOCK = 128
B_Q_HEADS = 8
B_IDX_HEADS = 8
B_TOPK = 256
C_HEADS = 16
C_BLOCK = 256
C_TOPK = 3
DN_ALPHA = (2 * DEPTH) ** 0.25
LN_EPS = 1e-5
RMS_EPS = 1e-6
AB_IN_WIDTH = 1224
AB_IN_PAD = 1280
LOG2E = 1.4426950408889634
ATT_SCALE = HEAD_DIM ** -0.5 * LOG2E
IDX_SCALE = B_IDX_HEADS ** -0.5 * HEAD_DIM ** -0.5

F32 = jnp.float32
BF16 = jnp.bfloat16
MASK_NEG = -1e30
INT_MIN = -(2 ** 31)
HALF = 2 ** 15
VMEM_LIMIT = 48 * 1024 * 1024

_NT = (((1,), (1,)), ((), ()))


def _params(sem):
    return pltpu.CompilerParams(dimension_semantics=sem, vmem_limit_bytes=VMEM_LIMIT)


def _rope(t, cos, sin_a, sin_b):
    outs = []
    for k in range(t.shape[-1] // 128):
        blk = t[:, 128 * k:128 * (k + 1)]
        outs.append(blk * cos + pltpu.roll(blk, 32, 1) * sin_a + pltpu.roll(blk, 96, 1) * sin_b)
    return outs[0] if len(outs) == 1 else jnp.concatenate(outs, axis=-1)


def _rope_coeffs(tab_t):
    tab = tab_t.T
    low = (lax.broadcasted_iota(jnp.int32, tab.shape, 1) & (HEAD_DIM - 1)) < HEAD_DIM // 2
    swapped = pltpu.roll(tab, HEAD_DIM // 2, 1)
    cos = jnp.where(low, tab, swapped)
    sin = jnp.where(low, swapped, tab)
    return cos, jnp.where(low, 0.0, sin), jnp.where(low, -sin, 0.0)


def _layer_norm(z, g, b):
    mu = jnp.mean(z, axis=-1, keepdims=True)
    zc = z - mu
    var = jnp.mean(zc * zc, axis=-1, keepdims=True)
    return zc * lax.rsqrt(var + LN_EPS) * g + b


def _ada_kernel(c_ref, w_ref, b_ref, o_ref):
    c = c_ref[...]
    sc = c / (1.0 + jnp.exp(-c))
    o_ref[0] = jnp.dot(sc, w_ref[0], preferred_element_type=F32) + b_ref[0]


def _ada_modulation(c, ada_w, ada_b):
    n = ada_w.shape[0] * ada_w.shape[1]
    bsz, d = c.shape
    w = ada_w.reshape(n, d, 3 * d)
    b = ada_b.reshape(n, 1, 3 * d)
    tn = 1024
    return pl.pallas_call(
        _ada_kernel,
        grid=(n, 3 * d // tn),
        in_specs=[
            pl.BlockSpec((bsz, d), lambda i, j: (0, 0)),
            pl.BlockSpec((1, d, tn), lambda i, j: (i, 0, j)),
            pl.BlockSpec((1, 1, tn), lambda i, j: (i, 0, j)),
        ],
        out_specs=pl.BlockSpec((1, bsz, tn), lambda i, j: (i, 0, j)),
        out_shape=jax.ShapeDtypeStruct((n, bsz, 3 * d), F32),
        compiler_params=_params(("arbitrary", "arbitrary")),
        name="ada_modulation",
    )(c, w, b)


def _proj_ab_kernel(x_ref, shift_ref, scale_ref, tab_ref, win_ref, qn_ref, wuq_ref, wuiq_ref,
                    aq_ref, ak_ref, avt_ref, bq_ref, biq_ref, misc_ref, bvt_ref, iwt_ref):
    h = x_ref[0] * (1.0 + scale_ref[0]) + shift_ref[0]
    proj = jnp.dot(h.astype(BF16), win_ref[...], preferred_element_type=F32)
    cos, sa, sb = _rope_coeffs(tab_ref[0])

    aq_ref[0] = (_rope(proj[:, 0:512], cos, sa, sb) * ATT_SCALE).astype(BF16)
    ak_ref[0] = _rope(proj[:, 512:640], cos, sa, sb).astype(BF16)
    avt_ref[0] = proj[:, 640:768].T.astype(BF16)

    cq = proj[:, 768:1024]
    ms = jnp.mean(cq * cq, axis=-1, keepdims=True)
    cqn = (cq * lax.rsqrt(ms + RMS_EPS) * qn_ref[...]).astype(BF16)
    bq = jnp.dot(cqn, wuq_ref[...], preferred_element_type=F32)
    biq = jnp.dot(cqn, wuiq_ref[...], preferred_element_type=F32)
    bq_ref[0] = (_rope(bq, cos, sa, sb) * ATT_SCALE).astype(BF16)
    biq_ref[0] = _rope(biq, cos, sa, sb).astype(BF16)

    lane = lax.broadcasted_iota(jnp.int32, cos.shape, 1)
    roped = lane < HEAD_DIM
    c0 = jnp.where(roped, cos, 1.0)
    a0 = jnp.where(roped, sa, 0.0)
    b0 = jnp.where(roped, sb, 0.0)
    m0 = _rope(proj[:, 1024:1152], c0, a0, b0)
    m1 = _rope(proj[:, 1152:1280], c0, a0, b0)
    misc_ref[0] = jnp.concatenate([m0, m1], axis=-1).astype(BF16)
    bvt_ref[0] = m0.T[HEAD_DIM:2 * HEAD_DIM].astype(BF16)
    iwt_ref[0] = m1.T[HEAD_DIM:HEAD_DIM + B_IDX_HEADS] * IDX_SCALE


def _proj_ab(x, shift, scale, rope_tab, w_in, q_norm, w_uq, w_uiq, tm=512):
    bsz, s, d = x.shape
    row = lambda w: pl.BlockSpec((1, tm, w), lambda b, i: (b, i, 0))
    vec = pl.BlockSpec((1, 1, d), lambda b, i: (b, 0, 0))
    full = lambda a: pl.BlockSpec(a.shape, lambda b, i: (0,) * a.ndim)
    col = lambda r: pl.BlockSpec((1, r, tm), lambda b, i: (b, 0, i))
    tok = lambda w: jax.ShapeDtypeStruct((bsz, s, w), BF16)
    kvw = A_KV_HEADS * HEAD_DIM
    out_shape = [tok(512), tok(kvw), jax.ShapeDtypeStruct((bsz, kvw, s), BF16), tok(512), tok(512), tok(256),
                 jax.ShapeDtypeStruct((bsz, HEAD_DIM, s), BF16), jax.ShapeDtypeStruct((bsz, B_IDX_HEADS, s), F32)]
    out_specs = [row(512), row(kvw), col(kvw), row(512), row(512), row(256), col(HEAD_DIM), col(B_IDX_HEADS)]
    return pl.pallas_call(
        _proj_ab_kernel,
        grid=(bsz, s // tm),
        in_specs=[row(d), vec, vec, col(128), full(w_in), full(q_norm), full(w_uq), full(w_uiq)],
        out_specs=out_specs,
        out_shape=out_shape,
        compiler_params=_params(("parallel", "parallel")),
        name="proj_ab",
    )(x, shift, scale, rope_tab, w_in, q_norm, w_uq, w_uiq)


SWA_BLOCKS = 4
SWA_TQ = SWA_BLOCKS * A_BLOCK


def _swa_kernel(sink_ref, q_ref, kp_ref, kc_ref, vtp_ref, vtc_ref, o_ref):
    i = pl.program_id(1)
    group = A_Q_HEADS // A_KV_HEADS
    kband = jnp.concatenate([kp_ref[0], kc_ref[0]], axis=0)
    vtband = jnp.concatenate([vtp_ref[0], vtc_ref[0]], axis=1)
    c = lax.broadcasted_iota(jnp.int32, (2 * A_BLOCK, A_BLOCK), 0)
    qi = lax.broadcasted_iota(jnp.int32, (2 * A_BLOCK, A_BLOCK), 1)
    in_window = (c > qi) & (c <= qi + A_BLOCK)
    later = jnp.where(in_window, 0.0, MASK_NEG)
    masks = [jnp.where(in_window & ((c >= A_BLOCK) | (i > 0)), 0.0, MASK_NEG)] + [later] * (SWA_BLOCKS - 1)

    ss = {}
    for t in range(SWA_BLOCKS):
        for kh in range(A_KV_HEADS):
            kb = kband[t * A_BLOCK:(t + 2) * A_BLOCK, kh * HEAD_DIM:(kh + 1) * HEAD_DIM]
            qstack = jnp.concatenate(
                [q_ref[0, t * A_BLOCK:(t + 1) * A_BLOCK, hq * HEAD_DIM:(hq + 1) * HEAD_DIM]
                 for hq in range(kh * group, (kh + 1) * group)], axis=0)
            s4 = lax.dot_general(kb, qstack, _NT, preferred_element_type=F32)
            for g in range(group):
                ss[t, kh * group + g] = s4[:, g * A_BLOCK:(g + 1) * A_BLOCK] + masks[t]
    ps, dens = {}, {}
    for t in range(SWA_BLOCKS):
        for hq in range(A_Q_HEADS):
            sink = sink_ref[hq] * LOG2E
            m = jnp.maximum(jnp.max(ss[t, hq], axis=0, keepdims=True), sink)
            p = jnp.exp2(ss[t, hq] - m)
            dens[t, hq] = jnp.sum(p, axis=0, keepdims=True) + jnp.exp2(sink - m)
            ps[t, hq] = p.astype(BF16)
    for t in range(SWA_BLOCKS):
        outs = []
        for hq in range(A_Q_HEADS):
            kh = hq // group
            vt = vtband[kh * HEAD_DIM:(kh + 1) * HEAD_DIM, t * A_BLOCK:(t + 2) * A_BLOCK]
            outs.append(jnp.dot(vt, ps[t, hq], preferred_element_type=F32) / dens[t, hq])
        o_ref[0, t * A_BLOCK:(t + 1) * A_BLOCK, :] = jnp.concatenate(outs, axis=0).T.astype(BF16)


def _swa_attention(sinks, aq, ak, avt):
    bsz, s, qw = aq.shape
    kvw = A_KV_HEADS * HEAD_DIM
    prev_blk = lambda i: jnp.maximum(SWA_BLOCKS * i - 1, 0)
    qspec = pl.BlockSpec((1, SWA_TQ, qw), lambda b, i: (b, i, 0))
    return pl.pallas_call(
        _swa_kernel,
        grid=(bsz, s // SWA_TQ),
        in_specs=[pl.BlockSpec(memory_space=pltpu.SMEM), qspec,
                  pl.BlockSpec((1, A_BLOCK, kvw), lambda b, i: (b, prev_blk(i), 0)),
                  pl.BlockSpec((1, SWA_TQ, kvw), lambda b, i: (b, i, 0)),
                  pl.BlockSpec((1, kvw, A_BLOCK), lambda b, i: (b, 0, prev_blk(i))),
                  pl.BlockSpec((1, kvw, SWA_TQ), lambda b, i: (b, 0, i))],
        out_specs=qspec,
        out_shape=jax.ShapeDtypeStruct(aq.shape, BF16),
        compiler_params=_params(("parallel", "parallel")),
        name="swa_attention",
    )(sinks, aq, ak, ak, avt, avt)


DSA_TQ = 256
DSA_KC = 256


def _dsa_kernel(*refs):
    npairs = (pl.program_id(1) + 2) // 2
    for n in range(1, refs[-4].shape[0] // 2 + 1):
        pl.when(npairs == n)(functools.partial(_dsa_body, n, *refs))


def _dsa_body(npairs, q_ref, iq_ref, iwt_ref, misc_ref, vt_ref, o_ref, key_scr, hi_scr, lo_scr, acc_scr):
    i = pl.program_id(1)
    tq, kc = DSA_TQ, DSA_KC
    nkc = i + 1
    krow = lax.broadcasted_iota(jnp.int32, (kc, tq), 0)
    qcol = lax.broadcasted_iota(jnp.int32, (kc, tq), 1)
    on_or_below_diag = krow <= qcol
    iwt = iwt_ref[0]
    hsl = [slice(h * HEAD_DIM, (h + 1) * HEAD_DIM) for h in range(B_Q_HEADS)]
    iqs = [iq_ref[0, :, hsl[h]] for h in range(B_IDX_HEADS)]
    qs = [q_ref[0, :, hsl[h]] for h in range(B_Q_HEADS)]


    def score_chunk(c):
        ik = misc_ref[0, pl.ds(c * kc, kc), 128:192]
        sc = jnp.zeros((kc, tq), F32)
        for h in range(B_IDX_HEADS):
            raw = lax.dot_general(ik, iqs[h], _NT, preferred_element_type=F32)
            sc = sc + jnp.maximum(raw, 0.0) * iwt[h:h + 1, :]
        sc = jnp.where(sc == 0.0, 0.0, sc)
        bits = lax.bitcast_convert_type(sc, jnp.int32)
        key = jnp.where(bits >= 0, bits, bits ^ jnp.int32(0x7FFFFFFF))
        diag_or_beyond = jnp.where(c == i, jnp.where(on_or_below_diag, key, INT_MIN), INT_MIN)
        key = jnp.where(c < i, key, diag_or_beyond)
        key_scr[c] = key
        hi_scr[c] = (key >> 16).astype(jnp.int16)
        lo_scr[c] = ((key & 0xFFFF) - HALF).astype(jnp.int16)

    for c in range(2 * npairs):
        score_chunk(c)

    def count_ge(ref, c, cand):
        ones = jnp.where(ref[c] >= cand.astype(jnp.int16), jnp.int16(1), jnp.int16(0))
        part = ones[0:16]
        for r in range(1, kc // 16):
            part = part + ones[16 * r:16 * (r + 1)]
        return jnp.sum(part.astype(jnp.int32), axis=0, keepdims=True)

    def half_search(ref, wanted):
        def step(it, thr):
            cand = thr + lax.shift_left(jnp.int32(1), 15 - it)
            cnt = count_ge(ref, 0, cand)
            for c in range(1, 2 * npairs):
                cnt = cnt + count_ge(ref, c, cand)
            return jnp.where(cnt >= wanted, cand, thr)

        return lax.fori_loop(0, 16, step, jnp.full((1, tq), -HALF, jnp.int32))

    thr_hi = half_search(hi_scr, B_TOPK)

    def mask_low(c, above):
        hi = hi_scr[c].astype(jnp.int32)
        lo_scr[c] = jnp.where(hi == thr_hi, lo_scr[c].astype(jnp.int32), -HALF).astype(jnp.int16)
        return above + jnp.sum(jnp.where(hi > thr_hi, 1, 0), axis=0, keepdims=True)

    above = lax.fori_loop(0, nkc, mask_low, jnp.zeros((1, tq), jnp.int32))
    thr_lo = half_search(lo_scr, B_TOPK - above)
    thr = (thr_hi << 16) | (thr_lo + HALF)

    def stats(c, carry):
        ngt, neq = carry
        k = key_scr[c]
        return (ngt + jnp.sum(jnp.where(k > thr, 1.0, 0.0), axis=0, keepdims=True),
                neq + jnp.sum(jnp.where(k == thr, 1.0, 0.0), axis=0, keepdims=True))

    zero_row = jnp.zeros((1, tq), F32)
    ngt, neq = lax.fori_loop(0, nkc, stats, (zero_row, zero_row))
    need = float(B_TOPK) - ngt

    @pl.when(jnp.max(neq - need) > 0.0)
    def _():
        ra = lax.broadcasted_iota(jnp.int32, (kc, kc), 0)
        rb = lax.broadcasted_iota(jnp.int32, (kc, kc), 1)
        lower = jnp.where(rb < ra, 1.0, 0.0).astype(BF16)

        def drop_late_ties(c, before):
            k = key_scr[c]
            eq = jnp.where(k == thr, 1.0, 0.0)
            prefix = jnp.dot(lower, eq.astype(BF16), preferred_element_type=F32) + before
            key_scr[c] = jnp.where(k == thr, jnp.where(prefix >= need, INT_MIN, k), k)
            return before + jnp.sum(eq, axis=0, keepdims=True)

        lax.fori_loop(0, nkc, drop_late_ties, zero_row)

    thr_sel = jnp.maximum(thr, INT_MIN + 1)

    acc_scr[...] = jnp.zeros_like(acc_scr)

    def attend(p, carry):
        ms, ls = carry
        kk = misc_ref[0, pl.ds(p * 2 * kc, 2 * kc), 0:64]
        vt = vt_ref[0, :, pl.ds(p * 2 * kc, 2 * kc)]
        sel = jnp.concatenate([key_scr[2 * p], key_scr[2 * p + 1]], axis=0) >= thr_sel
        sel_bias = jnp.where(sel, 0.0, MASK_NEG)
        ss = [lax.dot_general(kk, qs[h], _NT, preferred_element_type=F32) + sel_bias
              for h in range(B_Q_HEADS)]
        ms_new, ls_new, alphas, ps = [], [], [], []
        for h in range(B_Q_HEADS):
            m_new = jnp.maximum(ms[h], jnp.max(ss[h], axis=0, keepdims=True))
            alpha = jnp.exp2(ms[h] - m_new)
            p = jnp.exp2(ss[h] - m_new)
            ls_new.append(alpha * ls[h] + jnp.sum(p, axis=0, keepdims=True))
            ms_new.append(m_new)
            alphas.append(alpha)
            ps.append(p.astype(BF16))
        for h in range(B_Q_HEADS):
            acc_scr[h] = alphas[h] * acc_scr[h] + jnp.dot(vt, ps[h], preferred_element_type=F32)
        return tuple(ms_new), tuple(ls_new)

    init = (tuple(jnp.full((1, tq), MASK_NEG, F32) for _ in range(B_Q_HEADS)),
            tuple(jnp.zeros((1, tq), F32) for _ in range(B_Q_HEADS)))
    carry = init
    for p in range(npairs):
        carry = attend(p, carry)
    ls = carry[1]

    outs = []
    for h in range(0, B_Q_HEADS, 2):
        pair = jnp.concatenate([acc_scr[h] / ls[h], acc_scr[h + 1] / ls[h + 1]], axis=0)
        outs.append(pair.T)
    o_ref[0] = jnp.concatenate(outs, axis=-1).astype(BF16)


def _dsa_attention(bq, biq, iwt, misc, vt):
    bsz, s, _ = bq.shape
    tq = DSA_TQ
    qspec = pl.BlockSpec((1, tq, B_Q_HEADS * HEAD_DIM), lambda b, i: (b, i, 0))
    return pl.pallas_call(
        _dsa_kernel,
        grid=(bsz, s // tq),
        in_specs=[qspec, qspec,
                  pl.BlockSpec((1, B_IDX_HEADS, tq), lambda b, i: (b, 0, i)),
                  pl.BlockSpec((1, s, misc.shape[2]), lambda b, i: (b, 0, 0)),
                  pl.BlockSpec((1, HEAD_DIM, s), lambda b, i: (b, 0, 0))],
        out_specs=qspec,
        out_shape=jax.ShapeDtypeStruct(bq.shape, BF16),
        scratch_shapes=[pltpu.VMEM((s // DSA_KC, DSA_KC, tq), jnp.int32),
                        pltpu.VMEM((s // DSA_KC, DSA_KC, tq), jnp.int16),
                        pltpu.VMEM((s // DSA_KC, DSA_KC, tq), jnp.int16),
                        pltpu.VMEM((B_Q_HEADS, HEAD_DIM, tq), F32)],
        compiler_params=_params(("parallel", "arbitrary")),
        name="dsa_attention",
    )(bq, biq, iwt, misc, vt)


POST_TM = 512
POST_ROWS = 256
POST_TF = 1024
POST_VMEM_LIMIT = 56 * 1024 * 1024


def _post_attn_kernel(*refs, widths):
    n = len(widths)
    parts = refs[:n]
    (wout_ref, x_ref, gate1_ref, g1_ref, b1_ref, shift2_ref, scale2_ref, gate2_ref,
     w1_ref, w2_ref, g2_ref, b2_ref, o_ref) = refs[n:]
    groups = [slice(r * POST_ROWS, (r + 1) * POST_ROWS) for r in range(x_ref.shape[1] // POST_ROWS)]

    def out_proj(rows):
        y = None
        off = 0
        for p_ref, wd in zip(parts, widths):
            t = jnp.dot(p_ref[0, rows, :], wout_ref[off:off + wd, :], preferred_element_type=F32)
            y = t if y is None else y + t
            off += wd
        return y

    def norm1(rows, y):
        x1 = _layer_norm(DN_ALPHA * x_ref[0, rows, :] + gate1_ref[0] * y, g1_ref[...], b1_ref[...])
        return x1, (x1 * (1.0 + scale2_ref[0]) + shift2_ref[0]).astype(BF16)

    def mlp(h):
        acc = None
        for f in range(w1_ref.shape[1] // POST_TF):
            cols = slice(f * POST_TF, (f + 1) * POST_TF)
            u = jnp.maximum(jnp.dot(h, w1_ref[:, cols], preferred_element_type=F32), 0.0)
            t = jnp.dot((u * u).astype(BF16), w2_ref[cols, :], preferred_element_type=F32)
            acc = t if acc is None else acc + t
        return acc

    ys = [out_proj(rows) for rows in groups]
    x1s, accs = [], []
    for rows, y in zip(groups, ys):
        x1, h = norm1(rows, y)
        x1s.append(x1)
        accs.append(mlp(h))
    for rows, x1, acc in zip(groups, x1s, accs):
        o_ref[0, rows, :] = _layer_norm(DN_ALPHA * x1 + gate2_ref[0] * acc, g2_ref[...], b2_ref[...])


def _post_attn(parts, w_out, x, gate1, g1, b1, shift2, scale2, gate2, w1, w2, g2, b2):
    bsz, s, d = x.shape
    tm = POST_TM
    widths = tuple(p.shape[-1] for p in parts)
    row = lambda w: pl.BlockSpec((1, tm, w), lambda bi, i: (bi, i, 0))
    vec = pl.BlockSpec((1, 1, d), lambda bi, i: (bi, 0, 0))
    full = lambda a: pl.BlockSpec(a.shape, lambda bi, i: (0,) * a.ndim, pipeline_mode=pl.Buffered(1))
    return pl.pallas_call(
        functools.partial(_post_attn_kernel, widths=widths),
        grid=(bsz, s // tm),
        in_specs=([row(w) for w in widths]
                  + [full(w_out), row(d), vec, full(g1), full(b1), vec, vec, vec,
                     full(w1), full(w2), full(g2), full(b2)]),
        out_specs=row(d),
        out_shape=jax.ShapeDtypeStruct(x.shape, F32),
        compiler_params=pltpu.CompilerParams(dimension_semantics=("parallel", "parallel"),
                                             vmem_limit_bytes=POST_VMEM_LIMIT),
        name="post_attn",
    )(*parts, w_out, x, gate1, g1, b1, shift2, scale2, gate2, w1, w2, g2, b2)


def _proj_c_kernel(x_ref, shift_ref, scale_ref, tab_ref, win_ref, q_ref, k_ref, vt_ref, km_ref):
    h = x_ref[0] * (1.0 + scale_ref[0]) + shift_ref[0]
    proj = jnp.dot(h.astype(BF16), win_ref[...], preferred_element_type=F32)
    cw = C_HEADS * HEAD_DIM
    cos, sa, sb = _rope_coeffs(tab_ref[0])
    q_ref[0] = (_rope(proj[:, 0:cw], cos, sa, sb) * ATT_SCALE).astype(BF16)
    k = _rope(proj[:, cw:2 * cw], cos, sa, sb)
    k_ref[0] = k.astype(BF16)
    vt_ref[0] = proj[:, 2 * cw:3 * cw].T.astype(BF16)
    for t in range(k.shape[0] // C_BLOCK):
        km_ref[0, t] = jnp.mean(k[t * C_BLOCK:(t + 1) * C_BLOCK], axis=0, keepdims=True)


PROJ_C_TM = 2 * C_BLOCK


def _proj_c(x, shift, scale, rope_tab, w_in):
    bsz, s, d = x.shape
    tm = PROJ_C_TM
    nbt = tm // C_BLOCK
    cw = C_HEADS * HEAD_DIM
    row = lambda w: pl.BlockSpec((1, tm, w), lambda b, i: (b, i, 0))
    vec = pl.BlockSpec((1, 1, d), lambda b, i: (b, 0, 0))
    full = lambda a: pl.BlockSpec(a.shape, lambda b, i: (0,) * a.ndim)
    qkv = jax.ShapeDtypeStruct((bsz, s, cw), BF16)
    return pl.pallas_call(
        _proj_c_kernel,
        grid=(bsz, s // tm),
        in_specs=[row(d), vec, vec, pl.BlockSpec((1, 128, tm), lambda b, i: (b, 0, i)), full(w_in)],
        out_specs=[row(cw), row(cw), pl.BlockSpec((1, cw, tm), lambda b, i: (b, 0, i)),
                   pl.BlockSpec((1, nbt, 1, cw), lambda b, i: (b, i, 0, 0))],
        out_shape=[qkv, qkv, jax.ShapeDtypeStruct((bsz, cw, s), BF16),
                   jax.ShapeDtypeStruct((bsz, s // C_BLOCK, 1, cw), F32)],
        compiler_params=_params(("parallel", "parallel")),
        name="proj_c",
    )(x, shift, scale, rope_tab, w_in)


MOBA_HEADS_PER_STEP = 16


def _moba_kernel(q_ref, k_ref, vt_ref, km_ref, o_ref, bias_scr, acc_scr):
    i = pl.program_id(2)
    tq = q_ref.shape[1]
    nb = k_ref.shape[1] // C_BLOCK
    nh = MOBA_HEADS_PER_STEP
    hsl = [slice(hh * HEAD_DIM, (hh + 1) * HEAD_DIM) for hh in range(nh)]
    qs = [q_ref[0, :, hsl[hh]] for hh in range(nh)]

    n_idx = lax.broadcasted_iota(jnp.int32, (nb, tq), 0)
    for hh in range(nh):
        gate = lax.dot_general(km_ref[0, hh].astype(BF16), qs[hh], _NT, preferred_element_type=F32)
        cnt = jnp.zeros((nb, tq), F32)
        for m_idx in range(nb - 1):
            other = gate[m_idx:m_idx + 1, :]
            tie = jnp.where(n_idx > m_idx, 1.0, 0.0)
            beats = jnp.where(other > gate, 1.0, jnp.where(other == gate, tie, 0.0))
            cnt = cnt + jnp.where(m_idx < i, beats, 0.0)
        bias_scr[hh] = jnp.where(n_idx < i, jnp.where(cnt < float(C_TOPK), 0.0, MASK_NEG), MASK_NEG)

    def block_update(blk, mask, bias, ms, ls):
        k0 = pl.multiple_of(blk * C_BLOCK, C_BLOCK)
        ss = []
        for hh in range(nh):
            kj = k_ref[0, pl.ds(k0, C_BLOCK), hsl[hh]]
            ss.append(mask(lax.dot_general(kj, qs[hh], _NT, preferred_element_type=F32)))
        ms_new, ls_new, alphas, ps = [], [], [], []
        for hh in range(nh):
            b = bias(hh)
            m_new = jnp.maximum(ms[hh], jnp.max(ss[hh], axis=0, keepdims=True) + b)
            alpha = jnp.exp2(ms[hh] - m_new)
            p = jnp.exp2(ss[hh] - (m_new - b))
            ls_new.append(alpha * ls[hh] + jnp.sum(p, axis=0, keepdims=True))
            ms_new.append(m_new)
            alphas.append(alpha)
            ps.append(p.astype(BF16))
        for hh in range(nh):
            vj = vt_ref[0, hsl[hh], pl.ds(k0, C_BLOCK)]
            acc_scr[hh] = alphas[hh] * acc_scr[hh] + jnp.dot(vj, ps[hh], preferred_element_type=F32)
        return tuple(ms_new), tuple(ls_new)

    krow = lax.broadcasted_iota(jnp.int32, (C_BLOCK, tq), 0)
    qcol = lax.broadcasted_iota(jnp.int32, (C_BLOCK, tq), 1)
    causal = krow <= qcol
    acc_scr[...] = jnp.zeros_like(acc_scr)
    ms0 = tuple(jnp.full((1, tq), MASK_NEG, F32) for _ in range(nh))
    ls0 = tuple(jnp.zeros((1, tq), F32) for _ in range(nh))
    carry = block_update(i, lambda s: jnp.where(causal, s, MASK_NEG), lambda hh: 0.0, ms0, ls0)

    def past_block(j, carry):
        return block_update(j, lambda s: s, lambda hh: bias_scr[hh, pl.ds(j, 1), :], carry[0], carry[1])

    _, ls = lax.fori_loop(0, i, past_block, carry)

    outs = []
    for hh in range(0, nh, 2):
        pair = jnp.concatenate([acc_scr[hh] / ls[hh], acc_scr[hh + 1] / ls[hh + 1]], axis=0)
        outs.append(pair.T)
    o_ref[0] = jnp.concatenate(outs, axis=-1).astype(BF16)


def _moba_attention(q, k, vt, kmean):
    bsz, s, cw = q.shape
    tq = C_BLOCK
    nh = MOBA_HEADS_PER_STEP
    w = nh * HEAD_DIM
    nb = s // C_BLOCK
    qspec = pl.BlockSpec((1, tq, w), lambda b, hg, i: (b, i, hg))
    return pl.pallas_call(
        _moba_kernel,
        grid=(bsz, cw // w, s // tq),
        in_specs=[qspec,
                  pl.BlockSpec((1, s, w), lambda b, hg, i: (b, 0, hg)),
                  pl.BlockSpec((1, w, s), lambda b, hg, i: (b, hg, 0)),
                  pl.BlockSpec((1, nh, nb, HEAD_DIM), lambda b, hg, i: (b, hg, 0, 0))],
        out_specs=qspec,
        out_shape=jax.ShapeDtypeStruct(q.shape, BF16),
        scratch_shapes=[pltpu.VMEM((nh, nb, tq), F32), pltpu.VMEM((nh, HEAD_DIM, tq), F32)],
        compiler_params=_params(("parallel", "parallel", "arbitrary")),
        name="moba_attention",
    )(q, k, vt, kmean)


def _rope_table(positions):
    inv = ROPE_THETA ** (-jnp.arange(0, HEAD_DIM, 2, dtype=F32) / HEAD_DIM)
    ang = positions.astype(F32)[:, None, :] * inv[None, :, None]
    cos, sin = lax.optimization_barrier((jnp.cos(ang), jnp.sin(ang)))
    return jnp.concatenate([cos, sin, cos, sin], axis=1)


def kernel(x, c, positions, ab_w_in, ab_q_norm, ab_w_uq, ab_w_uiq, ab_sinks, ab_w_out, c_w_in, c_w_out,
           ada_w, ada_b, ln_g, ln_b, mlp_w1, mlp_w2):
    bsz, s, d = x.shape
    rope_tab = _rope_table(positions)
    mod = _ada_modulation(c, ada_w, ada_b)

    def mods(idx):
        m = mod[idx]
        return m[:, None, 0:d], m[:, None, d:2 * d], m[:, None, 2 * d:3 * d] + 1.0

    for layer in range(DEPTH):
        shift, scale, gate = mods(2 * layer)
        shift2, scale2, gate2 = mods(2 * layer + 1)
        post = lambda parts, w_out: _post_attn(
            parts, w_out.astype(BF16), x, gate, ln_g[layer, 0][None], ln_b[layer, 0][None],
            shift2, scale2, gate2, mlp_w1[layer].astype(BF16), mlp_w2[layer].astype(BF16),
            ln_g[layer, 1][None], ln_b[layer, 1][None])
        if layer % 2 == 0:
            e = layer // 2
            w_in = jnp.pad(ab_w_in[e], ((0, 0), (0, AB_IN_PAD - AB_IN_WIDTH))).astype(BF16)
            aq, ak, avt, bq, biq, misc, bvt, iwt = _proj_ab(
                x, shift, scale, rope_tab, w_in, ab_q_norm[e][None],
                ab_w_uq[e].astype(BF16), ab_w_uiq[e].astype(BF16))
            ya = _swa_attention(ab_sinks[e], aq, ak, avt)
            yb = _dsa_attention(bq, biq, iwt, misc, bvt)
            x = post([ya, yb], ab_w_out[e])
        else:
            o = layer // 2
            q, k, vt, kmean = _proj_c(x, shift, scale, rope_tab, c_w_in[o].astype(BF16))
            km = kmean.reshape(bsz, s // C_BLOCK, C_HEADS, HEAD_DIM).transpose(0, 2, 1, 3)
            y = _moba_attention(q, k, vt, km)
            x = post([y], c_w_out[o])
    return x
```

```python
import functools

import jax
import jax.numpy as jnp
from jax import lax
from jax.experimental import pallas as pl
from jax.experimental.pallas import tpu as pltpu

HEAD_DIM = 64
ROPE_THETA = 10000.0
DEPTH = 2
A_Q_HEADS = 8
A_KV_HEADS = 2
A_BLOCK = 128
B_Q_HEADS = 8
B_IDX_HEADS = 8
B_TOPK = 256
C_HEADS = 16
C_BLOCK = 256
C_TOPK = 3
DN_ALPHA = (2 * DEPTH) ** 0.25
LN_EPS = 1e-5
RMS_EPS = 1e-6
AB_IN_WIDTH = 1224
AB_IN_PAD = 1280
LOG2E = 1.4426950408889634
ATT_SCALE = HEAD_DIM ** -0.5 * LOG2E
IDX_SCALE = B_IDX_HEADS ** -0.5 * HEAD_DIM ** -0.5

F32 = jnp.float32
BF16 = jnp.bfloat16
MASK_NEG = -1e30
NEG_INF = float("-inf")
F32_LOWEST = -3.4028234663852886e38
INT_MIN = -(2 ** 31)
HALF = 2 ** 15
VMEM_LIMIT = 48 * 1024 * 1024

_NT = (((1,), (1,)), ((), ()))


def _params(sem):
    return pltpu.CompilerParams(dimension_semantics=sem, vmem_limit_bytes=VMEM_LIMIT)


def _rope(t, cos, sin_a, sin_b):
    outs = []
    for k in range(t.shape[-1] // 128):
        blk = t[:, 128 * k:128 * (k + 1)]
        outs.append(blk * cos + pltpu.roll(blk, 32, 1) * sin_a + pltpu.roll(blk, 96, 1) * sin_b)
    return outs[0] if len(outs) == 1 else jnp.concatenate(outs, axis=-1)


def _rope_coeffs(tab_t):
    tab = tab_t.T
    low = (lax.broadcasted_iota(jnp.int32, tab.shape, 1) & (HEAD_DIM - 1)) < HEAD_DIM // 2
    swapped = pltpu.roll(tab, HEAD_DIM // 2, 1)
    cos = jnp.where(low, tab, swapped)
    sin = jnp.where(low, swapped, tab)
    return cos, jnp.where(low, 0.0, sin), jnp.where(low, -sin, 0.0)


def _layer_norm(z, g, b):
    mu = jnp.mean(z, axis=-1, keepdims=True)
    zc = z - mu
    var = jnp.mean(zc * zc, axis=-1, keepdims=True)
    return zc * lax.rsqrt(var + LN_EPS) * g + b


def _ada_kernel(c_ref, w_ref, b_ref, o_ref):
    c = c_ref[...]
    sc = c / (1.0 + jnp.exp(-c))
    o_ref[0] = jnp.dot(sc, w_ref[0], preferred_element_type=F32) + b_ref[0]


def _ada_modulation(c, ada_w, ada_b):
    n = ada_w.shape[0] * ada_w.shape[1]
    bsz, d = c.shape
    w = ada_w.reshape(n, d, 3 * d)
    b = ada_b.reshape(n, 1, 3 * d)
    tn = 1024
    return pl.pallas_call(
        _ada_kernel,
        grid=(n, 3 * d // tn),
        in_specs=[
            pl.BlockSpec((bsz, d), lambda i, j: (0, 0)),
            pl.BlockSpec((1, d, tn), lambda i, j: (i, 0, j)),
            pl.BlockSpec((1, 1, tn), lambda i, j: (i, 0, j)),
        ],
        out_specs=pl.BlockSpec((1, bsz, tn), lambda i, j: (i, 0, j)),
        out_shape=jax.ShapeDtypeStruct((n, bsz, 3 * d), F32),
        compiler_params=_params(("arbitrary", "arbitrary")),
        name="ada_modulation",
    )(c, w, b)


def _proj_ab_kernel(x_ref, shift_ref, scale_ref, tab_ref, win_ref, qn_ref, wuq_ref, wuiq_ref,
                    aq_ref, ak_ref, avt_ref, bq_ref, biq_ref, misc_ref, bvt_ref, iwt_ref):
    h = x_ref[0] * (1.0 + scale_ref[0]) + shift_ref[0]
    proj = jnp.dot(h.astype(BF16), win_ref[...], preferred_element_type=F32)
    cos, sa, sb = _rope_coeffs(tab_ref[0])

    aq_ref[0] = (_rope(proj[:, 0:512], cos, sa, sb) * ATT_SCALE).astype(BF16)
    ak_ref[0] = _rope(proj[:, 512:640], cos, sa, sb).astype(BF16)
    avt_ref[0] = proj[:, 640:768].T.astype(BF16)

    cq = proj[:, 768:1024]
    ms = jnp.mean(cq * cq, axis=-1, keepdims=True)
    cqn = (cq * lax.rsqrt(ms + RMS_EPS) * qn_ref[...]).astype(BF16)
    bq = jnp.dot(cqn, wuq_ref[...], preferred_element_type=F32)
    biq = jnp.dot(cqn, wuiq_ref[...], preferred_element_type=F32)
    bq_ref[0] = (_rope(bq, cos, sa, sb) * ATT_SCALE).astype(BF16)
    biq_ref[0] = _rope(biq, cos, sa, sb).astype(BF16)

    lane = lax.broadcasted_iota(jnp.int32, cos.shape, 1)
    roped = lane < HEAD_DIM
    c0 = jnp.where(roped, cos, 1.0)
    a0 = jnp.where(roped, sa, 0.0)
    b0 = jnp.where(roped, sb, 0.0)
    m0 = _rope(proj[:, 1024:1152], c0, a0, b0)
    m1 = _rope(proj[:, 1152:1280], c0, a0, b0)
    misc_ref[0] = jnp.concatenate([m0, m1], axis=-1).astype(BF16)
    bvt_ref[0] = m0.T[HEAD_DIM:2 * HEAD_DIM].astype(BF16)
    iwt_ref[0] = m1.T[HEAD_DIM:HEAD_DIM + B_IDX_HEADS] * IDX_SCALE


def _proj_ab(x, shift, scale, rope_tab, w_in, q_norm, w_uq, w_uiq, tm=512):
    bsz, s, d = x.shape
    row = lambda w: pl.BlockSpec((1, tm, w), lambda b, i: (b, i, 0))
    vec = pl.BlockSpec((1, 1, d), lambda b, i: (b, 0, 0))
    full = lambda a: pl.BlockSpec(a.shape, lambda b, i: (0,) * a.ndim)
    col = lambda r: pl.BlockSpec((1, r, tm), lambda b, i: (b, 0, i))
    tok = lambda w: jax.ShapeDtypeStruct((bsz, s, w), BF16)
    kvw = A_KV_HEADS * HEAD_DIM
    out_shape = [tok(512), tok(kvw), jax.ShapeDtypeStruct((bsz, kvw, s), BF16), tok(512), tok(512), tok(256),
                 jax.ShapeDtypeStruct((bsz, HEAD_DIM, s), BF16), jax.ShapeDtypeStruct((bsz, B_IDX_HEADS, s), F32)]
    out_specs = [row(512), row(kvw), col(kvw), row(512), row(512), row(256), col(HEAD_DIM), col(B_IDX_HEADS)]
    return pl.pallas_call(
        _proj_ab_kernel,
        grid=(bsz, s // tm),
        in_specs=[row(d), vec, vec, col(128), full(w_in), full(q_norm), full(w_uq), full(w_uiq)],
        out_specs=out_specs,
        out_shape=out_shape,
        compiler_params=_params(("parallel", "parallel")),
        name="proj_ab",
    )(x, shift, scale, rope_tab, w_in, q_norm, w_uq, w_uiq)


SWA_BLOCKS = 4
SWA_TQ = SWA_BLOCKS * A_BLOCK


def _swa_kernel(sink_ref, q_ref, kp_ref, kc_ref, vtp_ref, vtc_ref, o_ref):
    i = pl.program_id(1)
    group = A_Q_HEADS // A_KV_HEADS
    kband = jnp.concatenate([kp_ref[0], kc_ref[0]], axis=0)
    vtband = jnp.concatenate([vtp_ref[0], vtc_ref[0]], axis=1)
    c = lax.broadcasted_iota(jnp.int32, (2 * A_BLOCK, A_BLOCK), 0)
    qi = lax.broadcasted_iota(jnp.int32, (2 * A_BLOCK, A_BLOCK), 1)
    in_window = (c > qi) & (c <= qi + A_BLOCK)
    later = jnp.where(in_window, 0.0, MASK_NEG)
    masks = [jnp.where(in_window & ((c >= A_BLOCK) | (i > 0)), 0.0, MASK_NEG)] + [later] * (SWA_BLOCKS - 1)

    ss = {}
    for t in range(SWA_BLOCKS):
        for kh in range(A_KV_HEADS):
            kb = kband[t * A_BLOCK:(t + 2) * A_BLOCK, kh * HEAD_DIM:(kh + 1) * HEAD_DIM]
            qstack = jnp.concatenate(
                [q_ref[0, t * A_BLOCK:(t + 1) * A_BLOCK, hq * HEAD_DIM:(hq + 1) * HEAD_DIM]
                 for hq in range(kh * group, (kh + 1) * group)], axis=0)
            s4 = lax.dot_general(kb, qstack, _NT, preferred_element_type=F32)
            for g in range(group):
                ss[t, kh * group + g] = s4[:, g * A_BLOCK:(g + 1) * A_BLOCK] + masks[t]
    ps, dens = {}, {}
    for t in range(SWA_BLOCKS):
        for hq in range(A_Q_HEADS):
            sink = sink_ref[hq] * LOG2E
            m = jnp.maximum(jnp.max(ss[t, hq], axis=0, keepdims=True), sink)
            p = jnp.exp2(ss[t, hq] - m)
            dens[t, hq] = jnp.sum(p, axis=0, keepdims=True) + jnp.exp2(sink - m)
            ps[t, hq] = p.astype(BF16)
    for t in range(SWA_BLOCKS):
        outs = []
        for hq in range(A_Q_HEADS):
            kh = hq // group
            vt = vtband[kh * HEAD_DIM:(kh + 1) * HEAD_DIM, t * A_BLOCK:(t + 2) * A_BLOCK]
            outs.append(jnp.dot(vt, ps[t, hq], preferred_element_type=F32) / dens[t, hq])
        o_ref[0, t * A_BLOCK:(t + 1) * A_BLOCK, :] = jnp.concatenate(outs, axis=0).T.astype(BF16)


def _swa_attention(sinks, aq, ak, avt):
    bsz, s, qw = aq.shape
    kvw = A_KV_HEADS * HEAD_DIM
    prev_blk = lambda i: jnp.maximum(SWA_BLOCKS * i - 1, 0)
    qspec = pl.BlockSpec((1, SWA_TQ, qw), lambda b, i: (b, i, 0))
    return pl.pallas_call(
        _swa_kernel,
        grid=(bsz, s // SWA_TQ),
        in_specs=[pl.BlockSpec(memory_space=pltpu.SMEM), qspec,
                  pl.BlockSpec((1, A_BLOCK, kvw), lambda b, i: (b, prev_blk(i), 0)),
                  pl.BlockSpec((1, SWA_TQ, kvw), lambda b, i: (b, i, 0)),
                  pl.BlockSpec((1, kvw, A_BLOCK), lambda b, i: (b, 0, prev_blk(i))),
                  pl.BlockSpec((1, kvw, SWA_TQ), lambda b, i: (b, 0, i))],
        out_specs=qspec,
        out_shape=jax.ShapeDtypeStruct(aq.shape, BF16),
        compiler_params=_params(("parallel", "parallel")),
        name="swa_attention",
    )(sinks, aq, ak, ak, avt, avt)


DSA_TQ = 256
DSA_KC = 256


def _dsa_kernel(*refs):
    npairs = (pl.program_id(1) + 2) // 2
    for n in range(1, refs[-5].shape[0] // 2 + 1):
        pl.when(npairs == n)(functools.partial(_dsa_body, n, *refs))


def _dsa_body(npairs, q_ref, iq_ref, iwt_ref, misc_ref, vt_ref, o_ref, sc_scr, hi_scr, lo_scr, st_scr, acc_scr):
    i = pl.program_id(1)
    tq, kc = DSA_TQ, DSA_KC
    nkc = i + 1
    krow = lax.broadcasted_iota(jnp.int32, (kc, tq), 0)
    qcol = lax.broadcasted_iota(jnp.int32, (kc, tq), 1)
    on_or_below_diag = krow <= qcol
    iwt = iwt_ref[0]
    hsl = [slice(h * HEAD_DIM, (h + 1) * HEAD_DIM) for h in range(B_Q_HEADS)]
    iqs = [iq_ref[0, :, hsl[h]] for h in range(B_IDX_HEADS)]
    qs = [q_ref[0, :, hsl[h]] for h in range(B_Q_HEADS)]


    def ordered(bits):
        return jnp.where(bits >= 0, bits, bits ^ jnp.int32(0x7FFFFFFF))

    def key_to_float(key):
        return jnp.where(key == INT_MIN, NEG_INF, lax.bitcast_convert_type(ordered(key), F32))

    def score_chunk(c):
        ik = misc_ref[0, pl.ds(c * kc, kc), 128:192]
        sc = jnp.zeros((kc, tq), F32)
        for h in range(B_IDX_HEADS):
            raw = lax.dot_general(ik, iqs[h], _NT, preferred_element_type=F32)
            sc = sc + jnp.maximum(raw, 0.0) * iwt[h:h + 1, :]
        sc = jnp.where(sc == 0.0, 0.0, sc)
        key = ordered(lax.bitcast_convert_type(sc, jnp.int32))
        sc_scr[c] = jnp.where(c < i, sc, jnp.where(c == i, jnp.where(on_or_below_diag, sc, NEG_INF), NEG_INF))
        key = jnp.where(c < i, key, jnp.where(c == i, jnp.where(on_or_below_diag, key, INT_MIN), INT_MIN))
        hi_scr[c] = (key >> 16).astype(jnp.int16)
        lo_scr[c] = ((key & 0xFFFF) - HALF).astype(jnp.int16)

    for c in range(2 * npairs):
        score_chunk(c)

    def count_ge(ref, c, cand):
        ones = jnp.where(ref[c] >= cand.astype(jnp.int16), jnp.int16(1), jnp.int16(0))
        part = ones[0:16]
        for r in range(1, kc // 16):
            part = part + ones[16 * r:16 * (r + 1)]
        return jnp.sum(part.astype(jnp.int32), axis=0, keepdims=True)

    def half_search(ref, wanted):
        def step(it, thr):
            cand = thr + lax.shift_left(jnp.int32(1), 15 - it)
            cnt = count_ge(ref, 0, cand)
            for c in range(1, 2 * npairs):
                cnt = cnt + count_ge(ref, c, cand)
            return jnp.where(cnt >= wanted, cand, thr)

        return lax.fori_loop(0, 16, step, jnp.full((1, tq), -HALF, jnp.int32))

    thr_hi = half_search(hi_scr, B_TOPK)

    def mask_low(c, above):
        hi = hi_scr[c].astype(jnp.int32)
        lo_scr[c] = jnp.where(hi == thr_hi, lo_scr[c].astype(jnp.int32), -HALF).astype(jnp.int16)
        return above + jnp.sum(jnp.where(hi > thr_hi, 1, 0), axis=0, keepdims=True)

    above = lax.fori_loop(0, nkc, mask_low, jnp.zeros((1, tq), jnp.int32))
    thr_lo = half_search(lo_scr, B_TOPK - above)
    thr_guess = key_to_float((thr_hi << 16) | (thr_lo + HALF))

    zero_row = jnp.zeros((1, tq), F32)

    def count_gt_eq(t):
        def body(c, carry):
            ngt, neq = carry
            k = sc_scr[c]
            return (ngt + jnp.sum(jnp.where(k > t, 1.0, 0.0), axis=0, keepdims=True),
                    neq + jnp.sum(jnp.where(k == t, 1.0, 0.0), axis=0, keepdims=True))

        return lax.fori_loop(0, nkc, body, (zero_row, zero_row))

    def keep(t, ngt, neq):
        st_scr[0:1, :] = t
        st_scr[1:2, :] = ngt
        st_scr[2:3, :] = neq

    ngt, neq = count_gt_eq(thr_guess)
    keep(thr_guess, ngt, neq)
    is_kth = jnp.where(ngt < float(B_TOPK), jnp.where(ngt + neq >= float(B_TOPK), 1.0, 0.0), 0.0)

    @pl.when(jnp.min(is_kth) < 1.0)
    def _():
        def step(it, key):
            cand = key + lax.shift_left(jnp.int32(1), 31 - it)
            cand_f = lax.bitcast_convert_type(ordered(cand), F32)
            cnt = lax.fori_loop(
                0, nkc, lambda c, acc: acc + jnp.sum(jnp.where(sc_scr[c] >= cand_f, 1.0, 0.0), axis=0,
                                                      keepdims=True), zero_row)
            return jnp.where(cnt >= float(B_TOPK), cand, key)

        t = key_to_float(lax.fori_loop(0, 32, step, jnp.full((1, tq), INT_MIN, jnp.int32)))
        keep(t, *count_gt_eq(t))

    thr = st_scr[0:1, :]
    need = float(B_TOPK) - st_scr[1:2, :]

    @pl.when(jnp.max(st_scr[2:3, :] - need) > 0.0)
    def _():
        ra = lax.broadcasted_iota(jnp.int32, (kc, kc), 0)
        rb = lax.broadcasted_iota(jnp.int32, (kc, kc), 1)
        lower = jnp.where(rb < ra, 1.0, 0.0).astype(BF16)

        def drop_late_ties(c, before):
            k = sc_scr[c]
            eq = jnp.where(k == thr, 1.0, 0.0)
            prefix = jnp.dot(lower, eq.astype(BF16), preferred_element_type=F32) + before
            sc_scr[c] = jnp.where(k == thr, jnp.where(prefix >= need, NEG_INF, k), k)
            return before + jnp.sum(eq, axis=0, keepdims=True)

        lax.fori_loop(0, nkc, drop_late_ties, zero_row)

    thr_sel = jnp.maximum(thr, F32_LOWEST)

    acc_scr[...] = jnp.zeros_like(acc_scr)

    def attend(p, carry):
        ms, ls = carry
        kk = misc_ref[0, pl.ds(p * 2 * kc, 2 * kc), 0:64]
        vt = vt_ref[0, :, pl.ds(p * 2 * kc, 2 * kc)]
        sel = jnp.concatenate([sc_scr[2 * p], sc_scr[2 * p + 1]], axis=0) >= thr_sel
        sel_bias = jnp.where(sel, 0.0, MASK_NEG)
        ss = [lax.dot_general(kk, qs[h], _NT, preferred_element_type=F32) + sel_bias
              for h in range(B_Q_HEADS)]
        ms_new, ls_new, alphas, ps = [], [], [], []
        for h in range(B_Q_HEADS):
            m_new = jnp.maximum(ms[h], jnp.max(ss[h], axis=0, keepdims=True))
            alpha = jnp.exp2(ms[h] - m_new)
            p = jnp.exp2(ss[h] - m_new)
            ls_new.append(alpha * ls[h] + jnp.sum(p, axis=0, keepdims=True))
            ms_new.append(m_new)
            alphas.append(alpha)
            ps.append(p.astype(BF16))
        for h in range(B_Q_HEADS):
            acc_scr[h] = alphas[h] * acc_scr[h] + jnp.dot(vt, ps[h], preferred_element_type=F32)
        return tuple(ms_new), tuple(ls_new)

    init = (tuple(jnp.full((1, tq), MASK_NEG, F32) for _ in range(B_Q_HEADS)),
            tuple(jnp.zeros((1, tq), F32) for _ in range(B_Q_HEADS)))
    carry = init
    for p in range(npairs):
        carry = attend(p, carry)
    ls = carry[1]

    outs = []
    for h in range(0, B_Q_HEADS, 2):
        pair = jnp.concatenate([acc_scr[h] / ls[h], acc_scr[h + 1] / ls[h + 1]], axis=0)
        outs.append(pair.T)
    o_ref[0] = jnp.concatenate(outs, axis=-1).astype(BF16)


def _dsa_attention(bq, biq, iwt, misc, vt):
    bsz, s, _ = bq.shape
    tq = DSA_TQ
    qspec = pl.BlockSpec((1, tq, B_Q_HEADS * HEAD_DIM), lambda b, i: (b, i, 0))
    return pl.pallas_call(
        _dsa_kernel,
        grid=(bsz, s // tq),
        in_specs=[qspec, qspec,
                  pl.BlockSpec((1, B_IDX_HEADS, tq), lambda b, i: (b, 0, i)),
                  pl.BlockSpec((1, s, misc.shape[2]), lambda b, i: (b, 0, 0)),
                  pl.BlockSpec((1, HEAD_DIM, s), lambda b, i: (b, 0, 0))],
        out_specs=qspec,
        out_shape=jax.ShapeDtypeStruct(bq.shape, BF16),
        scratch_shapes=[pltpu.VMEM((s // DSA_KC, DSA_KC, tq), F32),
                        pltpu.VMEM((s // DSA_KC, DSA_KC, tq), jnp.int16),
                        pltpu.VMEM((s // DSA_KC, DSA_KC, tq), jnp.int16),
                        pltpu.VMEM((8, tq), F32),
                        pltpu.VMEM((B_Q_HEADS, HEAD_DIM, tq), F32)],
        compiler_params=_params(("parallel", "arbitrary")),
        name="dsa_attention",
    )(bq, biq, iwt, misc, vt)


POST_TM = 512
POST_ROWS = 256
POST_TF = 1024
POST_VMEM_LIMIT = 56 * 1024 * 1024


def _post_attn_kernel(*refs, widths):
    n = len(widths)
    parts = refs[:n]
    (wout_ref, x_ref, gate1_ref, g1_ref, b1_ref, shift2_ref, scale2_ref, gate2_ref,
     w1_ref, w2_ref, g2_ref, b2_ref, o_ref) = refs[n:]
    groups = [slice(r * POST_ROWS, (r + 1) * POST_ROWS) for r in range(x_ref.shape[1] // POST_ROWS)]

    def out_proj(rows):
        y = None
        off = 0
        for p_ref, wd in zip(parts, widths):
            t = jnp.dot(p_ref[0, rows, :], wout_ref[off:off + wd, :], preferred_element_type=F32)
            y = t if y is None else y + t
            off += wd
        return y

    def norm1(rows, y):
        x1 = _layer_norm(DN_ALPHA * x_ref[0, rows, :] + gate1_ref[0] * y, g1_ref[...], b1_ref[...])
        return x1, (x1 * (1.0 + scale2_ref[0]) + shift2_ref[0]).astype(BF16)

    def mlp(h):
        acc = None
        for f in range(w1_ref.shape[1] // POST_TF):
            cols = slice(f * POST_TF, (f + 1) * POST_TF)
            u = jnp.maximum(jnp.dot(h, w1_ref[:, cols], preferred_element_type=F32), 0.0)
            t = jnp.dot((u * u).astype(BF16), w2_ref[cols, :], preferred_element_type=F32)
            acc = t if acc is None else acc + t
        return acc

    ys = [out_proj(rows) for rows in groups]
    x1s, accs = [], []
    for rows, y in zip(groups, ys):
        x1, h = norm1(rows, y)
        x1s.append(x1)
        accs.append(mlp(h))
    for rows, x1, acc in zip(groups, x1s, accs):
        o_ref[0, rows, :] = _layer_norm(DN_ALPHA * x1 + gate2_ref[0] * acc, g2_ref[...], b2_ref[...])


def _post_attn(parts, w_out, x, gate1, g1, b1, shift2, scale2, gate2, w1, w2, g2, b2):
    bsz, s, d = x.shape
    tm = POST_TM
    widths = tuple(p.shape[-1] for p in parts)
    row = lambda w: pl.BlockSpec((1, tm, w), lambda bi, i: (bi, i, 0))
    vec = pl.BlockSpec((1, 1, d), lambda bi, i: (bi, 0, 0))
    full = lambda a: pl.BlockSpec(a.shape, lambda bi, i: (0,) * a.ndim, pipeline_mode=pl.Buffered(1))
    return pl.pallas_call(
        functools.partial(_post_attn_kernel, widths=widths),
        grid=(bsz, s // tm),
        in_specs=([row(w) for w in widths]
                  + [full(w_out), row(d), vec, full(g1), full(b1), vec, vec, vec,
                     full(w1), full(w2), full(g2), full(b2)]),
        out_specs=row(d),
        out_shape=jax.ShapeDtypeStruct(x.shape, F32),
        compiler_params=pltpu.CompilerParams(dimension_semantics=("parallel", "parallel"),
                                             vmem_limit_bytes=POST_VMEM_LIMIT),
        name="post_attn",
    )(*parts, w_out, x, gate1, g1, b1, shift2, scale2, gate2, w1, w2, g2, b2)


def _proj_c_kernel(x_ref, shift_ref, scale_ref, tab_ref, win_ref, q_ref, k_ref, vt_ref, km_ref):
    h = x_ref[0] * (1.0 + scale_ref[0]) + shift_ref[0]
    proj = jnp.dot(h.astype(BF16), win_ref[...], preferred_element_type=F32)
    cw = C_HEADS * HEAD_DIM
    cos, sa, sb = _rope_coeffs(tab_ref[0])
    q_ref[0] = (_rope(proj[:, 0:cw], cos, sa, sb) * ATT_SCALE).astype(BF16)
    k = _rope(proj[:, cw:2 * cw], cos, sa, sb)
    k_ref[0] = k.astype(BF16)
    vt_ref[0] = proj[:, 2 * cw:3 * cw].T.astype(BF16)
    for t in range(k.shape[0] // C_BLOCK):
        km_ref[0, t] = jnp.mean(k[t * C_BLOCK:(t + 1) * C_BLOCK], axis=0, keepdims=True)


PROJ_C_TM = 2 * C_BLOCK


def _proj_c(x, shift, scale, rope_tab, w_in):
    bsz, s, d = x.shape
    tm = PROJ_C_TM
    nbt = tm // C_BLOCK
    cw = C_HEADS * HEAD_DIM
    row = lambda w: pl.BlockSpec((1, tm, w), lambda b, i: (b, i, 0))
    vec = pl.BlockSpec((1, 1, d), lambda b, i: (b, 0, 0))
    full = lambda a: pl.BlockSpec(a.shape, lambda b, i: (0,) * a.ndim)
    qkv = jax.ShapeDtypeStruct((bsz, s, cw), BF16)
    return pl.pallas_call(
        _proj_c_kernel,
        grid=(bsz, s // tm),
        in_specs=[row(d), vec, vec, pl.BlockSpec((1, 128, tm), lambda b, i: (b, 0, i)), full(w_in)],
        out_specs=[row(cw), row(cw), pl.BlockSpec((1, cw, tm), lambda b, i: (b, 0, i)),
                   pl.BlockSpec((1, nbt, 1, cw), lambda b, i: (b, i, 0, 0))],
        out_shape=[qkv, qkv, jax.ShapeDtypeStruct((bsz, cw, s), BF16),
                   jax.ShapeDtypeStruct((bsz, s // C_BLOCK, 1, cw), F32)],
        compiler_params=_params(("parallel", "parallel")),
        name="proj_c",
    )(x, shift, scale, rope_tab, w_in)


MOBA_HEADS_PER_STEP = 16


def _moba_kernel(q_ref, k_ref, vt_ref, km_ref, o_ref, bias_scr, acc_scr):
    i = pl.program_id(2)
    tq = q_ref.shape[1]
    nb = k_ref.shape[1] // C_BLOCK
    nh = MOBA_HEADS_PER_STEP
    hsl = [slice(hh * HEAD_DIM, (hh + 1) * HEAD_DIM) for hh in range(nh)]
    qs = [q_ref[0, :, hsl[hh]] for hh in range(nh)]

    n_idx = lax.broadcasted_iota(jnp.int32, (nb, tq), 0)
    for hh in range(nh):
        gate = lax.dot_general(km_ref[0, hh].astype(BF16), qs[hh], _NT, preferred_element_type=F32)
        cnt = jnp.zeros((nb, tq), F32)
        for m_idx in range(nb - 1):
            other = gate[m_idx:m_idx + 1, :]
            tie = jnp.where(n_idx > m_idx, 1.0, 0.0)
            beats = jnp.where(other > gate, 1.0, jnp.where(other == gate, tie, 0.0))
            cnt = cnt + jnp.where(m_idx < i, beats, 0.0)
        bias_scr[hh] = jnp.where(n_idx < i, jnp.where(cnt < float(C_TOPK), 0.0, MASK_NEG), MASK_NEG)

    def block_update(blk, mask, bias, ms, ls):
        k0 = pl.multiple_of(blk * C_BLOCK, C_BLOCK)
        ss = []
        for hh in range(nh):
            kj = k_ref[0, pl.ds(k0, C_BLOCK), hsl[hh]]
            ss.append(mask(lax.dot_general(kj, qs[hh], _NT, preferred_element_type=F32)))
        ms_new, ls_new, alphas, ps = [], [], [], []
        for hh in range(nh):
            b = bias(hh)
            m_new = jnp.maximum(ms[hh], jnp.max(ss[hh], axis=0, keepdims=True) + b)
            alpha = jnp.exp2(ms[hh] - m_new)
            p = jnp.exp2(ss[hh] - (m_new - b))
            ls_new.append(alpha * ls[hh] + jnp.sum(p, axis=0, keepdims=True))
            ms_new.append(m_new)
            alphas.append(alpha)
            ps.append(p.astype(BF16))
        for hh in range(nh):
            vj = vt_ref[0, hsl[hh], pl.ds(k0, C_BLOCK)]
            acc_scr[hh] = alphas[hh] * acc_scr[hh] + jnp.dot(vj, ps[hh], preferred_element_type=F32)
        return tuple(ms_new), tuple(ls_new)

    krow = lax.broadcasted_iota(jnp.int32, (C_BLOCK, tq), 0)
    qcol = lax.broadcasted_iota(jnp.int32, (C_BLOCK, tq), 1)
    causal = krow <= qcol
    acc_scr[...] = jnp.zeros_like(acc_scr)
    ms0 = tuple(jnp.full((1, tq), MASK_NEG, F32) for _ in range(nh))
    ls0 = tuple(jnp.zeros((1, tq), F32) for _ in range(nh))
    carry = block_update(i, lambda s: jnp.where(causal, s, MASK_NEG), lambda hh: 0.0, ms0, ls0)

    def past_block(j, carry):
        return block_update(j, lambda s: s, lambda hh: bias_scr[hh, pl.ds(j, 1), :], carry[0], carry[1])

    _, ls = lax.fori_loop(0, i, past_block, carry)

    outs = []
    for hh in range(0, nh, 2):
        pair = jnp.concatenate([acc_scr[hh] / ls[hh], acc_scr[hh + 1] / ls[hh + 1]], axis=0)
        outs.append(pair.T)
    o_ref[0] = jnp.concatenate(outs, axis=-1).astype(BF16)


def _moba_attention(q, k, vt, kmean):
    bsz, s, cw = q.shape
    tq = C_BLOCK
    nh = MOBA_HEADS_PER_STEP
    w = nh * HEAD_DIM
    nb = s // C_BLOCK
    qspec = pl.BlockSpec((1, tq, w), lambda b, hg, i: (b, i, hg))
    return pl.pallas_call(
        _moba_kernel,
        grid=(bsz, cw // w, s // tq),
        in_specs=[qspec,
                  pl.BlockSpec((1, s, w), lambda b, hg, i: (b, 0, hg)),
                  pl.BlockSpec((1, w, s), lambda b, hg, i: (b, hg, 0)),
                  pl.BlockSpec((1, nh, nb, HEAD_DIM), lambda b, hg, i: (b, hg, 0, 0))],
        out_specs=qspec,
        out_shape=jax.ShapeDtypeStruct(q.shape, BF16),
        scratch_shapes=[pltpu.VMEM((nh, nb, tq), F32), pltpu.VMEM((nh, HEAD_DIM, tq), F32)],
        compiler_params=_params(("parallel", "parallel", "arbitrary")),
        name="moba_attention",
    )(q, k, vt, kmean)


def _rope_table(positions):
    inv = ROPE_THETA ** (-jnp.arange(0, HEAD_DIM, 2, dtype=F32) / HEAD_DIM)
    ang = positions.astype(F32)[:, None, :] * inv[None, :, None]
    cos, sin = lax.optimization_barrier((jnp.cos(ang), jnp.sin(ang)))
    return jnp.concatenate([cos, sin, cos, sin], axis=1)


def kernel(x, c, positions, ab_w_in, ab_q_norm, ab_w_uq, ab_w_uiq, ab_sinks, ab_w_out, c_w_in, c_w_out,
           ada_w, ada_b, ln_g, ln_b, mlp_w1, mlp_w2):
    bsz, s, d = x.shape
    rope_tab = _rope_table(positions)
    mod = _ada_modulation(c, ada_w, ada_b)

    def mods(idx):
        m = mod[idx]
        return m[:, None, 0:d], m[:, None, d:2 * d], m[:, None, 2 * d:3 * d] + 1.0

    for layer in range(DEPTH):
        shift, scale, gate = mods(2 * layer)
        shift2, scale2, gate2 = mods(2 * layer + 1)
        post = lambda parts, w_out: _post_attn(
            parts, w_out.astype(BF16), x, gate, ln_g[layer, 0][None], ln_b[layer, 0][None],
            shift2, scale2, gate2, mlp_w1[layer].astype(BF16), mlp_w2[layer].astype(BF16),
            ln_g[layer, 1][None], ln_b[layer, 1][None])
        if layer % 2 == 0:
            e = layer // 2
            w_in = jnp.pad(ab_w_in[e], ((0, 0), (0, AB_IN_PAD - AB_IN_WIDTH))).astype(BF16)
            aq, ak, avt, bq, biq, misc, bvt, iwt = _proj_ab(
                x, shift, scale, rope_tab, w_in, ab_q_norm[e][None],
                ab_w_uq[e].astype(BF16), ab_w_uiq[e].astype(BF16))
            ya = _swa_attention(ab_sinks[e], aq, ak, avt)
            yb = _dsa_attention(bq, biq, iwt, misc, bvt)
            x = post([ya, yb], ab_w_out[e])
        else:
            o = layer // 2
            q, k, vt, kmean = _proj_c(x, shift, scale, rope_tab, c_w_in[o].astype(BF16))
            km = kmean.reshape(bsz, s // C_BLOCK, C_HEADS, HEAD_DIM).transpose(0, 2, 1, 3)
            y = _moba_attention(q, k, vt, km)
            x = post([y], c_w_out[o])
    return x
```

```python
import functools

import jax
import jax.numpy as jnp
from jax import lax
from jax.experimental import pallas as pl
from jax.experimental.pallas import tpu as pltpu

HEAD_DIM = 64
ROPE_THETA = 10000.0
DEPTH = 2
A_Q_HEADS = 8
A_KV_HEADS = 2
A_BLOCK = 128
B_Q_HEADS = 8
B_IDX_HEADS = 8
B_TOPK = 256
C_HEADS = 16
C_BLOCK = 256
C_TOPK = 3
DN_ALPHA = (2 * DEPTH) ** 0.25
LN_EPS = 1e-5
RMS_EPS = 1e-6
AB_IN_WIDTH = 1224
AB_IN_PAD = 1280
LOG2E = 1.4426950408889634
ATT_SCALE = HEAD_DIM ** -0.5 * LOG2E
IDX_SCALE = B_IDX_HEADS ** -0.5 * HEAD_DIM ** -0.5

F32 = jnp.float32
BF16 = jnp.bfloat16
MASK_NEG = -1e30
NEG_INF = float("-inf")
F32_LOWEST = -3.4028234663852886e38
INT_MIN = -(2 ** 31)
HALF = 2 ** 15
VMEM_LIMIT = 48 * 1024 * 1024

_NT = (((1,), (1,)), ((), ()))


def _params(sem):
    return pltpu.CompilerParams(dimension_semantics=sem, vmem_limit_bytes=VMEM_LIMIT)


def _rope(t, cos, sin_a, sin_b):
    outs = []
    for k in range(t.shape[-1] // 128):
        blk = t[:, 128 * k:128 * (k + 1)]
        outs.append(blk * cos + pltpu.roll(blk, 32, 1) * sin_a + pltpu.roll(blk, 96, 1) * sin_b)
    return outs[0] if len(outs) == 1 else jnp.concatenate(outs, axis=-1)


def _rope_coeffs(tab_t):
    tab = tab_t.T
    low = (lax.broadcasted_iota(jnp.int32, tab.shape, 1) & (HEAD_DIM - 1)) < HEAD_DIM // 2
    swapped = pltpu.roll(tab, HEAD_DIM // 2, 1)
    cos = jnp.where(low, tab, swapped)
    sin = jnp.where(low, swapped, tab)
    return cos, jnp.where(low, 0.0, sin), jnp.where(low, -sin, 0.0)


def _layer_norm(z, g, b):
    mu = jnp.mean(z, axis=-1, keepdims=True)
    zc = z - mu
    var = jnp.mean(zc * zc, axis=-1, keepdims=True)
    return zc * lax.rsqrt(var + LN_EPS) * g + b


def _ada_kernel(c_ref, w_ref, b_ref, o_ref):
    c = c_ref[...]
    sc = c / (1.0 + jnp.exp(-c))
    o_ref[0] = jnp.dot(sc, w_ref[0], preferred_element_type=F32) + b_ref[0]


def _ada_modulation(c, ada_w, ada_b):
    n = ada_w.shape[0] * ada_w.shape[1]
    bsz, d = c.shape
    w = ada_w.reshape(n, d, 3 * d)
    b = ada_b.reshape(n, 1, 3 * d)
    tn = 1024
    return pl.pallas_call(
        _ada_kernel,
        grid=(n, 3 * d // tn),
        in_specs=[
            pl.BlockSpec((bsz, d), lambda i, j: (0, 0)),
            pl.BlockSpec((1, d, tn), lambda i, j: (i, 0, j)),
            pl.BlockSpec((1, 1, tn), lambda i, j: (i, 0, j)),
        ],
        out_specs=pl.BlockSpec((1, bsz, tn), lambda i, j: (i, 0, j)),
        out_shape=jax.ShapeDtypeStruct((n, bsz, 3 * d), F32),
        compiler_params=_params(("arbitrary", "arbitrary")),
        name="ada_modulation",
    )(c, w, b)


PROJ_AB_GROUPS = 2


def _proj_ab_kernel(x_ref, shift_ref, scale_ref, tab_ref, win_ref, qn_ref, wuq_ref, wuiq_ref,
                    aq_ref, ak_ref, avt_ref, bq_ref, biq_ref, misc_ref, bvt_ref, iwt_ref):
    tm = x_ref.shape[1]
    rows_of = [slice(g * tm // PROJ_AB_GROUPS, (g + 1) * tm // PROJ_AB_GROUPS) for g in range(PROJ_AB_GROUPS)]
    projs = []
    for rows in rows_of:
        h = x_ref[0, rows, :] * (1.0 + scale_ref[0]) + shift_ref[0]
        projs.append(jnp.dot(h.astype(BF16), win_ref[...], preferred_element_type=F32))
    lows = []
    for proj in projs:
        cq = proj[:, 768:1024]
        ms = jnp.mean(cq * cq, axis=-1, keepdims=True)
        cqn = (cq * lax.rsqrt(ms + RMS_EPS) * qn_ref[...]).astype(BF16)
        lows.append((jnp.dot(cqn, wuq_ref[...], preferred_element_type=F32),
                     jnp.dot(cqn, wuiq_ref[...], preferred_element_type=F32)))

    for rows, proj, (bq, biq) in zip(rows_of, projs, lows):
        cos, sa, sb = _rope_coeffs(tab_ref[0, :, rows])
        aq_ref[0, rows, :] = (_rope(proj[:, 0:512], cos, sa, sb) * ATT_SCALE).astype(BF16)
        ak_ref[0, rows, :] = _rope(proj[:, 512:640], cos, sa, sb).astype(BF16)
        avt_ref[0, :, rows] = proj[:, 640:768].T.astype(BF16)
        bq_ref[0, rows, :] = (_rope(bq, cos, sa, sb) * ATT_SCALE).astype(BF16)
        biq_ref[0, rows, :] = _rope(biq, cos, sa, sb).astype(BF16)

        lane = lax.broadcasted_iota(jnp.int32, cos.shape, 1)
        roped = lane < HEAD_DIM
        c0 = jnp.where(roped, cos, 1.0)
        a0 = jnp.where(roped, sa, 0.0)
        b0 = jnp.where(roped, sb, 0.0)
        m0 = _rope(proj[:, 1024:1152], c0, a0, b0)
        m1 = _rope(proj[:, 1152:1280], c0, a0, b0)
        misc_ref[0, rows, :] = jnp.concatenate([m0, m1], axis=-1).astype(BF16)
        bvt_ref[0, :, rows] = m0.T[HEAD_DIM:2 * HEAD_DIM].astype(BF16)
        iwt_ref[0, :, rows] = m1.T[HEAD_DIM:HEAD_DIM + B_IDX_HEADS] * IDX_SCALE


def _proj_ab(x, shift, scale, rope_tab, w_in, q_norm, w_uq, w_uiq, tm=512):
    bsz, s, d = x.shape
    row = lambda w: pl.BlockSpec((1, tm, w), lambda b, i: (b, i, 0))
    vec = pl.BlockSpec((1, 1, d), lambda b, i: (b, 0, 0))
    full = lambda a: pl.BlockSpec(a.shape, lambda b, i: (0,) * a.ndim)
    col = lambda r: pl.BlockSpec((1, r, tm), lambda b, i: (b, 0, i))
    tok = lambda w: jax.ShapeDtypeStruct((bsz, s, w), BF16)
    kvw = A_KV_HEADS * HEAD_DIM
    out_shape = [tok(512), tok(kvw), jax.ShapeDtypeStruct((bsz, kvw, s), BF16), tok(512), tok(512), tok(256),
                 jax.ShapeDtypeStruct((bsz, HEAD_DIM, s), BF16), jax.ShapeDtypeStruct((bsz, B_IDX_HEADS, s), F32)]
    out_specs = [row(512), row(kvw), col(kvw), row(512), row(512), row(256), col(HEAD_DIM), col(B_IDX_HEADS)]
    return pl.pallas_call(
        _proj_ab_kernel,
        grid=(bsz, s // tm),
        in_specs=[row(d), vec, vec, col(128), full(w_in), full(q_norm), full(w_uq), full(w_uiq)],
        out_specs=out_specs,
        out_shape=out_shape,
        compiler_params=_params(("parallel", "parallel")),
        name="proj_ab",
    )(x, shift, scale, rope_tab, w_in, q_norm, w_uq, w_uiq)


SWA_BLOCKS = 4
SWA_TQ = SWA_BLOCKS * A_BLOCK


def _swa_kernel(sink_ref, q_ref, kp_ref, kc_ref, vtp_ref, vtc_ref, o_ref):
    i = pl.program_id(1)
    group = A_Q_HEADS // A_KV_HEADS
    kband = jnp.concatenate([kp_ref[0], kc_ref[0]], axis=0)
    vtband = jnp.concatenate([vtp_ref[0], vtc_ref[0]], axis=1)
    c = lax.broadcasted_iota(jnp.int32, (2 * A_BLOCK, A_BLOCK), 0)
    qi = lax.broadcasted_iota(jnp.int32, (2 * A_BLOCK, A_BLOCK), 1)
    in_window = (c > qi) & (c <= qi + A_BLOCK)
    later = jnp.where(in_window, 0.0, MASK_NEG)
    masks = [jnp.where(in_window & ((c >= A_BLOCK) | (i > 0)), 0.0, MASK_NEG)] + [later] * (SWA_BLOCKS - 1)

    ss = {}
    for t in range(SWA_BLOCKS):
        for kh in range(A_KV_HEADS):
            kb = kband[t * A_BLOCK:(t + 2) * A_BLOCK, kh * HEAD_DIM:(kh + 1) * HEAD_DIM]
            qstack = jnp.concatenate(
                [q_ref[0, t * A_BLOCK:(t + 1) * A_BLOCK, hq * HEAD_DIM:(hq + 1) * HEAD_DIM]
                 for hq in range(kh * group, (kh + 1) * group)], axis=0)
            s4 = lax.dot_general(kb, qstack, _NT, preferred_element_type=F32)
            for g in range(group):
                ss[t, kh * group + g] = s4[:, g * A_BLOCK:(g + 1) * A_BLOCK] + masks[t]
    ps, dens = {}, {}
    for t in range(SWA_BLOCKS):
        for hq in range(A_Q_HEADS):
            sink = sink_ref[hq] * LOG2E
            m = jnp.maximum(jnp.max(ss[t, hq], axis=0, keepdims=True), sink)
            p = jnp.exp2(ss[t, hq] - m)
            dens[t, hq] = jnp.sum(p, axis=0, keepdims=True) + jnp.exp2(sink - m)
            ps[t, hq] = p.astype(BF16)
    for t in range(SWA_BLOCKS):
        outs = []
        for hq in range(A_Q_HEADS):
            kh = hq // group
            vt = vtband[kh * HEAD_DIM:(kh + 1) * HEAD_DIM, t * A_BLOCK:(t + 2) * A_BLOCK]
            outs.append(jnp.dot(vt, ps[t, hq], preferred_element_type=F32) / dens[t, hq])
        o_ref[0, t * A_BLOCK:(t + 1) * A_BLOCK, :] = jnp.concatenate(outs, axis=0).T.astype(BF16)


def _swa_attention(sinks, aq, ak, avt):
    bsz, s, qw = aq.shape
    kvw = A_KV_HEADS * HEAD_DIM
    prev_blk = lambda i: jnp.maximum(SWA_BLOCKS * i - 1, 0)
    qspec = pl.BlockSpec((1, SWA_TQ, qw), lambda b, i: (b, i, 0))
    return pl.pallas_call(
        _swa_kernel,
        grid=(bsz, s // SWA_TQ),
        in_specs=[pl.BlockSpec(memory_space=pltpu.SMEM), qspec,
                  pl.BlockSpec((1, A_BLOCK, kvw), lambda b, i: (b, prev_blk(i), 0)),
                  pl.BlockSpec((1, SWA_TQ, kvw), lambda b, i: (b, i, 0)),
                  pl.BlockSpec((1, kvw, A_BLOCK), lambda b, i: (b, 0, prev_blk(i))),
                  pl.BlockSpec((1, kvw, SWA_TQ), lambda b, i: (b, 0, i))],
        out_specs=qspec,
        out_shape=jax.ShapeDtypeStruct(aq.shape, BF16),
        compiler_params=_params(("parallel", "parallel")),
        name="swa_attention",
    )(sinks, aq, ak, ak, avt, avt)


DSA_TQ = 256
DSA_KC = 256


def _dsa_kernel(*refs):
    npairs = (pl.program_id(1) + 2) // 2
    for n in range(1, refs[-5].shape[0] // 2 + 1):
        pl.when(npairs == n)(functools.partial(_dsa_body, n, *refs))


def _dsa_body(npairs, q_ref, iq_ref, iwt_ref, misc_ref, vt_ref, o_ref, sc_scr, hi_scr, lo_scr, st_scr, acc_scr):
    i = pl.program_id(1)
    tq, kc = DSA_TQ, DSA_KC
    nkc = i + 1
    krow = lax.broadcasted_iota(jnp.int32, (kc, tq), 0)
    qcol = lax.broadcasted_iota(jnp.int32, (kc, tq), 1)
    on_or_below_diag = krow <= qcol
    iwt = iwt_ref[0]
    hsl = [slice(h * HEAD_DIM, (h + 1) * HEAD_DIM) for h in range(B_Q_HEADS)]
    iqs = [iq_ref[0, :, hsl[h]] for h in range(B_IDX_HEADS)]
    qs = [q_ref[0, :, hsl[h]] for h in range(B_Q_HEADS)]


    def ordered(bits):
        return jnp.where(bits >= 0, bits, bits ^ jnp.int32(0x7FFFFFFF))

    def key_to_float(key):
        return jnp.where(key == INT_MIN, NEG_INF, lax.bitcast_convert_type(ordered(key), F32))

    def score_chunk(c):
        ik = misc_ref[0, pl.ds(c * kc, kc), 128:192]
        sc = jnp.zeros((kc, tq), F32)
        for h in range(B_IDX_HEADS):
            raw = lax.dot_general(ik, iqs[h], _NT, preferred_element_type=F32)
            sc = sc + jnp.maximum(raw, 0.0) * iwt[h:h + 1, :]
        sc = jnp.where(sc == 0.0, 0.0, sc)
        key = ordered(lax.bitcast_convert_type(sc, jnp.int32))
        sc_scr[c] = jnp.where(c < i, sc, jnp.where(c == i, jnp.where(on_or_below_diag, sc, NEG_INF), NEG_INF))
        key = jnp.where(c < i, key, jnp.where(c == i, jnp.where(on_or_below_diag, key, INT_MIN), INT_MIN))
        hi_scr[c] = (key >> 16).astype(jnp.int16)
        lo_scr[c] = ((key & 0xFFFF) - HALF).astype(jnp.int16)

    for c in range(2 * npairs):
        score_chunk(c)

    def count_ge(ref, c, cand):
        ones = jnp.where(ref[c] >= cand.astype(jnp.int16), jnp.int16(1), jnp.int16(0))
        part = ones[0:16]
        for r in range(1, kc // 16):
            part = part + ones[16 * r:16 * (r + 1)]
        return jnp.sum(part.astype(jnp.int32), axis=0, keepdims=True)

    def half_search(ref, wanted):
        def step(it, thr):
            cand = thr + lax.shift_left(jnp.int32(1), 15 - it)
            cnt = count_ge(ref, 0, cand)
            for c in range(1, 2 * npairs):
                cnt = cnt + count_ge(ref, c, cand)
            return jnp.where(cnt >= wanted, cand, thr)

        return lax.fori_loop(0, 16, step, jnp.full((1, tq), -HALF, jnp.int32))

    thr_hi = half_search(hi_scr, B_TOPK)

    def mask_low(c, above):
        hi = hi_scr[c].astype(jnp.int32)
        lo_scr[c] = jnp.where(hi == thr_hi, lo_scr[c].astype(jnp.int32), -HALF).astype(jnp.int16)
        return above + jnp.sum(jnp.where(hi > thr_hi, 1, 0), axis=0, keepdims=True)

    above = lax.fori_loop(0, nkc, mask_low, jnp.zeros((1, tq), jnp.int32))
    thr_lo = half_search(lo_scr, B_TOPK - above)
    thr_guess = key_to_float((thr_hi << 16) | (thr_lo + HALF))

    zero_row = jnp.zeros((1, tq), F32)

    def count_gt_eq(t):
        def body(c, carry):
            ngt, neq = carry
            k = sc_scr[c]
            return (ngt + jnp.sum(jnp.where(k > t, 1.0, 0.0), axis=0, keepdims=True),
                    neq + jnp.sum(jnp.where(k == t, 1.0, 0.0), axis=0, keepdims=True))

        return lax.fori_loop(0, nkc, body, (zero_row, zero_row))

    def keep(t, ngt, neq):
        st_scr[0:1, :] = t
        st_scr[1:2, :] = ngt
        st_scr[2:3, :] = neq

    ngt, neq = count_gt_eq(thr_guess)
    keep(thr_guess, ngt, neq)
    is_kth = jnp.where(ngt < float(B_TOPK), jnp.where(ngt + neq >= float(B_TOPK), 1.0, 0.0), 0.0)

    @pl.when(jnp.min(is_kth) < 1.0)
    def _():
        def step(it, key):
            cand = key + lax.shift_left(jnp.int32(1), 31 - it)
            cand_f = lax.bitcast_convert_type(ordered(cand), F32)
            cnt = lax.fori_loop(
                0, nkc, lambda c, acc: acc + jnp.sum(jnp.where(sc_scr[c] >= cand_f, 1.0, 0.0), axis=0,
                                                      keepdims=True), zero_row)
            return jnp.where(cnt >= float(B_TOPK), cand, key)

        t = key_to_float(lax.fori_loop(0, 32, step, jnp.full((1, tq), INT_MIN, jnp.int32)))
        keep(t, *count_gt_eq(t))

    thr = st_scr[0:1, :]
    need = float(B_TOPK) - st_scr[1:2, :]

    @pl.when(jnp.max(st_scr[2:3, :] - need) > 0.0)
    def _():
        ra = lax.broadcasted_iota(jnp.int32, (kc, kc), 0)
        rb = lax.broadcasted_iota(jnp.int32, (kc, kc), 1)
        lower = jnp.where(rb < ra, 1.0, 0.0).astype(BF16)

        def drop_late_ties(c, before):
            k = sc_scr[c]
            eq = jnp.where(k == thr, 1.0, 0.0)
            prefix = jnp.dot(lower, eq.astype(BF16), preferred_element_type=F32) + before
            sc_scr[c] = jnp.where(k == thr, jnp.where(prefix >= need, NEG_INF, k), k)
            return before + jnp.sum(eq, axis=0, keepdims=True)

        lax.fori_loop(0, nkc, drop_late_ties, zero_row)

    thr_sel = jnp.maximum(thr, F32_LOWEST)

    acc_scr[...] = jnp.zeros_like(acc_scr)

    def attend(p, carry):
        ms, ls = carry
        kk = misc_ref[0, pl.ds(p * 2 * kc, 2 * kc), 0:64]
        vt = vt_ref[0, :, pl.ds(p * 2 * kc, 2 * kc)]
        sel = jnp.concatenate([sc_scr[2 * p], sc_scr[2 * p + 1]], axis=0) >= thr_sel
        sel_bias = jnp.where(sel, 0.0, MASK_NEG)
        ss = [lax.dot_general(kk, qs[h], _NT, preferred_element_type=F32) + sel_bias
              for h in range(B_Q_HEADS)]
        ms_new, ls_new, alphas, ps = [], [], [], []
        for h in range(B_Q_HEADS):
            m_new = jnp.maximum(ms[h], jnp.max(ss[h], axis=0, keepdims=True))
            alpha = jnp.exp2(ms[h] - m_new)
            p = jnp.exp2(ss[h] - m_new)
            ls_new.append(alpha * ls[h] + jnp.sum(p, axis=0, keepdims=True))
            ms_new.append(m_new)
            alphas.append(alpha)
            ps.append(p.astype(BF16))
        for h in range(B_Q_HEADS):
            acc_scr[h] = alphas[h] * acc_scr[h] + jnp.dot(vt, ps[h], preferred_element_type=F32)
        return tuple(ms_new), tuple(ls_new)

    init = (tuple(jnp.full((1, tq), MASK_NEG, F32) for _ in range(B_Q_HEADS)),
            tuple(jnp.zeros((1, tq), F32) for _ in range(B_Q_HEADS)))
    carry = init
    for p in range(npairs):
        carry = attend(p, carry)
    ls = carry[1]

    outs = []
    for h in range(0, B_Q_HEADS, 2):
        pair = jnp.concatenate([acc_scr[h] / ls[h], acc_scr[h + 1] / ls[h + 1]], axis=0)
        outs.append(pair.T)
    o_ref[0] = jnp.concatenate(outs, axis=-1).astype(BF16)


def _dsa_attention(bq, biq, iwt, misc, vt):
    bsz, s, _ = bq.shape
    tq = DSA_TQ
    qspec = pl.BlockSpec((1, tq, B_Q_HEADS * HEAD_DIM), lambda b, i: (b, i, 0))
    return pl.pallas_call(
        _dsa_kernel,
        grid=(bsz, s // tq),
        in_specs=[qspec, qspec,
                  pl.BlockSpec((1, B_IDX_HEADS, tq), lambda b, i: (b, 0, i)),
                  pl.BlockSpec((1, s, misc.shape[2]), lambda b, i: (b, 0, 0)),
                  pl.BlockSpec((1, HEAD_DIM, s), lambda b, i: (b, 0, 0))],
        out_specs=qspec,
        out_shape=jax.ShapeDtypeStruct(bq.shape, BF16),
        scratch_shapes=[pltpu.VMEM((s // DSA_KC, DSA_KC, tq), F32),
                        pltpu.VMEM((s // DSA_KC, DSA_KC, tq), jnp.int16),
                        pltpu.VMEM((s // DSA_KC, DSA_KC, tq), jnp.int16),
                        pltpu.VMEM((8, tq), F32),
                        pltpu.VMEM((B_Q_HEADS, HEAD_DIM, tq), F32)],
        compiler_params=_params(("parallel", "arbitrary")),
        name="dsa_attention",
    )(bq, biq, iwt, misc, vt)


POST_TM = 512
POST_ROWS = 256
POST_TF = 1024
POST_VMEM_LIMIT = 56 * 1024 * 1024


def _post_attn_kernel(*refs, widths):
    n = len(widths)
    parts = refs[:n]
    (wout_ref, x_ref, gate1_ref, g1_ref, b1_ref, shift2_ref, scale2_ref, gate2_ref,
     w1_ref, w2_ref, g2_ref, b2_ref, o_ref) = refs[n:]
    groups = [slice(r * POST_ROWS, (r + 1) * POST_ROWS) for r in range(x_ref.shape[1] // POST_ROWS)]

    def out_proj(rows):
        y = None
        off = 0
        for p_ref, wd in zip(parts, widths):
            t = jnp.dot(p_ref[0, rows, :], wout_ref[off:off + wd, :], preferred_element_type=F32)
            y = t if y is None else y + t
            off += wd
        return y

    def norm1(rows, y):
        x1 = _layer_norm(DN_ALPHA * x_ref[0, rows, :] + gate1_ref[0] * y, g1_ref[...], b1_ref[...])
        return x1, (x1 * (1.0 + scale2_ref[0]) + shift2_ref[0]).astype(BF16)

    def mlp(h):
        acc = None
        for f in range(w1_ref.shape[1] // POST_TF):
            cols = slice(f * POST_TF, (f + 1) * POST_TF)
            u = jnp.maximum(jnp.dot(h, w1_ref[:, cols], preferred_element_type=F32), 0.0)
            t = jnp.dot((u * u).astype(BF16), w2_ref[cols, :], preferred_element_type=F32)
            acc = t if acc is None else acc + t
        return acc

    ys = [out_proj(rows) for rows in groups]
    x1s, accs = [], []
    for rows, y in zip(groups, ys):
        x1, h = norm1(rows, y)
        x1s.append(x1)
        accs.append(mlp(h))
    for rows, x1, acc in zip(groups, x1s, accs):
        o_ref[0, rows, :] = _layer_norm(DN_ALPHA * x1 + gate2_ref[0] * acc, g2_ref[...], b2_ref[...])


def _post_attn(parts, w_out, x, gate1, g1, b1, shift2, scale2, gate2, w1, w2, g2, b2):
    bsz, s, d = x.shape
    tm = POST_TM
    widths = tuple(p.shape[-1] for p in parts)
    row = lambda w: pl.BlockSpec((1, tm, w), lambda bi, i: (bi, i, 0))
    vec = pl.BlockSpec((1, 1, d), lambda bi, i: (bi, 0, 0))
    full = lambda a: pl.BlockSpec(a.shape, lambda bi, i: (0,) * a.ndim, pipeline_mode=pl.Buffered(1))
    return pl.pallas_call(
        functools.partial(_post_attn_kernel, widths=widths),
        grid=(bsz, s // tm),
        in_specs=([row(w) for w in widths]
                  + [full(w_out), row(d), vec, full(g1), full(b1), vec, vec, vec,
                     full(w1), full(w2), full(g2), full(b2)]),
        out_specs=row(d),
        out_shape=jax.ShapeDtypeStruct(x.shape, F32),
        compiler_params=pltpu.CompilerParams(dimension_semantics=("parallel", "parallel"),
                                             vmem_limit_bytes=POST_VMEM_LIMIT),
        name="post_attn",
    )(*parts, w_out, x, gate1, g1, b1, shift2, scale2, gate2, w1, w2, g2, b2)


def _proj_c_kernel(x_ref, shift_ref, scale_ref, tab_ref, win_ref, q_ref, k_ref, vt_ref, km_ref):
    h = x_ref[0] * (1.0 + scale_ref[0]) + shift_ref[0]
    proj = jnp.dot(h.astype(BF16), win_ref[...], preferred_element_type=F32)
    cw = C_HEADS * HEAD_DIM
    cos, sa, sb = _rope_coeffs(tab_ref[0])
    q_ref[0] = (_rope(proj[:, 0:cw], cos, sa, sb) * ATT_SCALE).astype(BF16)
    k = _rope(proj[:, cw:2 * cw], cos, sa, sb)
    k_ref[0] = k.astype(BF16)
    vt_ref[0] = proj[:, 2 * cw:3 * cw].T.astype(BF16)
    for t in range(k.shape[0] // C_BLOCK):
        km_ref[0, t] = jnp.mean(k[t * C_BLOCK:(t + 1) * C_BLOCK], axis=0, keepdims=True)


PROJ_C_TM = 2 * C_BLOCK


def _proj_c(x, shift, scale, rope_tab, w_in):
    bsz, s, d = x.shape
    tm = PROJ_C_TM
    nbt = tm // C_BLOCK
    cw = C_HEADS * HEAD_DIM
    row = lambda w: pl.BlockSpec((1, tm, w), lambda b, i: (b, i, 0))
    vec = pl.BlockSpec((1, 1, d), lambda b, i: (b, 0, 0))
    full = lambda a: pl.BlockSpec(a.shape, lambda b, i: (0,) * a.ndim)
    qkv = jax.ShapeDtypeStruct((bsz, s, cw), BF16)
    return pl.pallas_call(
        _proj_c_kernel,
        grid=(bsz, s // tm),
        in_specs=[row(d), vec, vec, pl.BlockSpec((1, 128, tm), lambda b, i: (b, 0, i)), full(w_in)],
        out_specs=[row(cw), row(cw), pl.BlockSpec((1, cw, tm), lambda b, i: (b, 0, i)),
                   pl.BlockSpec((1, nbt, 1, cw), lambda b, i: (b, i, 0, 0))],
        out_shape=[qkv, qkv, jax.ShapeDtypeStruct((bsz, cw, s), BF16),
                   jax.ShapeDtypeStruct((bsz, s // C_BLOCK, 1, cw), F32)],
        compiler_params=_params(("parallel", "parallel")),
        name="proj_c",
    )(x, shift, scale, rope_tab, w_in)


MOBA_HEADS_PER_STEP = 16


def _moba_kernel(q_ref, k_ref, vt_ref, km_ref, o_ref, bias_scr, acc_scr):
    i = pl.program_id(2)
    tq = q_ref.shape[1]
    nb = k_ref.shape[1] // C_BLOCK
    nh = MOBA_HEADS_PER_STEP
    hsl = [slice(hh * HEAD_DIM, (hh + 1) * HEAD_DIM) for hh in range(nh)]
    qs = [q_ref[0, :, hsl[hh]] for hh in range(nh)]

    n_idx = lax.broadcasted_iota(jnp.int32, (nb, tq), 0)
    for hh in range(nh):
        gate = lax.dot_general(km_ref[0, hh].astype(BF16), qs[hh], _NT, preferred_element_type=F32)
        cnt = jnp.zeros((nb, tq), F32)
        for m_idx in range(nb - 1):
            other = gate[m_idx:m_idx + 1, :]
            tie = jnp.where(n_idx > m_idx, 1.0, 0.0)
            beats = jnp.where(other > gate, 1.0, jnp.where(other == gate, tie, 0.0))
            cnt = cnt + jnp.where(m_idx < i, beats, 0.0)
        bias_scr[hh] = jnp.where(n_idx < i, jnp.where(cnt < float(C_TOPK), 0.0, MASK_NEG), MASK_NEG)

    def block_update(blk, mask, bias, ms, ls):
        k0 = pl.multiple_of(blk * C_BLOCK, C_BLOCK)
        ss = []
        for hh in range(nh):
            kj = k_ref[0, pl.ds(k0, C_BLOCK), hsl[hh]]
            ss.append(mask(lax.dot_general(kj, qs[hh], _NT, preferred_element_type=F32)))
        ms_new, ls_new, alphas, ps = [], [], [], []
        for hh in range(nh):
            b = bias(hh)
            m_new = jnp.maximum(ms[hh], jnp.max(ss[hh], axis=0, keepdims=True) + b)
            alpha = jnp.exp2(ms[hh] - m_new)
            p = jnp.exp2(ss[hh] - (m_new - b))
            ls_new.append(alpha * ls[hh] + jnp.sum(p, axis=0, keepdims=True))
            ms_new.append(m_new)
            alphas.append(alpha)
            ps.append(p.astype(BF16))
        for hh in range(nh):
            vj = vt_ref[0, hsl[hh], pl.ds(k0, C_BLOCK)]
            acc_scr[hh] = alphas[hh] * acc_scr[hh] + jnp.dot(vj, ps[hh], preferred_element_type=F32)
        return tuple(ms_new), tuple(ls_new)

    krow = lax.broadcasted_iota(jnp.int32, (C_BLOCK, tq), 0)
    qcol = lax.broadcasted_iota(jnp.int32, (C_BLOCK, tq), 1)
    causal = krow <= qcol
    acc_scr[...] = jnp.zeros_like(acc_scr)
    ms0 = tuple(jnp.full((1, tq), MASK_NEG, F32) for _ in range(nh))
    ls0 = tuple(jnp.zeros((1, tq), F32) for _ in range(nh))
    carry = block_update(i, lambda s: jnp.where(causal, s, MASK_NEG), lambda hh: 0.0, ms0, ls0)

    def past_block(j, carry):
        return block_update(j, lambda s: s, lambda hh: bias_scr[hh, pl.ds(j, 1), :], carry[0], carry[1])

    _, ls = lax.fori_loop(0, i, past_block, carry)

    outs = []
    for hh in range(0, nh, 2):
        pair = jnp.concatenate([acc_scr[hh] / ls[hh], acc_scr[hh + 1] / ls[hh + 1]], axis=0)
        outs.append(pair.T)
    o_ref[0] = jnp.concatenate(outs, axis=-1).astype(BF16)


def _moba_attention(q, k, vt, kmean):
    bsz, s, cw = q.shape
    tq = C_BLOCK
    nh = MOBA_HEADS_PER_STEP
    w = nh * HEAD_DIM
    nb = s // C_BLOCK
    qspec = pl.BlockSpec((1, tq, w), lambda b, hg, i: (b, i, hg))
    return pl.pallas_call(
        _moba_kernel,
        grid=(bsz, cw // w, s // tq),
        in_specs=[qspec,
                  pl.BlockSpec((1, s, w), lambda b, hg, i: (b, 0, hg)),
                  pl.BlockSpec((1, w, s), lambda b, hg, i: (b, hg, 0)),
                  pl.BlockSpec((1, nh, nb, HEAD_DIM), lambda b, hg, i: (b, hg, 0, 0))],
        out_specs=qspec,
        out_shape=jax.ShapeDtypeStruct(q.shape, BF16),
        scratch_shapes=[pltpu.VMEM((nh, nb, tq), F32), pltpu.VMEM((nh, HEAD_DIM, tq), F32)],
        compiler_params=_params(("parallel", "parallel", "arbitrary")),
        name="moba_attention",
    )(q, k, vt, kmean)


def _rope_table(positions):
    inv = ROPE_THETA ** (-jnp.arange(0, HEAD_DIM, 2, dtype=F32) / HEAD_DIM)
    ang = positions.astype(F32)[:, None, :] * inv[None, :, None]
    cos, sin = lax.optimization_barrier((jnp.cos(ang), jnp.sin(ang)))
    return jnp.concatenate([cos, sin, cos, sin], axis=1)


def kernel(x, c, positions, ab_w_in, ab_q_norm, ab_w_uq, ab_w_uiq, ab_sinks, ab_w_out, c_w_in, c_w_out,
           ada_w, ada_b, ln_g, ln_b, mlp_w1, mlp_w2):
    bsz, s, d = x.shape
    rope_tab = _rope_table(positions)
    mod = _ada_modulation(c, ada_w, ada_b)

    def mods(idx):
        m = mod[idx]
        return m[:, None, 0:d], m[:, None, d:2 * d], m[:, None, 2 * d:3 * d] + 1.0

    for layer in range(DEPTH):
        shift, scale, gate = mods(2 * layer)
        shift2, scale2, gate2 = mods(2 * layer + 1)
        post = lambda parts, w_out: _post_attn(
            parts, w_out.astype(BF16), x, gate, ln_g[layer, 0][None], ln_b[layer, 0][None],
            shift2, scale2, gate2, mlp_w1[layer].astype(BF16), mlp_w2[layer].astype(BF16),
            ln_g[layer, 1][None], ln_b[layer, 1][None])
        if layer % 2 == 0:
            e = layer // 2
            w_in = jnp.pad(ab_w_in[e], ((0, 0), (0, AB_IN_PAD - AB_IN_WIDTH))).astype(BF16)
            aq, ak, avt, bq, biq, misc, bvt, iwt = _proj_ab(
                x, shift, scale, rope_tab, w_in, ab_q_norm[e][None],
                ab_w_uq[e].astype(BF16), ab_w_uiq[e].astype(BF16))
            ya = _swa_attention(ab_sinks[e], aq, ak, avt)
            yb = _dsa_attention(bq, biq, iwt, misc, bvt)
            x = post([ya, yb], ab_w_out[e])
        else:
            o = layer // 2
            q, k, vt, kmean = _proj_c(x, shift, scale, rope_tab, c_w_in[o].astype(BF16))
            km = kmean.reshape(bsz, s // C_BLOCK, C_HEADS, HEAD_DIM).transpose(0, 2, 1, 3)
            y = _moba_attention(q, k, vt, km)
            x = post([y], c_w_out[o])
    return x
```

```python
import functools

import jax
import jax.numpy as jnp
from jax import lax
from jax.experimental import pallas as pl
from jax.experimental.pallas import tpu as pltpu

HEAD_DIM = 64
ROPE_THETA = 10000.0
DEPTH = 2
A_Q_HEADS = 8
A_KV_HEADS = 2
A_BLOCK = 128
B_Q_HEADS = 8
B_IDX_HEADS = 8
B_TOPK = 256
C_HEADS = 16
C_BLOCK = 256
C_TOPK = 3
DN_ALPHA = (2 * DEPTH) ** 0.25
LN_EPS = 1e-5
RMS_EPS = 1e-6
AB_IN_WIDTH = 1224
AB_IN_PAD = 1280
LOG2E = 1.4426950408889634
ATT_SCALE = HEAD_DIM ** -0.5 * LOG2E
IDX_SCALE = B_IDX_HEADS ** -0.5 * HEAD_DIM ** -0.5

F32 = jnp.float32
BF16 = jnp.bfloat16
MASK_NEG = -1e30
NEG_INF = float("-inf")
F32_LOWEST = -3.4028234663852886e38
INT_MIN = -(2 ** 31)
HALF = 2 ** 15
VMEM_LIMIT = 48 * 1024 * 1024

_NT = (((1,), (1,)), ((), ()))


def _params(sem):
    return pltpu.CompilerParams(dimension_semantics=sem, vmem_limit_bytes=VMEM_LIMIT)


def _rope(t, cos, sin_a, sin_b):
    outs = []
    for k in range(t.shape[-1] // 128):
        blk = t[:, 128 * k:128 * (k + 1)]
        outs.append(blk * cos + pltpu.roll(blk, 32, 1) * sin_a + pltpu.roll(blk, 96, 1) * sin_b)
    return outs[0] if len(outs) == 1 else jnp.concatenate(outs, axis=-1)


def _rope_coeffs(tab_t):
    tab = tab_t.T
    low = (lax.broadcasted_iota(jnp.int32, tab.shape, 1) & (HEAD_DIM - 1)) < HEAD_DIM // 2
    swapped = pltpu.roll(tab, HEAD_DIM // 2, 1)
    cos = jnp.where(low, tab, swapped)
    sin = jnp.where(low, swapped, tab)
    return cos, jnp.where(low, 0.0, sin), jnp.where(low, -sin, 0.0)


def _layer_norm(z, g, b):
    mu = jnp.mean(z, axis=-1, keepdims=True)
    zc = z - mu
    var = jnp.mean(zc * zc, axis=-1, keepdims=True)
    return zc * lax.rsqrt(var + LN_EPS) * g + b


def _ada_kernel(c_ref, w_ref, b_ref, o_ref):
    c = c_ref[...]
    sc = c / (1.0 + jnp.exp(-c))
    o_ref[0] = jnp.dot(sc, w_ref[0], preferred_element_type=F32) + b_ref[0]


def _ada_modulation(c, ada_w, ada_b):
    n = ada_w.shape[0] * ada_w.shape[1]
    bsz, d = c.shape
    w = ada_w.reshape(n, d, 3 * d)
    b = ada_b.reshape(n, 1, 3 * d)
    tn = 1024
    return pl.pallas_call(
        _ada_kernel,
        grid=(n, 3 * d // tn),
        in_specs=[
            pl.BlockSpec((bsz, d), lambda i, j: (0, 0)),
            pl.BlockSpec((1, d, tn), lambda i, j: (i, 0, j)),
            pl.BlockSpec((1, 1, tn), lambda i, j: (i, 0, j)),
        ],
        out_specs=pl.BlockSpec((1, bsz, tn), lambda i, j: (i, 0, j)),
        out_shape=jax.ShapeDtypeStruct((n, bsz, 3 * d), F32),
        compiler_params=_params(("arbitrary", "arbitrary")),
        name="ada_modulation",
    )(c, w, b)


PROJ_AB_GROUPS = 2


def _proj_ab_kernel(x_ref, shift_ref, scale_ref, tab_ref, win_ref, qn_ref, wuq_ref, wuiq_ref,
                    aq_ref, ak_ref, avt_ref, bq_ref, biq_ref, misc_ref, bvt_ref, iwt_ref):
    tm = x_ref.shape[1]
    rows_of = [slice(g * tm // PROJ_AB_GROUPS, (g + 1) * tm // PROJ_AB_GROUPS) for g in range(PROJ_AB_GROUPS)]
    projs = []
    for rows in rows_of:
        h = x_ref[0, rows, :] * (1.0 + scale_ref[0]) + shift_ref[0]
        projs.append(jnp.dot(h.astype(BF16), win_ref[...], preferred_element_type=F32))
    lows = []
    for proj in projs:
        cq = proj[:, 768:1024]
        ms = jnp.mean(cq * cq, axis=-1, keepdims=True)
        cqn = (cq * lax.rsqrt(ms + RMS_EPS) * qn_ref[...]).astype(BF16)
        lows.append((jnp.dot(cqn, wuq_ref[...], preferred_element_type=F32),
                     jnp.dot(cqn, wuiq_ref[...], preferred_element_type=F32)))

    for rows, proj, (bq, biq) in zip(rows_of, projs, lows):
        cos, sa, sb = _rope_coeffs(tab_ref[0, :, rows])
        aq_ref[0, rows, :] = (_rope(proj[:, 0:512], cos, sa, sb) * ATT_SCALE).astype(BF16)
        ak_ref[0, rows, :] = _rope(proj[:, 512:640], cos, sa, sb).astype(BF16)
        avt_ref[0, :, rows] = proj[:, 640:768].T.astype(BF16)
        bq_ref[0, rows, :] = (_rope(bq, cos, sa, sb) * ATT_SCALE).astype(BF16)
        biq_ref[0, rows, :] = _rope(biq, cos, sa, sb).astype(BF16)

        lane = lax.broadcasted_iota(jnp.int32, cos.shape, 1)
        roped = lane < HEAD_DIM
        c0 = jnp.where(roped, cos, 1.0)
        a0 = jnp.where(roped, sa, 0.0)
        b0 = jnp.where(roped, sb, 0.0)
        m0 = _rope(proj[:, 1024:1152], c0, a0, b0)
        m1 = _rope(proj[:, 1152:1280], c0, a0, b0)
        misc_ref[0, rows, :] = jnp.concatenate([m0, m1], axis=-1).astype(BF16)
        bvt_ref[0, :, rows] = m0.T[HEAD_DIM:2 * HEAD_DIM].astype(BF16)
        iwt_ref[0, :, rows] = m1.T[HEAD_DIM:HEAD_DIM + B_IDX_HEADS] * IDX_SCALE


def _proj_ab(x, shift, scale, rope_tab, w_in, q_norm, w_uq, w_uiq, tm=512):
    bsz, s, d = x.shape
    row = lambda w: pl.BlockSpec((1, tm, w), lambda b, i: (b, i, 0))
    vec = pl.BlockSpec((1, 1, d), lambda b, i: (b, 0, 0))
    full = lambda a: pl.BlockSpec(a.shape, lambda b, i: (0,) * a.ndim)
    col = lambda r: pl.BlockSpec((1, r, tm), lambda b, i: (b, 0, i))
    tok = lambda w: jax.ShapeDtypeStruct((bsz, s, w), BF16)
    kvw = A_KV_HEADS * HEAD_DIM
    out_shape = [tok(512), tok(kvw), jax.ShapeDtypeStruct((bsz, kvw, s), BF16), tok(512), tok(512), tok(256),
                 jax.ShapeDtypeStruct((bsz, HEAD_DIM, s), BF16), jax.ShapeDtypeStruct((bsz, B_IDX_HEADS, s), F32)]
    out_specs = [row(512), row(kvw), col(kvw), row(512), row(512), row(256), col(HEAD_DIM), col(B_IDX_HEADS)]
    return pl.pallas_call(
        _proj_ab_kernel,
        grid=(bsz, s // tm),
        in_specs=[row(d), vec, vec, col(128), full(w_in), full(q_norm), full(w_uq), full(w_uiq)],
        out_specs=out_specs,
        out_shape=out_shape,
        compiler_params=_params(("parallel", "parallel")),
        name="proj_ab",
    )(x, shift, scale, rope_tab, w_in, q_norm, w_uq, w_uiq)


SWA_BLOCKS = 4
SWA_TQ = SWA_BLOCKS * A_BLOCK


def _swa_kernel(sink_ref, q_ref, kp_ref, kc_ref, vtp_ref, vtc_ref, o_ref):
    i = pl.program_id(1)
    group = A_Q_HEADS // A_KV_HEADS
    kband = jnp.concatenate([kp_ref[0], kc_ref[0]], axis=0)
    vtband = jnp.concatenate([vtp_ref[0], vtc_ref[0]], axis=1)
    c = lax.broadcasted_iota(jnp.int32, (2 * A_BLOCK, A_BLOCK), 0)
    qi = lax.broadcasted_iota(jnp.int32, (2 * A_BLOCK, A_BLOCK), 1)
    in_window = (c > qi) & (c <= qi + A_BLOCK)
    later = jnp.where(in_window, 0.0, MASK_NEG)
    masks = [jnp.where(in_window & ((c >= A_BLOCK) | (i > 0)), 0.0, MASK_NEG)] + [later] * (SWA_BLOCKS - 1)

    ss = {}
    for t in range(SWA_BLOCKS):
        for kh in range(A_KV_HEADS):
            kb = kband[t * A_BLOCK:(t + 2) * A_BLOCK, kh * HEAD_DIM:(kh + 1) * HEAD_DIM]
            qstack = jnp.concatenate(
                [q_ref[0, t * A_BLOCK:(t + 1) * A_BLOCK, hq * HEAD_DIM:(hq + 1) * HEAD_DIM]
                 for hq in range(kh * group, (kh + 1) * group)], axis=0)
            s4 = lax.dot_general(kb, qstack, _NT, preferred_element_type=F32)
            for g in range(group):
                ss[t, kh * group + g] = s4[:, g * A_BLOCK:(g + 1) * A_BLOCK] + masks[t]
    ps, dens = {}, {}
    for t in range(SWA_BLOCKS):
        for hq in range(A_Q_HEADS):
            sink = sink_ref[hq] * LOG2E
            m = jnp.maximum(jnp.max(ss[t, hq], axis=0, keepdims=True), sink)
            p = jnp.exp2(ss[t, hq] - m)
            dens[t, hq] = jnp.sum(p, axis=0, keepdims=True) + jnp.exp2(sink - m)
            ps[t, hq] = p.astype(BF16)
    for t in range(SWA_BLOCKS):
        outs = []
        for hq in range(A_Q_HEADS):
            kh = hq // group
            vt = vtband[kh * HEAD_DIM:(kh + 1) * HEAD_DIM, t * A_BLOCK:(t + 2) * A_BLOCK]
            outs.append(jnp.dot(vt, ps[t, hq], preferred_element_type=F32) / dens[t, hq])
        o_ref[0, t * A_BLOCK:(t + 1) * A_BLOCK, :] = jnp.concatenate(outs, axis=0).T.astype(BF16)


def _swa_attention(sinks, aq, ak, avt):
    bsz, s, qw = aq.shape
    kvw = A_KV_HEADS * HEAD_DIM
    prev_blk = lambda i: jnp.maximum(SWA_BLOCKS * i - 1, 0)
    qspec = pl.BlockSpec((1, SWA_TQ, qw), lambda b, i: (b, i, 0))
    return pl.pallas_call(
        _swa_kernel,
        grid=(bsz, s // SWA_TQ),
        in_specs=[pl.BlockSpec(memory_space=pltpu.SMEM), qspec,
                  pl.BlockSpec((1, A_BLOCK, kvw), lambda b, i: (b, prev_blk(i), 0)),
                  pl.BlockSpec((1, SWA_TQ, kvw), lambda b, i: (b, i, 0)),
                  pl.BlockSpec((1, kvw, A_BLOCK), lambda b, i: (b, 0, prev_blk(i))),
                  pl.BlockSpec((1, kvw, SWA_TQ), lambda b, i: (b, 0, i))],
        out_specs=qspec,
        out_shape=jax.ShapeDtypeStruct(aq.shape, BF16),
        compiler_params=_params(("parallel", "parallel")),
        name="swa_attention",
    )(sinks, aq, ak, ak, avt, avt)


DSA_TQ = 256
DSA_KC = 256


def _dsa_kernel(*refs):
    npairs = (pl.program_id(1) + 2) // 2
    for n in range(1, refs[-5].shape[0] // 2 + 1):
        pl.when(npairs == n)(functools.partial(_dsa_body, n, *refs))


def _dsa_body(npairs, q_ref, iq_ref, iwt_ref, misc_ref, vt_ref, o_ref, sc_scr, hi_scr, lo_scr, st_scr, acc_scr):
    i = pl.program_id(1)
    tq, kc = DSA_TQ, DSA_KC
    nkc = i + 1
    krow = lax.broadcasted_iota(jnp.int32, (kc, tq), 0)
    qcol = lax.broadcasted_iota(jnp.int32, (kc, tq), 1)
    on_or_below_diag = krow <= qcol
    iwt = iwt_ref[0]
    hsl = [slice(h * HEAD_DIM, (h + 1) * HEAD_DIM) for h in range(B_Q_HEADS)]
    iqs = [iq_ref[0, :, hsl[h]] for h in range(B_IDX_HEADS)]
    qs = [q_ref[0, :, hsl[h]] for h in range(B_Q_HEADS)]


    def ordered(bits):
        return jnp.where(bits >= 0, bits, bits ^ jnp.int32(0x7FFFFFFF))

    def key_to_float(key):
        return jnp.where(key == INT_MIN, NEG_INF, lax.bitcast_convert_type(ordered(key), F32))

    def score_chunk(c):
        ik = misc_ref[0, pl.ds(c * kc, kc), 128:192]
        sc = jnp.zeros((kc, tq), F32)
        for h in range(B_IDX_HEADS):
            raw = lax.dot_general(ik, iqs[h], _NT, preferred_element_type=F32)
            sc = sc + jnp.maximum(raw, 0.0) * iwt[h:h + 1, :]
        sc = jnp.where(sc == 0.0, 0.0, sc)
        key = ordered(lax.bitcast_convert_type(sc, jnp.int32))
        sc_scr[c] = jnp.where(c < i, sc, jnp.where(c == i, jnp.where(on_or_below_diag, sc, NEG_INF), NEG_INF))
        key = jnp.where(c < i, key, jnp.where(c == i, jnp.where(on_or_below_diag, key, INT_MIN), INT_MIN))
        hi_scr[c] = (key >> 16).astype(jnp.int16)
        lo_scr[c] = ((key & 0xFFFF) - HALF).astype(jnp.int16)

    for c in range(2 * npairs):
        score_chunk(c)

    def count_ge(ref, c, cand):
        ones = jnp.where(ref[c] >= cand.astype(jnp.int16), jnp.int16(1), jnp.int16(0))
        part = ones[0:16]
        for r in range(1, kc // 16):
            part = part + ones[16 * r:16 * (r + 1)]
        return jnp.sum(part.astype(jnp.int32), axis=0, keepdims=True)

    def half_search(ref, wanted):
        def step(it, thr):
            cand = thr + lax.shift_left(jnp.int32(1), 15 - it)
            cnt = count_ge(ref, 0, cand)
            for c in range(1, 2 * npairs):
                cnt = cnt + count_ge(ref, c, cand)
            return jnp.where(cnt >= wanted, cand, thr)

        return lax.fori_loop(0, 16, step, jnp.full((1, tq), -HALF, jnp.int32))

    thr_hi = half_search(hi_scr, B_TOPK)

    def mask_low(c, above):
        hi = hi_scr[c].astype(jnp.int32)
        lo_scr[c] = jnp.where(hi == thr_hi, lo_scr[c].astype(jnp.int32), -HALF).astype(jnp.int16)
        return above + jnp.sum(jnp.where(hi > thr_hi, 1, 0), axis=0, keepdims=True)

    above = lax.fori_loop(0, nkc, mask_low, jnp.zeros((1, tq), jnp.int32))
    thr_lo = half_search(lo_scr, B_TOPK - above)
    thr_guess = key_to_float((thr_hi << 16) | (thr_lo + HALF))

    zero_row = jnp.zeros((1, tq), F32)

    def count_gt_eq(t):
        def body(c, carry):
            ngt, neq = carry
            k = sc_scr[c]
            return (ngt + jnp.sum(jnp.where(k > t, 1.0, 0.0), axis=0, keepdims=True),
                    neq + jnp.sum(jnp.where(k == t, 1.0, 0.0), axis=0, keepdims=True))

        return lax.fori_loop(0, nkc, body, (zero_row, zero_row))

    def keep(t, ngt, neq):
        st_scr[0:1, :] = t
        st_scr[1:2, :] = ngt
        st_scr[2:3, :] = neq

    ngt, neq = count_gt_eq(thr_guess)
    keep(thr_guess, ngt, neq)
    is_kth = jnp.where(ngt < float(B_TOPK), jnp.where(ngt + neq >= float(B_TOPK), 1.0, 0.0), 0.0)

    @pl.when(jnp.min(is_kth) < 1.0)
    def _():
        def step(it, key):
            cand = key + lax.shift_left(jnp.int32(1), 31 - it)
            cand_f = lax.bitcast_convert_type(ordered(cand), F32)
            cnt = lax.fori_loop(
                0, nkc, lambda c, acc: acc + jnp.sum(jnp.where(sc_scr[c] >= cand_f, 1.0, 0.0), axis=0,
                                                      keepdims=True), zero_row)
            return jnp.where(cnt >= float(B_TOPK), cand, key)

        t = key_to_float(lax.fori_loop(0, 32, step, jnp.full((1, tq), INT_MIN, jnp.int32)))
        keep(t, *count_gt_eq(t))

    thr = st_scr[0:1, :]
    need = float(B_TOPK) - st_scr[1:2, :]

    @pl.when(jnp.max(st_scr[2:3, :] - need) > 0.0)
    def _():
        ra = lax.broadcasted_iota(jnp.int32, (kc, kc), 0)
        rb = lax.broadcasted_iota(jnp.int32, (kc, kc), 1)
        lower = jnp.where(rb < ra, 1.0, 0.0).astype(BF16)

        def drop_late_ties(c, before):
            k = sc_scr[c]
            eq = jnp.where(k == thr, 1.0, 0.0)
            prefix = jnp.dot(lower, eq.astype(BF16), preferred_element_type=F32) + before
            sc_scr[c] = jnp.where(k == thr, jnp.where(prefix >= need, NEG_INF, k), k)
            return before + jnp.sum(eq, axis=0, keepdims=True)

        lax.fori_loop(0, nkc, drop_late_ties, zero_row)

    thr_sel = jnp.maximum(thr, F32_LOWEST)

    acc_scr[...] = jnp.zeros_like(acc_scr)

    def attend(p, carry):
        ms, ls = carry
        kk = misc_ref[0, pl.ds(p * 2 * kc, 2 * kc), 0:64]
        vt = vt_ref[0, :, pl.ds(p * 2 * kc, 2 * kc)]
        sel = jnp.concatenate([sc_scr[2 * p], sc_scr[2 * p + 1]], axis=0) >= thr_sel
        sel_bias = jnp.where(sel, 0.0, MASK_NEG)
        ss = [lax.dot_general(kk, qs[h], _NT, preferred_element_type=F32) + sel_bias
              for h in range(B_Q_HEADS)]
        ms_new, ls_new, alphas, ps = [], [], [], []
        for h in range(B_Q_HEADS):
            m_new = jnp.maximum(ms[h], jnp.max(ss[h], axis=0, keepdims=True))
            alpha = jnp.exp2(ms[h] - m_new)
            p = jnp.exp2(ss[h] - m_new)
            ls_new.append(alpha * ls[h] + jnp.sum(p, axis=0, keepdims=True))
            ms_new.append(m_new)
            alphas.append(alpha)
            ps.append(p.astype(BF16))
        for h in range(B_Q_HEADS):
            acc_scr[h] = alphas[h] * acc_scr[h] + jnp.dot(vt, ps[h], preferred_element_type=F32)
        return tuple(ms_new), tuple(ls_new)

    init = (tuple(jnp.full((1, tq), MASK_NEG, F32) for _ in range(B_Q_HEADS)),
            tuple(jnp.zeros((1, tq), F32) for _ in range(B_Q_HEADS)))
    carry = init
    for p in range(npairs):
        carry = attend(p, carry)
    ls = carry[1]

    outs = []
    for h in range(0, B_Q_HEADS, 2):
        pair = jnp.concatenate([acc_scr[h] / ls[h], acc_scr[h + 1] / ls[h + 1]], axis=0)
        outs.append(pair.T)
    o_ref[0] = jnp.concatenate(outs, axis=-1).astype(BF16)


def _dsa_attention(bq, biq, iwt, misc, vt):
    bsz, s, _ = bq.shape
    tq = DSA_TQ
    qspec = pl.BlockSpec((1, tq, B_Q_HEADS * HEAD_DIM), lambda b, i: (b, i, 0))
    return pl.pallas_call(
        _dsa_kernel,
        grid=(bsz, s // tq),
        in_specs=[qspec, qspec,
                  pl.BlockSpec((1, B_IDX_HEADS, tq), lambda b, i: (b, 0, i)),
                  pl.BlockSpec((1, s, misc.shape[2]), lambda b, i: (b, 0, 0)),
                  pl.BlockSpec((1, HEAD_DIM, s), lambda b, i: (b, 0, 0))],
        out_specs=qspec,
        out_shape=jax.ShapeDtypeStruct(bq.shape, BF16),
        scratch_shapes=[pltpu.VMEM((s // DSA_KC, DSA_KC, tq), F32),
                        pltpu.VMEM((s // DSA_KC, DSA_KC, tq), jnp.int16),
                        pltpu.VMEM((s // DSA_KC, DSA_KC, tq), jnp.int16),
                        pltpu.VMEM((8, tq), F32),
                        pltpu.VMEM((B_Q_HEADS, HEAD_DIM, tq), F32)],
        compiler_params=_params(("parallel", "arbitrary")),
        name="dsa_attention",
    )(bq, biq, iwt, misc, vt)


POST_TM = 512
POST_ROWS = 256
POST_TF = 1024
POST_VMEM_LIMIT = 56 * 1024 * 1024


def _post_attn_kernel(*refs, widths):
    n = len(widths)
    parts = refs[:n]
    (wout_ref, x_ref, gate1_ref, g1_ref, b1_ref, shift2_ref, scale2_ref, gate2_ref,
     w1_ref, w2_ref, g2_ref, b2_ref, o_ref) = refs[n:]
    groups = [slice(r * POST_ROWS, (r + 1) * POST_ROWS) for r in range(x_ref.shape[1] // POST_ROWS)]

    def out_proj(rows):
        y = None
        off = 0
        for p_ref, wd in zip(parts, widths):
            t = jnp.dot(p_ref[0, rows, :], wout_ref[off:off + wd, :], preferred_element_type=F32)
            y = t if y is None else y + t
            off += wd
        return y

    def norm1(rows, y):
        x1 = _layer_norm(DN_ALPHA * x_ref[0, rows, :] + gate1_ref[0] * y, g1_ref[...], b1_ref[...])
        return x1, (x1 * (1.0 + scale2_ref[0]) + shift2_ref[0]).astype(BF16)

    def mlp(h):
        acc = None
        for f in range(w1_ref.shape[1] // POST_TF):
            cols = slice(f * POST_TF, (f + 1) * POST_TF)
            u = jnp.maximum(jnp.dot(h, w1_ref[:, cols], preferred_element_type=F32), 0.0)
            t = jnp.dot((u * u).astype(BF16), w2_ref[cols, :], preferred_element_type=F32)
            acc = t if acc is None else acc + t
        return acc

    ys = [out_proj(rows) for rows in groups]
    x1s, accs = [], []
    for rows, y in zip(groups, ys):
        x1, h = norm1(rows, y)
        x1s.append(x1)
        accs.append(mlp(h))
    for rows, x1, acc in zip(groups, x1s, accs):
        o_ref[0, rows, :] = _layer_norm(DN_ALPHA * x1 + gate2_ref[0] * acc, g2_ref[...], b2_ref[...])


def _post_attn(parts, w_out, x, gate1, g1, b1, shift2, scale2, gate2, w1, w2, g2, b2):
    bsz, s, d = x.shape
    tm = POST_TM
    widths = tuple(p.shape[-1] for p in parts)
    row = lambda w: pl.BlockSpec((1, tm, w), lambda bi, i: (bi, i, 0))
    vec = pl.BlockSpec((1, 1, d), lambda bi, i: (bi, 0, 0))
    full = lambda a: pl.BlockSpec(a.shape, lambda bi, i: (0,) * a.ndim, pipeline_mode=pl.Buffered(1))
    return pl.pallas_call(
        functools.partial(_post_attn_kernel, widths=widths),
        grid=(bsz, s // tm),
        in_specs=([row(w) for w in widths]
                  + [full(w_out), row(d), vec, full(g1), full(b1), vec, vec, vec,
                     full(w1), full(w2), full(g2), full(b2)]),
        out_specs=row(d),
        out_shape=jax.ShapeDtypeStruct(x.shape, F32),
        compiler_params=pltpu.CompilerParams(dimension_semantics=("parallel", "parallel"),
                                             vmem_limit_bytes=POST_VMEM_LIMIT),
        name="post_attn",
    )(*parts, w_out, x, gate1, g1, b1, shift2, scale2, gate2, w1, w2, g2, b2)


def _proj_c_kernel(x_ref, shift_ref, scale_ref, tab_ref, win_ref, q_ref, k_ref, vt_ref, km_ref):
    h = x_ref[0] * (1.0 + scale_ref[0]) + shift_ref[0]
    proj = jnp.dot(h.astype(BF16), win_ref[...], preferred_element_type=F32)
    cw = C_HEADS * HEAD_DIM
    cos, sa, sb = _rope_coeffs(tab_ref[0])
    q_ref[0] = (_rope(proj[:, 0:cw], cos, sa, sb) * ATT_SCALE).astype(BF16)
    k = _rope(proj[:, cw:2 * cw], cos, sa, sb)
    k_ref[0] = k.astype(BF16)
    vt_ref[0] = proj[:, 2 * cw:3 * cw].T.astype(BF16)
    for t in range(k.shape[0] // C_BLOCK):
        km_ref[0, t] = jnp.mean(k[t * C_BLOCK:(t + 1) * C_BLOCK], axis=0, keepdims=True)


PROJ_C_TM = 2 * C_BLOCK


def _proj_c(x, shift, scale, rope_tab, w_in):
    bsz, s, d = x.shape
    tm = PROJ_C_TM
    nbt = tm // C_BLOCK
    cw = C_HEADS * HEAD_DIM
    row = lambda w: pl.BlockSpec((1, tm, w), lambda b, i: (b, i, 0))
    vec = pl.BlockSpec((1, 1, d), lambda b, i: (b, 0, 0))
    full = lambda a: pl.BlockSpec(a.shape, lambda b, i: (0,) * a.ndim)
    qkv = jax.ShapeDtypeStruct((bsz, s, cw), BF16)
    return pl.pallas_call(
        _proj_c_kernel,
        grid=(bsz, s // tm),
        in_specs=[row(d), vec, vec, pl.BlockSpec((1, 128, tm), lambda b, i: (b, 0, i)), full(w_in)],
        out_specs=[row(cw), row(cw), pl.BlockSpec((1, cw, tm), lambda b, i: (b, 0, i)),
                   pl.BlockSpec((1, nbt, 1, cw), lambda b, i: (b, i, 0, 0))],
        out_shape=[qkv, qkv, jax.ShapeDtypeStruct((bsz, cw, s), BF16),
                   jax.ShapeDtypeStruct((bsz, s // C_BLOCK, 1, cw), F32)],
        compiler_params=_params(("parallel", "parallel")),
        name="proj_c",
    )(x, shift, scale, rope_tab, w_in)


MOBA_HEADS_PER_STEP = 16


def _moba_kernel(q_ref, k_ref, vt_ref, km_ref, o_ref, bias_scr, acc_scr):
    i = pl.program_id(2)
    tq = q_ref.shape[1]
    nb = k_ref.shape[1] // C_BLOCK
    nh = MOBA_HEADS_PER_STEP
    hsl = [slice(hh * HEAD_DIM, (hh + 1) * HEAD_DIM) for hh in range(nh)]
    qs = [q_ref[0, :, hsl[hh]] for hh in range(nh)]

    n_idx = lax.broadcasted_iota(jnp.int32, (nb, tq), 0)
    for hh in range(nh):
        gate = lax.dot_general(km_ref[0, hh].astype(BF16), qs[hh], _NT, preferred_element_type=F32)
        cnt = jnp.zeros((nb, tq), F32)
        for m_idx in range(nb - 1):
            other = gate[m_idx:m_idx + 1, :]
            tie = jnp.where(n_idx > m_idx, 1.0, 0.0)
            beats = jnp.where(other > gate, 1.0, jnp.where(other == gate, tie, 0.0))
            cnt = cnt + jnp.where(m_idx < i, beats, 0.0)
        bias_scr[hh] = jnp.where(n_idx < i, jnp.where(cnt < float(C_TOPK), 0.0, MASK_NEG), MASK_NEG)

    def block_update(blk, nblk, mask, bias, ms, ls):
        rows = nblk * C_BLOCK
        k0 = pl.multiple_of(blk * C_BLOCK, C_BLOCK)
        ss = []
        for hh in range(nh):
            kj = k_ref[0, pl.ds(k0, rows), hsl[hh]]
            ss.append(mask(lax.dot_general(kj, qs[hh], _NT, preferred_element_type=F32)))
        ms_new, ls_new, alphas, ps = [], [], [], []
        for hh in range(nh):
            bs = [bias(hh, t) for t in range(nblk)]
            parts = [ss[hh][t * C_BLOCK:(t + 1) * C_BLOCK] for t in range(nblk)]
            m_new = ms[hh]
            for t in range(nblk):
                m_new = jnp.maximum(m_new, jnp.max(parts[t], axis=0, keepdims=True) + bs[t])
            alpha = jnp.exp2(ms[hh] - m_new)
            p = [jnp.exp2(parts[t] - (m_new - bs[t])) for t in range(nblk)]
            psum = sum(jnp.sum(pt, axis=0, keepdims=True) for pt in p)
            ls_new.append(alpha * ls[hh] + psum)
            ms_new.append(m_new)
            alphas.append(alpha)
            ps.append(jnp.concatenate(p, axis=0).astype(BF16) if nblk > 1 else p[0].astype(BF16))
        for hh in range(nh):
            vj = vt_ref[0, hsl[hh], pl.ds(k0, rows)]
            acc_scr[hh] = alphas[hh] * acc_scr[hh] + jnp.dot(vj, ps[hh], preferred_element_type=F32)
        return tuple(ms_new), tuple(ls_new)

    krow = lax.broadcasted_iota(jnp.int32, (C_BLOCK, tq), 0)
    qcol = lax.broadcasted_iota(jnp.int32, (C_BLOCK, tq), 1)
    causal = krow <= qcol
    acc_scr[...] = jnp.zeros_like(acc_scr)
    ms0 = tuple(jnp.full((1, tq), MASK_NEG, F32) for _ in range(nh))
    ls0 = tuple(jnp.zeros((1, tq), F32) for _ in range(nh))
    carry = block_update(i, 1, lambda s: jnp.where(causal, s, MASK_NEG), lambda hh, t: 0.0, ms0, ls0)
    past_bias = lambda first: (lambda hh, t: bias_scr[hh, pl.ds(first + t, 1), :])

    def past_pair(p, carry):
        return block_update(2 * p, 2, lambda s: s, past_bias(2 * p), carry[0], carry[1])

    carry = lax.fori_loop(0, i // 2, past_pair, carry)
    carry = lax.cond(i % 2 == 1,
                     lambda c: block_update(i - 1, 1, lambda s: s, past_bias(i - 1), c[0], c[1]),
                     lambda c: c, carry)
    ls = carry[1]

    outs = []
    for hh in range(0, nh, 2):
        pair = jnp.concatenate([acc_scr[hh] / ls[hh], acc_scr[hh + 1] / ls[hh + 1]], axis=0)
        outs.append(pair.T)
    o_ref[0] = jnp.concatenate(outs, axis=-1).astype(BF16)


def _moba_attention(q, k, vt, kmean):
    bsz, s, cw = q.shape
    tq = C_BLOCK
    nh = MOBA_HEADS_PER_STEP
    w = nh * HEAD_DIM
    nb = s // C_BLOCK
    qspec = pl.BlockSpec((1, tq, w), lambda b, hg, i: (b, i, hg))
    return pl.pallas_call(
        _moba_kernel,
        grid=(bsz, cw // w, s // tq),
        in_specs=[qspec,
                  pl.BlockSpec((1, s, w), lambda b, hg, i: (b, 0, hg)),
                  pl.BlockSpec((1, w, s), lambda b, hg, i: (b, hg, 0)),
                  pl.BlockSpec((1, nh, nb, HEAD_DIM), lambda b, hg, i: (b, hg, 0, 0))],
        out_specs=qspec,
        out_shape=jax.ShapeDtypeStruct(q.shape, BF16),
        scratch_shapes=[pltpu.VMEM((nh, nb, tq), F32), pltpu.VMEM((nh, HEAD_DIM, tq), F32)],
        compiler_params=_params(("parallel", "parallel", "arbitrary")),
        name="moba_attention",
    )(q, k, vt, kmean)


def _rope_table(positions):
    inv = ROPE_THETA ** (-jnp.arange(0, HEAD_DIM, 2, dtype=F32) / HEAD_DIM)
    ang = positions.astype(F32)[:, None, :] * inv[None, :, None]
    cos, sin = lax.optimization_barrier((jnp.cos(ang), jnp.sin(ang)))
    return jnp.concatenate([cos, sin, cos, sin], axis=1)


def kernel(x, c, positions, ab_w_in, ab_q_norm, ab_w_uq, ab_w_uiq, ab_sinks, ab_w_out, c_w_in, c_w_out,
           ada_w, ada_b, ln_g, ln_b, mlp_w1, mlp_w2):
    bsz, s, d = x.shape
    rope_tab = _rope_table(positions)
    mod = _ada_modulation(c, ada_w, ada_b)

    def mods(idx):
        m = mod[idx]
        return m[:, None, 0:d], m[:, None, d:2 * d], m[:, None, 2 * d:3 * d] + 1.0

    for layer in range(DEPTH):
        shift, scale, gate = mods(2 * layer)
        shift2, scale2, gate2 = mods(2 * layer + 1)
        post = lambda parts, w_out: _post_attn(
            parts, w_out.astype(BF16), x, gate, ln_g[layer, 0][None], ln_b[layer, 0][None],
            shift2, scale2, gate2, mlp_w1[layer].astype(BF16), mlp_w2[layer].astype(BF16),
            ln_g[layer, 1][None], ln_b[layer, 1][None])
        if layer % 2 == 0:
            e = layer // 2
            w_in = jnp.pad(ab_w_in[e], ((0, 0), (0, AB_IN_PAD - AB_IN_WIDTH))).astype(BF16)
            aq, ak, avt, bq, biq, misc, bvt, iwt = _proj_ab(
                x, shift, scale, rope_tab, w_in, ab_q_norm[e][None],
                ab_w_uq[e].astype(BF16), ab_w_uiq[e].astype(BF16))
            ya = _swa_attention(ab_sinks[e], aq, ak, avt)
            yb = _dsa_attention(bq, biq, iwt, misc, bvt)
            x = post([ya, yb], ab_w_out[e])
        else:
            o = layer // 2
            q, k, vt, kmean = _proj_c(x, shift, scale, rope_tab, c_w_in[o].astype(BF16))
            km = kmean.reshape(bsz, s // C_BLOCK, C_HEADS, HEAD_DIM).transpose(0, 2, 1, 3)
            y = _moba_attention(q, k, vt, km)
            x = post([y], c_w_out[o])
    return x
```

```python
import functools

import jax
import jax.numpy as jnp
from jax import lax
from jax.experimental import pallas as pl
from jax.experimental.pallas import tpu as pltpu

HEAD_DIM = 64
ROPE_THETA = 10000.0
DEPTH = 2
A_Q_HEADS = 8
A_KV_HEADS = 2
A_BLOCK = 128
B_Q_HEADS = 8
B_IDX_HEADS = 8
B_TOPK = 256
C_HEADS = 16
C_BLOCK = 256
C_TOPK = 3
DN_ALPHA = (2 * DEPTH) ** 0.25
LN_EPS = 1e-5
RMS_EPS = 1e-6
AB_IN_WIDTH = 1224
AB_IN_PAD = 1280
LOG2E = 1.4426950408889634
ATT_SCALE = HEAD_DIM ** -0.5 * LOG2E
IDX_SCALE = B_IDX_HEADS ** -0.5 * HEAD_DIM ** -0.5

F32 = jnp.float32
BF16 = jnp.bfloat16
MASK_NEG = -1e30
NEG_INF = float("-inf")
F32_LOWEST = -3.4028234663852886e38
INT_MIN = -(2 ** 31)
HALF = 2 ** 15
VMEM_LIMIT = 48 * 1024 * 1024

_NT = (((1,), (1,)), ((), ()))


def _params(sem):
    return pltpu.CompilerParams(dimension_semantics=sem, vmem_limit_bytes=VMEM_LIMIT)


def _rope(t, cos, sin_a, sin_b):
    outs = []
    for k in range(t.shape[-1] // 128):
        blk = t[:, 128 * k:128 * (k + 1)]
        outs.append(blk * cos + pltpu.roll(blk, 32, 1) * sin_a + pltpu.roll(blk, 96, 1) * sin_b)
    return outs[0] if len(outs) == 1 else jnp.concatenate(outs, axis=-1)


def _rope_coeffs(tab_t):
    tab = tab_t.T
    low = (lax.broadcasted_iota(jnp.int32, tab.shape, 1) & (HEAD_DIM - 1)) < HEAD_DIM // 2
    swapped = pltpu.roll(tab, HEAD_DIM // 2, 1)
    cos = jnp.where(low, tab, swapped)
    sin = jnp.where(low, swapped, tab)
    return cos, jnp.where(low, 0.0, sin), jnp.where(low, -sin, 0.0)


def _layer_norm(z, g, b):
    mu = jnp.mean(z, axis=-1, keepdims=True)
    zc = z - mu
    var = jnp.mean(zc * zc, axis=-1, keepdims=True)
    return zc * lax.rsqrt(var + LN_EPS) * g + b


def _ada_kernel(c_ref, w_ref, b_ref, o_ref):
    c = c_ref[...]
    sc = c / (1.0 + jnp.exp(-c))
    o_ref[0] = jnp.dot(sc, w_ref[0], preferred_element_type=F32) + b_ref[0]


def _ada_modulation(c, ada_w, ada_b):
    n = ada_w.shape[0] * ada_w.shape[1]
    bsz, d = c.shape
    w = ada_w.reshape(n, d, 3 * d)
    b = ada_b.reshape(n, 1, 3 * d)
    tn = 1024
    return pl.pallas_call(
        _ada_kernel,
        grid=(n, 3 * d // tn),
        in_specs=[
            pl.BlockSpec((bsz, d), lambda i, j: (0, 0)),
            pl.BlockSpec((1, d, tn), lambda i, j: (i, 0, j)),
            pl.BlockSpec((1, 1, tn), lambda i, j: (i, 0, j)),
        ],
        out_specs=pl.BlockSpec((1, bsz, tn), lambda i, j: (i, 0, j)),
        out_shape=jax.ShapeDtypeStruct((n, bsz, 3 * d), F32),
        compiler_params=_params(("arbitrary", "arbitrary")),
        name="ada_modulation",
    )(c, w, b)


PROJ_AB_GROUPS = 2


def _proj_ab_kernel(x_ref, shift_ref, scale_ref, tab_ref, win_ref, qn_ref, wuq_ref, wuiq_ref,
                    aq_ref, ak_ref, avt_ref, bq_ref, biq_ref, misc_ref, bvt_ref, iwt_ref):
    tm = x_ref.shape[1]
    rows_of = [slice(g * tm // PROJ_AB_GROUPS, (g + 1) * tm // PROJ_AB_GROUPS) for g in range(PROJ_AB_GROUPS)]
    projs = []
    for rows in rows_of:
        h = x_ref[0, rows, :] * (1.0 + scale_ref[0]) + shift_ref[0]
        projs.append(jnp.dot(h.astype(BF16), win_ref[...], preferred_element_type=F32))
    lows = []
    for proj in projs:
        cq = proj[:, 768:1024]
        ms = jnp.mean(cq * cq, axis=-1, keepdims=True)
        cqn = (cq * lax.rsqrt(ms + RMS_EPS) * qn_ref[...]).astype(BF16)
        lows.append((jnp.dot(cqn, wuq_ref[...], preferred_element_type=F32),
                     jnp.dot(cqn, wuiq_ref[...], preferred_element_type=F32)))

    for rows, proj, (bq, biq) in zip(rows_of, projs, lows):
        cos, sa, sb = _rope_coeffs(tab_ref[0, :, rows])
        aq_ref[0, rows, :] = (_rope(proj[:, 0:512], cos, sa, sb) * ATT_SCALE).astype(BF16)
        ak_ref[0, rows, :] = _rope(proj[:, 512:640], cos, sa, sb).astype(BF16)
        avt_ref[0, :, rows] = proj[:, 640:768].T.astype(BF16)
        bq_ref[0, rows, :] = (_rope(bq, cos, sa, sb) * ATT_SCALE).astype(BF16)
        biq_ref[0, rows, :] = _rope(biq, cos, sa, sb).astype(BF16)

        lane = lax.broadcasted_iota(jnp.int32, cos.shape, 1)
        roped = lane < HEAD_DIM
        c0 = jnp.where(roped, cos, 1.0)
        a0 = jnp.where(roped, sa, 0.0)
        b0 = jnp.where(roped, sb, 0.0)
        m0 = _rope(proj[:, 1024:1152], c0, a0, b0)
        m1 = _rope(proj[:, 1152:1280], c0, a0, b0)
        misc_ref[0, rows, :] = jnp.concatenate([m0, m1], axis=-1).astype(BF16)
        bvt_ref[0, :, rows] = m0.T[HEAD_DIM:2 * HEAD_DIM].astype(BF16)
        iwt_ref[0, :, rows] = m1.T[HEAD_DIM:HEAD_DIM + B_IDX_HEADS] * IDX_SCALE


def _proj_ab(x, shift, scale, rope_tab, w_in, q_norm, w_uq, w_uiq, tm=512):
    bsz, s, d = x.shape
    row = lambda w: pl.BlockSpec((1, tm, w), lambda b, i: (b, i, 0))
    vec = pl.BlockSpec((1, 1, d), lambda b, i: (b, 0, 0))
    full = lambda a: pl.BlockSpec(a.shape, lambda b, i: (0,) * a.ndim)
    col = lambda r: pl.BlockSpec((1, r, tm), lambda b, i: (b, 0, i))
    tok = lambda w: jax.ShapeDtypeStruct((bsz, s, w), BF16)
    kvw = A_KV_HEADS * HEAD_DIM
    out_shape = [tok(512), tok(kvw), jax.ShapeDtypeStruct((bsz, kvw, s), BF16), tok(512), tok(512), tok(256),
                 jax.ShapeDtypeStruct((bsz, HEAD_DIM, s), BF16), jax.ShapeDtypeStruct((bsz, B_IDX_HEADS, s), F32)]
    out_specs = [row(512), row(kvw), col(kvw), row(512), row(512), row(256), col(HEAD_DIM), col(B_IDX_HEADS)]
    return pl.pallas_call(
        _proj_ab_kernel,
        grid=(bsz, s // tm),
        in_specs=[row(d), vec, vec, col(128), full(w_in), full(q_norm), full(w_uq), full(w_uiq)],
        out_specs=out_specs,
        out_shape=out_shape,
        compiler_params=_params(("parallel", "parallel")),
        name="proj_ab",
    )(x, shift, scale, rope_tab, w_in, q_norm, w_uq, w_uiq)


SWA_BLOCKS = 8
SWA_TQ = SWA_BLOCKS * A_BLOCK


def _swa_kernel(sink_ref, q_ref, kp_ref, kc_ref, vtp_ref, vtc_ref, o_ref):
    i = pl.program_id(1)
    group = A_Q_HEADS // A_KV_HEADS
    kband = jnp.concatenate([kp_ref[0], kc_ref[0]], axis=0)
    vtband = jnp.concatenate([vtp_ref[0], vtc_ref[0]], axis=1)
    c = lax.broadcasted_iota(jnp.int32, (2 * A_BLOCK, A_BLOCK), 0)
    qi = lax.broadcasted_iota(jnp.int32, (2 * A_BLOCK, A_BLOCK), 1)
    in_window = (c > qi) & (c <= qi + A_BLOCK)
    later = jnp.where(in_window, 0.0, MASK_NEG)
    masks = [jnp.where(in_window & ((c >= A_BLOCK) | (i > 0)), 0.0, MASK_NEG)] + [later] * (SWA_BLOCKS - 1)

    ss = {}
    for t in range(SWA_BLOCKS):
        for kh in range(A_KV_HEADS):
            kb = kband[t * A_BLOCK:(t + 2) * A_BLOCK, kh * HEAD_DIM:(kh + 1) * HEAD_DIM]
            qstack = jnp.concatenate(
                [q_ref[0, t * A_BLOCK:(t + 1) * A_BLOCK, hq * HEAD_DIM:(hq + 1) * HEAD_DIM]
                 for hq in range(kh * group, (kh + 1) * group)], axis=0)
            s4 = lax.dot_general(kb, qstack, _NT, preferred_element_type=F32)
            for g in range(group):
                ss[t, kh * group + g] = s4[:, g * A_BLOCK:(g + 1) * A_BLOCK] + masks[t]
    ps, dens = {}, {}
    for t in range(SWA_BLOCKS):
        for hq in range(A_Q_HEADS):
            sink = sink_ref[hq] * LOG2E
            m = jnp.maximum(jnp.max(ss[t, hq], axis=0, keepdims=True), sink)
            p = jnp.exp2(ss[t, hq] - m)
            dens[t, hq] = jnp.sum(p, axis=0, keepdims=True) + jnp.exp2(sink - m)
            ps[t, hq] = p.astype(BF16)
    for t in range(SWA_BLOCKS):
        outs = []
        for hq in range(A_Q_HEADS):
            kh = hq // group
            vt = vtband[kh * HEAD_DIM:(kh + 1) * HEAD_DIM, t * A_BLOCK:(t + 2) * A_BLOCK]
            outs.append(jnp.dot(vt, ps[t, hq], preferred_element_type=F32) / dens[t, hq])
        o_ref[0, t * A_BLOCK:(t + 1) * A_BLOCK, :] = jnp.concatenate(outs, axis=0).T.astype(BF16)


def _swa_attention(sinks, aq, ak, avt):
    bsz, s, qw = aq.shape
    kvw = A_KV_HEADS * HEAD_DIM
    prev_blk = lambda i: jnp.maximum(SWA_BLOCKS * i - 1, 0)
    qspec = pl.BlockSpec((1, SWA_TQ, qw), lambda b, i: (b, i, 0))
    return pl.pallas_call(
        _swa_kernel,
        grid=(bsz, s // SWA_TQ),
        in_specs=[pl.BlockSpec(memory_space=pltpu.SMEM), qspec,
                  pl.BlockSpec((1, A_BLOCK, kvw), lambda b, i: (b, prev_blk(i), 0)),
                  pl.BlockSpec((1, SWA_TQ, kvw), lambda b, i: (b, i, 0)),
                  pl.BlockSpec((1, kvw, A_BLOCK), lambda b, i: (b, 0, prev_blk(i))),
                  pl.BlockSpec((1, kvw, SWA_TQ), lambda b, i: (b, 0, i))],
        out_specs=qspec,
        out_shape=jax.ShapeDtypeStruct(aq.shape, BF16),
        compiler_params=_params(("parallel", "parallel")),
        name="swa_attention",
    )(sinks, aq, ak, ak, avt, avt)


DSA_TQ = 256
DSA_KC = 256


def _dsa_kernel(*refs):
    npairs = (pl.program_id(1) + 2) // 2
    for n in range(1, refs[-5].shape[0] // 2 + 1):
        pl.when(npairs == n)(functools.partial(_dsa_body, n, *refs))


def _dsa_body(npairs, q_ref, iq_ref, iwt_ref, misc_ref, vt_ref, o_ref, sc_scr, hi_scr, lo_scr, st_scr, acc_scr):
    i = pl.program_id(1)
    tq, kc = DSA_TQ, DSA_KC
    nkc = i + 1
    krow = lax.broadcasted_iota(jnp.int32, (kc, tq), 0)
    qcol = lax.broadcasted_iota(jnp.int32, (kc, tq), 1)
    on_or_below_diag = krow <= qcol
    iwt = iwt_ref[0]
    hsl = [slice(h * HEAD_DIM, (h + 1) * HEAD_DIM) for h in range(B_Q_HEADS)]
    iqs = [iq_ref[0, :, hsl[h]] for h in range(B_IDX_HEADS)]
    qs = [q_ref[0, :, hsl[h]] for h in range(B_Q_HEADS)]


    def ordered(bits):
        return jnp.where(bits >= 0, bits, bits ^ jnp.int32(0x7FFFFFFF))

    def key_to_float(key):
        return jnp.where(key == INT_MIN, NEG_INF, lax.bitcast_convert_type(ordered(key), F32))

    def score_chunk(c):
        ik = misc_ref[0, pl.ds(c * kc, kc), 128:192]
        sc = jnp.zeros((kc, tq), F32)
        for h in range(B_IDX_HEADS):
            raw = lax.dot_general(ik, iqs[h], _NT, preferred_element_type=F32)
            sc = sc + jnp.maximum(raw, 0.0) * iwt[h:h + 1, :]
        sc = jnp.where(sc == 0.0, 0.0, sc)
        key = ordered(lax.bitcast_convert_type(sc, jnp.int32))
        sc_scr[c] = jnp.where(c < i, sc, jnp.where(c == i, jnp.where(on_or_below_diag, sc, NEG_INF), NEG_INF))
        key = jnp.where(c < i, key, jnp.where(c == i, jnp.where(on_or_below_diag, key, INT_MIN), INT_MIN))
        hi_scr[c] = (key >> 16).astype(jnp.int16)
        lo_scr[c] = ((key & 0xFFFF) - HALF).astype(jnp.int16)

    for c in range(2 * npairs):
        score_chunk(c)

    def count_ge(ref, c, cand):
        ones = jnp.where(ref[c] >= cand.astype(jnp.int16), jnp.int16(1), jnp.int16(0))
        part = ones[0:16]
        for r in range(1, kc // 16):
            part = part + ones[16 * r:16 * (r + 1)]
        return jnp.sum(part.astype(jnp.int32), axis=0, keepdims=True)

    def half_search(ref, wanted):
        def step(it, thr):
            cand = thr + lax.shift_left(jnp.int32(1), 15 - it)
            cnt = count_ge(ref, 0, cand)
            for c in range(1, 2 * npairs):
                cnt = cnt + count_ge(ref, c, cand)
            return jnp.where(cnt >= wanted, cand, thr)

        return lax.fori_loop(0, 16, step, jnp.full((1, tq), -HALF, jnp.int32))

    thr_hi = half_search(hi_scr, B_TOPK)

    def mask_low(c, above):
        hi = hi_scr[c].astype(jnp.int32)
        lo_scr[c] = jnp.where(hi == thr_hi, lo_scr[c].astype(jnp.int32), -HALF).astype(jnp.int16)
        return above + jnp.sum(jnp.where(hi > thr_hi, 1, 0), axis=0, keepdims=True)

    above = lax.fori_loop(0, nkc, mask_low, jnp.zeros((1, tq), jnp.int32))
    thr_lo = half_search(lo_scr, B_TOPK - above)
    thr_guess = key_to_float((thr_hi << 16) | (thr_lo + HALF))

    zero_row = jnp.zeros((1, tq), F32)

    def count_gt_eq(t):
        def body(c, carry):
            ngt, neq = carry
            k = sc_scr[c]
            return (ngt + jnp.sum(jnp.where(k > t, 1.0, 0.0), axis=0, keepdims=True),
                    neq + jnp.sum(jnp.where(k == t, 1.0, 0.0), axis=0, keepdims=True))

        return lax.fori_loop(0, nkc, body, (zero_row, zero_row))

    def keep(t, ngt, neq):
        st_scr[0:1, :] = t
        st_scr[1:2, :] = ngt
        st_scr[2:3, :] = neq

    ngt, neq = count_gt_eq(thr_guess)
    keep(thr_guess, ngt, neq)
    is_kth = jnp.where(ngt < float(B_TOPK), jnp.where(ngt + neq >= float(B_TOPK), 1.0, 0.0), 0.0)

    @pl.when(jnp.min(is_kth) < 1.0)
    def _():
        def step(it, key):
            cand = key + lax.shift_left(jnp.int32(1), 31 - it)
            cand_f = lax.bitcast_convert_type(ordered(cand), F32)
            cnt = lax.fori_loop(
                0, nkc, lambda c, acc: acc + jnp.sum(jnp.where(sc_scr[c] >= cand_f, 1.0, 0.0), axis=0,
                                                      keepdims=True), zero_row)
            return jnp.where(cnt >= float(B_TOPK), cand, key)

        t = key_to_float(lax.fori_loop(0, 32, step, jnp.full((1, tq), INT_MIN, jnp.int32)))
        keep(t, *count_gt_eq(t))

    thr = st_scr[0:1, :]
    need = float(B_TOPK) - st_scr[1:2, :]

    @pl.when(jnp.max(st_scr[2:3, :] - need) > 0.0)
    def _():
        ra = lax.broadcasted_iota(jnp.int32, (kc, kc), 0)
        rb = lax.broadcasted_iota(jnp.int32, (kc, kc), 1)
        lower = jnp.where(rb < ra, 1.0, 0.0).astype(BF16)

        def drop_late_ties(c, before):
            k = sc_scr[c]
            eq = jnp.where(k == thr, 1.0, 0.0)
            prefix = jnp.dot(lower, eq.astype(BF16), preferred_element_type=F32) + before
            sc_scr[c] = jnp.where(k == thr, jnp.where(prefix >= need, NEG_INF, k), k)
            return before + jnp.sum(eq, axis=0, keepdims=True)

        lax.fori_loop(0, nkc, drop_late_ties, zero_row)

    thr_sel = jnp.maximum(thr, F32_LOWEST)

    acc_scr[...] = jnp.zeros_like(acc_scr)

    def attend(p, carry):
        ms, ls = carry
        kk = misc_ref[0, pl.ds(p * 2 * kc, 2 * kc), 0:64]
        vt = vt_ref[0, :, pl.ds(p * 2 * kc, 2 * kc)]
        sel = jnp.concatenate([sc_scr[2 * p], sc_scr[2 * p + 1]], axis=0) >= thr_sel
        sel_bias = jnp.where(sel, 0.0, MASK_NEG)
        ss = [lax.dot_general(kk, qs[h], _NT, preferred_element_type=F32) + sel_bias
              for h in range(B_Q_HEADS)]
        ms_new, ls_new, alphas, ps = [], [], [], []
        for h in range(B_Q_HEADS):
            m_new = jnp.maximum(ms[h], jnp.max(ss[h], axis=0, keepdims=True))
            alpha = jnp.exp2(ms[h] - m_new)
            p = jnp.exp2(ss[h] - m_new)
            ls_new.append(alpha * ls[h] + jnp.sum(p, axis=0, keepdims=True))
            ms_new.append(m_new)
            alphas.append(alpha)
            ps.append(p.astype(BF16))
        for h in range(B_Q_HEADS):
            acc_scr[h] = alphas[h] * acc_scr[h] + jnp.dot(vt, ps[h], preferred_element_type=F32)
        return tuple(ms_new), tuple(ls_new)

    init = (tuple(jnp.full((1, tq), MASK_NEG, F32) for _ in range(B_Q_HEADS)),
            tuple(jnp.zeros((1, tq), F32) for _ in range(B_Q_HEADS)))
    carry = init
    for p in range(npairs):
        carry = attend(p, carry)
    ls = carry[1]

    outs = []
    for h in range(0, B_Q_HEADS, 2):
        pair = jnp.concatenate([acc_scr[h] / ls[h], acc_scr[h + 1] / ls[h + 1]], axis=0)
        outs.append(pair.T)
    o_ref[0] = jnp.concatenate(outs, axis=-1).astype(BF16)


def _dsa_attention(bq, biq, iwt, misc, vt):
    bsz, s, _ = bq.shape
    tq = DSA_TQ
    qspec = pl.BlockSpec((1, tq, B_Q_HEADS * HEAD_DIM), lambda b, i: (b, i, 0))
    return pl.pallas_call(
        _dsa_kernel,
        grid=(bsz, s // tq),
        in_specs=[qspec, qspec,
                  pl.BlockSpec((1, B_IDX_HEADS, tq), lambda b, i: (b, 0, i)),
                  pl.BlockSpec((1, s, misc.shape[2]), lambda b, i: (b, 0, 0)),
                  pl.BlockSpec((1, HEAD_DIM, s), lambda b, i: (b, 0, 0))],
        out_specs=qspec,
        out_shape=jax.ShapeDtypeStruct(bq.shape, BF16),
        scratch_shapes=[pltpu.VMEM((s // DSA_KC, DSA_KC, tq), F32),
                        pltpu.VMEM((s // DSA_KC, DSA_KC, tq), jnp.int16),
                        pltpu.VMEM((s // DSA_KC, DSA_KC, tq), jnp.int16),
                        pltpu.VMEM((8, tq), F32),
                        pltpu.VMEM((B_Q_HEADS, HEAD_DIM, tq), F32)],
        compiler_params=_params(("parallel", "arbitrary")),
        name="dsa_attention",
    )(bq, biq, iwt, misc, vt)


POST_TM = 512
POST_ROWS = 256
POST_TF = 1024
POST_VMEM_LIMIT = 56 * 1024 * 1024


def _post_attn_kernel(*refs, widths):
    n = len(widths)
    parts = refs[:n]
    (wout_ref, x_ref, gate1_ref, g1_ref, b1_ref, shift2_ref, scale2_ref, gate2_ref,
     w1_ref, w2_ref, g2_ref, b2_ref, o_ref) = refs[n:]
    groups = [slice(r * POST_ROWS, (r + 1) * POST_ROWS) for r in range(x_ref.shape[1] // POST_ROWS)]

    def out_proj(rows):
        y = None
        off = 0
        for p_ref, wd in zip(parts, widths):
            t = jnp.dot(p_ref[0, rows, :], wout_ref[off:off + wd, :], preferred_element_type=F32)
            y = t if y is None else y + t
            off += wd
        return y

    def norm1(rows, y):
        x1 = _layer_norm(DN_ALPHA * x_ref[0, rows, :] + gate1_ref[0] * y, g1_ref[...], b1_ref[...])
        return x1, (x1 * (1.0 + scale2_ref[0]) + shift2_ref[0]).astype(BF16)

    def mlp(h):
        acc = None
        for f in range(w1_ref.shape[1] // POST_TF):
            cols = slice(f * POST_TF, (f + 1) * POST_TF)
            u = jnp.maximum(jnp.dot(h, w1_ref[:, cols], preferred_element_type=F32), 0.0)
            t = jnp.dot((u * u).astype(BF16), w2_ref[cols, :], preferred_element_type=F32)
            acc = t if acc is None else acc + t
        return acc

    ys = [out_proj(rows) for rows in groups]
    x1s, accs = [], []
    for rows, y in zip(groups, ys):
        x1, h = norm1(rows, y)
        x1s.append(x1)
        accs.append(mlp(h))
    for rows, x1, acc in zip(groups, x1s, accs):
        o_ref[0, rows, :] = _layer_norm(DN_ALPHA * x1 + gate2_ref[0] * acc, g2_ref[...], b2_ref[...])


def _post_attn(parts, w_out, x, gate1, g1, b1, shift2, scale2, gate2, w1, w2, g2, b2):
    bsz, s, d = x.shape
    tm = POST_TM
    widths = tuple(p.shape[-1] for p in parts)
    row = lambda w: pl.BlockSpec((1, tm, w), lambda bi, i: (bi, i, 0))
    vec = pl.BlockSpec((1, 1, d), lambda bi, i: (bi, 0, 0))
    full = lambda a: pl.BlockSpec(a.shape, lambda bi, i: (0,) * a.ndim, pipeline_mode=pl.Buffered(1))
    return pl.pallas_call(
        functools.partial(_post_attn_kernel, widths=widths),
        grid=(bsz, s // tm),
        in_specs=([row(w) for w in widths]
                  + [full(w_out), row(d), vec, full(g1), full(b1), vec, vec, vec,
                     full(w1), full(w2), full(g2), full(b2)]),
        out_specs=row(d),
        out_shape=jax.ShapeDtypeStruct(x.shape, F32),
        compiler_params=pltpu.CompilerParams(dimension_semantics=("parallel", "parallel"),
                                             vmem_limit_bytes=POST_VMEM_LIMIT),
        name="post_attn",
    )(*parts, w_out, x, gate1, g1, b1, shift2, scale2, gate2, w1, w2, g2, b2)


def _proj_c_kernel(x_ref, shift_ref, scale_ref, tab_ref, win_ref, q_ref, k_ref, vt_ref, km_ref):
    h = x_ref[0] * (1.0 + scale_ref[0]) + shift_ref[0]
    proj = jnp.dot(h.astype(BF16), win_ref[...], preferred_element_type=F32)
    cw = C_HEADS * HEAD_DIM
    cos, sa, sb = _rope_coeffs(tab_ref[0])
    q_ref[0] = (_rope(proj[:, 0:cw], cos, sa, sb) * ATT_SCALE).astype(BF16)
    k = _rope(proj[:, cw:2 * cw], cos, sa, sb)
    k_ref[0] = k.astype(BF16)
    vt_ref[0] = proj[:, 2 * cw:3 * cw].T.astype(BF16)
    for t in range(k.shape[0] // C_BLOCK):
        km_ref[0, t] = jnp.mean(k[t * C_BLOCK:(t + 1) * C_BLOCK], axis=0, keepdims=True)


PROJ_C_TM = 2 * C_BLOCK


def _proj_c(x, shift, scale, rope_tab, w_in):
    bsz, s, d = x.shape
    tm = PROJ_C_TM
    nbt = tm // C_BLOCK
    cw = C_HEADS * HEAD_DIM
    row = lambda w: pl.BlockSpec((1, tm, w), lambda b, i: (b, i, 0))
    vec = pl.BlockSpec((1, 1, d), lambda b, i: (b, 0, 0))
    full = lambda a: pl.BlockSpec(a.shape, lambda b, i: (0,) * a.ndim)
    qkv = jax.ShapeDtypeStruct((bsz, s, cw), BF16)
    return pl.pallas_call(
        _proj_c_kernel,
        grid=(bsz, s // tm),
        in_specs=[row(d), vec, vec, pl.BlockSpec((1, 128, tm), lambda b, i: (b, 0, i)), full(w_in)],
        out_specs=[row(cw), row(cw), pl.BlockSpec((1, cw, tm), lambda b, i: (b, 0, i)),
                   pl.BlockSpec((1, nbt, 1, cw), lambda b, i: (b, i, 0, 0))],
        out_shape=[qkv, qkv, jax.ShapeDtypeStruct((bsz, cw, s), BF16),
                   jax.ShapeDtypeStruct((bsz, s // C_BLOCK, 1, cw), F32)],
        compiler_params=_params(("parallel", "parallel")),
        name="proj_c",
    )(x, shift, scale, rope_tab, w_in)


MOBA_HEADS_PER_STEP = 16


def _moba_kernel(q_ref, k_ref, vt_ref, km_ref, o_ref, bias_scr, acc_scr):
    i = pl.program_id(2)
    tq = q_ref.shape[1]
    nb = k_ref.shape[1] // C_BLOCK
    nh = MOBA_HEADS_PER_STEP
    hsl = [slice(hh * HEAD_DIM, (hh + 1) * HEAD_DIM) for hh in range(nh)]
    qs = [q_ref[0, :, hsl[hh]] for hh in range(nh)]

    n_idx = lax.broadcasted_iota(jnp.int32, (nb, tq), 0)
    for hh in range(nh):
        gate = lax.dot_general(km_ref[0, hh].astype(BF16), qs[hh], _NT, preferred_element_type=F32)
        cnt = jnp.zeros((nb, tq), F32)
        for m_idx in range(nb - 1):
            other = gate[m_idx:m_idx + 1, :]
            tie = jnp.where(n_idx > m_idx, 1.0, 0.0)
            beats = jnp.where(other > gate, 1.0, jnp.where(other == gate, tie, 0.0))
            cnt = cnt + jnp.where(m_idx < i, beats, 0.0)
        bias_scr[hh] = jnp.where(n_idx < i, jnp.where(cnt < float(C_TOPK), 0.0, MASK_NEG), MASK_NEG)

    def block_update(blk, nblk, mask, bias, ms, ls):
        rows = nblk * C_BLOCK
        k0 = pl.multiple_of(blk * C_BLOCK, C_BLOCK)
        ss = []
        for hh in range(nh):
            kj = k_ref[0, pl.ds(k0, rows), hsl[hh]]
            ss.append(mask(lax.dot_general(kj, qs[hh], _NT, preferred_element_type=F32)))
        ms_new, ls_new, alphas, ps = [], [], [], []
        for hh in range(nh):
            bs = [bias(hh, t) for t in range(nblk)]
            parts = [ss[hh][t * C_BLOCK:(t + 1) * C_BLOCK] for t in range(nblk)]
            m_new = ms[hh]
            for t in range(nblk):
                m_new = jnp.maximum(m_new, jnp.max(parts[t], axis=0, keepdims=True) + bs[t])
            alpha = jnp.exp2(ms[hh] - m_new)
            p = [jnp.exp2(parts[t] - (m_new - bs[t])) for t in range(nblk)]
            psum = sum(jnp.sum(pt, axis=0, keepdims=True) for pt in p)
            ls_new.append(alpha * ls[hh] + psum)
            ms_new.append(m_new)
            alphas.append(alpha)
            ps.append(jnp.concatenate(p, axis=0).astype(BF16) if nblk > 1 else p[0].astype(BF16))
        for hh in range(nh):
            vj = vt_ref[0, hsl[hh], pl.ds(k0, rows)]
            acc_scr[hh] = alphas[hh] * acc_scr[hh] + jnp.dot(vj, ps[hh], preferred_element_type=F32)
        return tuple(ms_new), tuple(ls_new)

    krow = lax.broadcasted_iota(jnp.int32, (C_BLOCK, tq), 0)
    qcol = lax.broadcasted_iota(jnp.int32, (C_BLOCK, tq), 1)
    causal = krow <= qcol
    acc_scr[...] = jnp.zeros_like(acc_scr)
    ms0 = tuple(jnp.full((1, tq), MASK_NEG, F32) for _ in range(nh))
    ls0 = tuple(jnp.zeros((1, tq), F32) for _ in range(nh))
    carry = block_update(i, 1, lambda s: jnp.where(causal, s, MASK_NEG), lambda hh, t: 0.0, ms0, ls0)
    past_bias = lambda first: (lambda hh, t: bias_scr[hh, pl.ds(first + t, 1), :])

    def past_pair(p, carry):
        return block_update(2 * p, 2, lambda s: s, past_bias(2 * p), carry[0], carry[1])

    carry = lax.fori_loop(0, i // 2, past_pair, carry)
    carry = lax.cond(i % 2 == 1,
                     lambda c: block_update(i - 1, 1, lambda s: s, past_bias(i - 1), c[0], c[1]),
                     lambda c: c, carry)
    ls = carry[1]

    outs = []
    for hh in range(0, nh, 2):
        pair = jnp.concatenate([acc_scr[hh] / ls[hh], acc_scr[hh + 1] / ls[hh + 1]], axis=0)
        outs.append(pair.T)
    o_ref[0] = jnp.concatenate(outs, axis=-1).astype(BF16)


def _moba_attention(q, k, vt, kmean):
    bsz, s, cw = q.shape
    tq = C_BLOCK
    nh = MOBA_HEADS_PER_STEP
    w = nh * HEAD_DIM
    nb = s // C_BLOCK
    qspec = pl.BlockSpec((1, tq, w), lambda b, hg, i: (b, i, hg))
    return pl.pallas_call(
        _moba_kernel,
        grid=(bsz, cw // w, s // tq),
        in_specs=[qspec,
                  pl.BlockSpec((1, s, w), lambda b, hg, i: (b, 0, hg)),
                  pl.BlockSpec((1, w, s), lambda b, hg, i: (b, hg, 0)),
                  pl.BlockSpec((1, nh, nb, HEAD_DIM), lambda b, hg, i: (b, hg, 0, 0))],
        out_specs=qspec,
        out_shape=jax.ShapeDtypeStruct(q.shape, BF16),
        scratch_shapes=[pltpu.VMEM((nh, nb, tq), F32), pltpu.VMEM((nh, HEAD_DIM, tq), F32)],
        compiler_params=_params(("parallel", "parallel", "arbitrary")),
        name="moba_attention",
    )(q, k, vt, kmean)


def _rope_table(positions):
    inv = ROPE_THETA ** (-jnp.arange(0, HEAD_DIM, 2, dtype=F32) / HEAD_DIM)
    ang = positions.astype(F32)[:, None, :] * inv[None, :, None]
    cos, sin = lax.optimization_barrier((jnp.cos(ang), jnp.sin(ang)))
    return jnp.concatenate([cos, sin, cos, sin], axis=1)


def kernel(x, c, positions, ab_w_in, ab_q_norm, ab_w_uq, ab_w_uiq, ab_sinks, ab_w_out, c_w_in, c_w_out,
           ada_w, ada_b, ln_g, ln_b, mlp_w1, mlp_w2):
    bsz, s, d = x.shape
    rope_tab = _rope_table(positions)
    mod = _ada_modulation(c, ada_w, ada_b)

    def mods(idx):
        m = mod[idx]
        return m[:, None, 0:d], m[:, None, d:2 * d], m[:, None, 2 * d:3 * d] + 1.0

    for layer in range(DEPTH):
        shift, scale, gate = mods(2 * layer)
        shift2, scale2, gate2 = mods(2 * layer + 1)
        post = lambda parts, w_out: _post_attn(
            parts, w_out.astype(BF16), x, gate, ln_g[layer, 0][None], ln_b[layer, 0][None],
            shift2, scale2, gate2, mlp_w1[layer].astype(BF16), mlp_w2[layer].astype(BF16),
            ln_g[layer, 1][None], ln_b[layer, 1][None])
        if layer % 2 == 0:
            e = layer // 2
            w_in = jnp.pad(ab_w_in[e], ((0, 0), (0, AB_IN_PAD - AB_IN_WIDTH))).astype(BF16)
            aq, ak, avt, bq, biq, misc, bvt, iwt = _proj_ab(
                x, shift, scale, rope_tab, w_in, ab_q_norm[e][None],
                ab_w_uq[e].astype(BF16), ab_w_uiq[e].astype(BF16))
            ya = _swa_attention(ab_sinks[e], aq, ak, avt)
            yb = _dsa_attention(bq, biq, iwt, misc, bvt)
            x = post([ya, yb], ab_w_out[e])
        else:
            o = layer // 2
            q, k, vt, kmean = _proj_c(x, shift, scale, rope_tab, c_w_in[o].astype(BF16))
            km = kmean.reshape(bsz, s // C_BLOCK, C_HEADS, HEAD_DIM).transpose(0, 2, 1, 3)
            y = _moba_attention(q, k, vt, km)
            x = post([y], c_w_out[o])
    return x
```

```python
import functools

import jax
import jax.numpy as jnp
from jax import lax
from jax.experimental import pallas as pl
from jax.experimental.pallas import tpu as pltpu

HEAD_DIM = 64
ROPE_THETA = 10000.0
DEPTH = 2
A_Q_HEADS = 8
A_KV_HEADS = 2
A_BLOCK = 128
B_Q_HEADS = 8
B_IDX_HEADS = 8
B_TOPK = 256
C_HEADS = 16
C_BLOCK = 256
C_TOPK = 3
DN_ALPHA = (2 * DEPTH) ** 0.25
LN_EPS = 1e-5
RMS_EPS = 1e-6
AB_IN_WIDTH = 1224
AB_IN_PAD = 1280
LOG2E = 1.4426950408889634
ATT_SCALE = HEAD_DIM ** -0.5 * LOG2E
IDX_SCALE = B_IDX_HEADS ** -0.5 * HEAD_DIM ** -0.5

F32 = jnp.float32
BF16 = jnp.bfloat16
MASK_NEG = -1e30
NEG_INF = float("-inf")
F32_LOWEST = -3.4028234663852886e38
INT_MIN = -(2 ** 31)
HALF = 2 ** 15
VMEM_LIMIT = 48 * 1024 * 1024

_NT = (((1,), (1,)), ((), ()))


def _params(sem):
    return pltpu.CompilerParams(dimension_semantics=sem, vmem_limit_bytes=VMEM_LIMIT)


def _rope(t, cos, sin_a, sin_b):
    outs = []
    for k in range(t.shape[-1] // 128):
        blk = t[:, 128 * k:128 * (k + 1)]
        outs.append(blk * cos + pltpu.roll(blk, 32, 1) * sin_a + pltpu.roll(blk, 96, 1) * sin_b)
    return outs[0] if len(outs) == 1 else jnp.concatenate(outs, axis=-1)


def _rope_coeffs(tab_t):
    tab = tab_t.T
    low = (lax.broadcasted_iota(jnp.int32, tab.shape, 1) & (HEAD_DIM - 1)) < HEAD_DIM // 2
    swapped = pltpu.roll(tab, HEAD_DIM // 2, 1)
    cos = jnp.where(low, tab, swapped)
    sin = jnp.where(low, swapped, tab)
    return cos, jnp.where(low, 0.0, sin), jnp.where(low, -sin, 0.0)


def _layer_norm(z, g, b):
    mu = jnp.mean(z, axis=-1, keepdims=True)
    zc = z - mu
    var = jnp.mean(zc * zc, axis=-1, keepdims=True)
    return zc * lax.rsqrt(var + LN_EPS) * g + b


def _ada_kernel(c_ref, w_ref, b_ref, o_ref):
    c = c_ref[...]
    sc = c / (1.0 + jnp.exp(-c))
    o_ref[0] = jnp.dot(sc, w_ref[0], preferred_element_type=F32) + b_ref[0]


def _ada_modulation(c, ada_w, ada_b):
    n = ada_w.shape[0] * ada_w.shape[1]
    bsz, d = c.shape
    w = ada_w.reshape(n, d, 3 * d)
    b = ada_b.reshape(n, 1, 3 * d)
    tn = 3 * d
    return pl.pallas_call(
        _ada_kernel,
        grid=(n, 3 * d // tn),
        in_specs=[
            pl.BlockSpec((bsz, d), lambda i, j: (0, 0)),
            pl.BlockSpec((1, d, tn), lambda i, j: (i, 0, j)),
            pl.BlockSpec((1, 1, tn), lambda i, j: (i, 0, j)),
        ],
        out_specs=pl.BlockSpec((1, bsz, tn), lambda i, j: (i, 0, j)),
        out_shape=jax.ShapeDtypeStruct((n, bsz, 3 * d), F32),
        compiler_params=_params(("arbitrary", "arbitrary")),
        name="ada_modulation",
    )(c, w, b)


PROJ_AB_GROUPS = 2


def _proj_ab_kernel(x_ref, shift_ref, scale_ref, tab_ref, win_ref, qn_ref, wuq_ref, wuiq_ref,
                    aq_ref, ak_ref, avt_ref, bq_ref, biq_ref, misc_ref, bvt_ref, iwt_ref):
    tm = x_ref.shape[1]
    rows_of = [slice(g * tm // PROJ_AB_GROUPS, (g + 1) * tm // PROJ_AB_GROUPS) for g in range(PROJ_AB_GROUPS)]
    projs = []
    for rows in rows_of:
        h = x_ref[0, rows, :] * (1.0 + scale_ref[0]) + shift_ref[0]
        projs.append(jnp.dot(h.astype(BF16), win_ref[...], preferred_element_type=F32))
    lows = []
    for proj in projs:
        cq = proj[:, 768:1024]
        ms = jnp.mean(cq * cq, axis=-1, keepdims=True)
        cqn = (cq * lax.rsqrt(ms + RMS_EPS) * qn_ref[...]).astype(BF16)
        lows.append((jnp.dot(cqn, wuq_ref[...], preferred_element_type=F32),
                     jnp.dot(cqn, wuiq_ref[...], preferred_element_type=F32)))

    for rows, proj, (bq, biq) in zip(rows_of, projs, lows):
        cos, sa, sb = _rope_coeffs(tab_ref[0, :, rows])
        aq_ref[0, rows, :] = (_rope(proj[:, 0:512], cos, sa, sb) * ATT_SCALE).astype(BF16)
        ak_ref[0, rows, :] = _rope(proj[:, 512:640], cos, sa, sb).astype(BF16)
        avt_ref[0, :, rows] = proj[:, 640:768].T.astype(BF16)
        bq_ref[0, rows, :] = (_rope(bq, cos, sa, sb) * ATT_SCALE).astype(BF16)
        biq_ref[0, rows, :] = _rope(biq, cos, sa, sb).astype(BF16)

        lane = lax.broadcasted_iota(jnp.int32, cos.shape, 1)
        roped = lane < HEAD_DIM
        c0 = jnp.where(roped, cos, 1.0)
        a0 = jnp.where(roped, sa, 0.0)
        b0 = jnp.where(roped, sb, 0.0)
        m0 = _rope(proj[:, 1024:1152], c0, a0, b0)
        m1 = _rope(proj[:, 1152:1280], c0, a0, b0)
        misc_ref[0, rows, :] = jnp.concatenate([m0, m1], axis=-1).astype(BF16)
        bvt_ref[0, :, rows] = m0.T[HEAD_DIM:2 * HEAD_DIM].astype(BF16)
        iwt_ref[0, :, rows] = m1.T[HEAD_DIM:HEAD_DIM + B_IDX_HEADS] * IDX_SCALE


def _proj_ab(x, shift, scale, rope_tab, w_in, q_norm, w_uq, w_uiq, tm=512):
    bsz, s, d = x.shape
    row = lambda w: pl.BlockSpec((1, tm, w), lambda b, i: (b, i, 0))
    vec = pl.BlockSpec((1, 1, d), lambda b, i: (b, 0, 0))
    full = lambda a: pl.BlockSpec(a.shape, lambda b, i: (0,) * a.ndim)
    col = lambda r: pl.BlockSpec((1, r, tm), lambda b, i: (b, 0, i))
    tok = lambda w: jax.ShapeDtypeStruct((bsz, s, w), BF16)
    kvw = A_KV_HEADS * HEAD_DIM
    out_shape = [tok(512), tok(kvw), jax.ShapeDtypeStruct((bsz, kvw, s), BF16), tok(512), tok(512), tok(256),
                 jax.ShapeDtypeStruct((bsz, HEAD_DIM, s), BF16), jax.ShapeDtypeStruct((bsz, B_IDX_HEADS, s), F32)]
    out_specs = [row(512), row(kvw), col(kvw), row(512), row(512), row(256), col(HEAD_DIM), col(B_IDX_HEADS)]
    return pl.pallas_call(
        _proj_ab_kernel,
        grid=(bsz, s // tm),
        in_specs=[row(d), vec, vec, col(128), full(w_in), full(q_norm), full(w_uq), full(w_uiq)],
        out_specs=out_specs,
        out_shape=out_shape,
        compiler_params=_params(("parallel", "parallel")),
        name="proj_ab",
    )(x, shift, scale, rope_tab, w_in, q_norm, w_uq, w_uiq)


SWA_BLOCKS = 16
SWA_TQ = SWA_BLOCKS * A_BLOCK


def _swa_kernel(sink_ref, q_ref, kp_ref, kc_ref, vtp_ref, vtc_ref, o_ref):
    i = pl.program_id(1)
    group = A_Q_HEADS // A_KV_HEADS
    kband = jnp.concatenate([kp_ref[0], kc_ref[0]], axis=0)
    vtband = jnp.concatenate([vtp_ref[0], vtc_ref[0]], axis=1)
    c = lax.broadcasted_iota(jnp.int32, (2 * A_BLOCK, A_BLOCK), 0)
    qi = lax.broadcasted_iota(jnp.int32, (2 * A_BLOCK, A_BLOCK), 1)
    in_window = (c > qi) & (c <= qi + A_BLOCK)
    later = jnp.where(in_window, 0.0, MASK_NEG)
    masks = [jnp.where(in_window & ((c >= A_BLOCK) | (i > 0)), 0.0, MASK_NEG)] + [later] * (SWA_BLOCKS - 1)

    ss = {}
    for t in range(SWA_BLOCKS):
        for kh in range(A_KV_HEADS):
            kb = kband[t * A_BLOCK:(t + 2) * A_BLOCK, kh * HEAD_DIM:(kh + 1) * HEAD_DIM]
            qstack = jnp.concatenate(
                [q_ref[0, t * A_BLOCK:(t + 1) * A_BLOCK, hq * HEAD_DIM:(hq + 1) * HEAD_DIM]
                 for hq in range(kh * group, (kh + 1) * group)], axis=0)
            s4 = lax.dot_general(kb, qstack, _NT, preferred_element_type=F32)
            for g in range(group):
                ss[t, kh * group + g] = s4[:, g * A_BLOCK:(g + 1) * A_BLOCK] + masks[t]
    ps, dens = {}, {}
    for t in range(SWA_BLOCKS):
        for hq in range(A_Q_HEADS):
            sink = sink_ref[hq] * LOG2E
            m = jnp.maximum(jnp.max(ss[t, hq], axis=0, keepdims=True), sink)
            p = jnp.exp2(ss[t, hq] - m)
            dens[t, hq] = jnp.sum(p, axis=0, keepdims=True) + jnp.exp2(sink - m)
            ps[t, hq] = p.astype(BF16)
    for t in range(SWA_BLOCKS):
        outs = []
        for hq in range(A_Q_HEADS):
            kh = hq // group
            vt = vtband[kh * HEAD_DIM:(kh + 1) * HEAD_DIM, t * A_BLOCK:(t + 2) * A_BLOCK]
            outs.append(jnp.dot(vt, ps[t, hq], preferred_element_type=F32) / dens[t, hq])
        o_ref[0, t * A_BLOCK:(t + 1) * A_BLOCK, :] = jnp.concatenate(outs, axis=0).T.astype(BF16)


def _swa_attention(sinks, aq, ak, avt):
    bsz, s, qw = aq.shape
    kvw = A_KV_HEADS * HEAD_DIM
    prev_blk = lambda i: jnp.maximum(SWA_BLOCKS * i - 1, 0)
    qspec = pl.BlockSpec((1, SWA_TQ, qw), lambda b, i: (b, i, 0))
    return pl.pallas_call(
        _swa_kernel,
        grid=(bsz, s // SWA_TQ),
        in_specs=[pl.BlockSpec(memory_space=pltpu.SMEM), qspec,
                  pl.BlockSpec((1, A_BLOCK, kvw), lambda b, i: (b, prev_blk(i), 0)),
                  pl.BlockSpec((1, SWA_TQ, kvw), lambda b, i: (b, i, 0)),
                  pl.BlockSpec((1, kvw, A_BLOCK), lambda b, i: (b, 0, prev_blk(i))),
                  pl.BlockSpec((1, kvw, SWA_TQ), lambda b, i: (b, 0, i))],
        out_specs=qspec,
        out_shape=jax.ShapeDtypeStruct(aq.shape, BF16),
        compiler_params=_params(("parallel", "parallel")),
        name="swa_attention",
    )(sinks, aq, ak, ak, avt, avt)


DSA_TQ = 256
DSA_KC = 256


def _dsa_kernel(*refs):
    npairs = (pl.program_id(1) + 2) // 2
    for n in range(1, refs[-5].shape[0] // 2 + 1):
        pl.when(npairs == n)(functools.partial(_dsa_body, n, *refs))


def _dsa_body(npairs, q_ref, iq_ref, iwt_ref, misc_ref, vt_ref, o_ref, sc_scr, hi_scr, lo_scr, st_scr, acc_scr):
    i = pl.program_id(1)
    tq, kc = DSA_TQ, DSA_KC
    nkc = i + 1
    krow = lax.broadcasted_iota(jnp.int32, (kc, tq), 0)
    qcol = lax.broadcasted_iota(jnp.int32, (kc, tq), 1)
    on_or_below_diag = krow <= qcol
    iwt = iwt_ref[0]
    hsl = [slice(h * HEAD_DIM, (h + 1) * HEAD_DIM) for h in range(B_Q_HEADS)]
    iqs = [iq_ref[0, :, hsl[h]] for h in range(B_IDX_HEADS)]
    qs = [q_ref[0, :, hsl[h]] for h in range(B_Q_HEADS)]


    def ordered(bits):
        return jnp.where(bits >= 0, bits, bits ^ jnp.int32(0x7FFFFFFF))

    def key_to_float(key):
        return jnp.where(key == INT_MIN, NEG_INF, lax.bitcast_convert_type(ordered(key), F32))

    def score_chunk(c):
        ik = misc_ref[0, pl.ds(c * kc, kc), 128:192]
        sc = jnp.zeros((kc, tq), F32)
        for h in range(B_IDX_HEADS):
            raw = lax.dot_general(ik, iqs[h], _NT, preferred_element_type=F32)
            sc = sc + jnp.maximum(raw, 0.0) * iwt[h:h + 1, :]
        sc = jnp.where(sc == 0.0, 0.0, sc)
        sc = jnp.where(c < i, sc, jnp.where(c == i, jnp.where(on_or_below_diag, sc, NEG_INF), NEG_INF))
        sc_scr[c] = sc
        key = ordered(lax.bitcast_convert_type(sc, jnp.int32))
        hi_scr[c] = (key >> 16).astype(jnp.int16)
        lo_scr[c] = ((key & 0xFFFF) - HALF).astype(jnp.int16)

    for c in range(2 * npairs):
        score_chunk(c)

    def count_ge(ref, c, cand):
        ones = jnp.where(ref[c] >= cand.astype(jnp.int16), jnp.int16(1), jnp.int16(0))
        part = ones[0:16]
        for r in range(1, kc // 16):
            part = part + ones[16 * r:16 * (r + 1)]
        return jnp.sum(part.astype(jnp.int32), axis=0, keepdims=True)

    def half_search(ref, wanted):
        def step(it, thr):
            cand = thr + lax.shift_left(jnp.int32(1), 15 - it)
            cnt = count_ge(ref, 0, cand)
            for c in range(1, 2 * npairs):
                cnt = cnt + count_ge(ref, c, cand)
            return jnp.where(cnt >= wanted, cand, thr)

        return lax.fori_loop(0, 16, step, jnp.full((1, tq), -HALF, jnp.int32))

    thr_hi = half_search(hi_scr, B_TOPK)

    def mask_low(c, above):
        hi = hi_scr[c].astype(jnp.int32)
        lo_scr[c] = jnp.where(hi == thr_hi, lo_scr[c].astype(jnp.int32), -HALF).astype(jnp.int16)
        return above + jnp.sum(jnp.where(hi > thr_hi, 1, 0), axis=0, keepdims=True)

    above = lax.fori_loop(0, nkc, mask_low, jnp.zeros((1, tq), jnp.int32))
    thr_lo = half_search(lo_scr, B_TOPK - above)
    thr_guess = key_to_float((thr_hi << 16) | (thr_lo + HALF))

    zero_row = jnp.zeros((1, tq), F32)

    def count_gt_eq(t):
        def body(c, carry):
            ngt, neq = carry
            k = sc_scr[c]
            return (ngt + jnp.sum(jnp.where(k > t, 1.0, 0.0), axis=0, keepdims=True),
                    neq + jnp.sum(jnp.where(k == t, 1.0, 0.0), axis=0, keepdims=True))

        return lax.fori_loop(0, nkc, body, (zero_row, zero_row))

    def keep(t, ngt, neq):
        st_scr[0:1, :] = t
        st_scr[1:2, :] = ngt
        st_scr[2:3, :] = neq

    ngt, neq = count_gt_eq(thr_guess)
    keep(thr_guess, ngt, neq)
    is_kth = jnp.where(ngt < float(B_TOPK), jnp.where(ngt + neq >= float(B_TOPK), 1.0, 0.0), 0.0)

    @pl.when(jnp.min(is_kth) < 1.0)
    def _():
        def step(it, key):
            cand = key + lax.shift_left(jnp.int32(1), 31 - it)
            cand_f = lax.bitcast_convert_type(ordered(cand), F32)
            cnt = lax.fori_loop(
                0, nkc, lambda c, acc: acc + jnp.sum(jnp.where(sc_scr[c] >= cand_f, 1.0, 0.0), axis=0,
                                                      keepdims=True), zero_row)
            return jnp.where(cnt >= float(B_TOPK), cand, key)

        t = key_to_float(lax.fori_loop(0, 32, step, jnp.full((1, tq), INT_MIN, jnp.int32)))
        keep(t, *count_gt_eq(t))

    thr = st_scr[0:1, :]
    need = float(B_TOPK) - st_scr[1:2, :]

    @pl.when(jnp.max(st_scr[2:3, :] - need) > 0.0)
    def _():
        ra = lax.broadcasted_iota(jnp.int32, (kc, kc), 0)
        rb = lax.broadcasted_iota(jnp.int32, (kc, kc), 1)
        lower = jnp.where(rb < ra, 1.0, 0.0).astype(BF16)

        def drop_late_ties(c, before):
            k = sc_scr[c]
            eq = jnp.where(k == thr, 1.0, 0.0)
            prefix = jnp.dot(lower, eq.astype(BF16), preferred_element_type=F32) + before
            sc_scr[c] = jnp.where(k == thr, jnp.where(prefix >= need, NEG_INF, k), k)
            return before + jnp.sum(eq, axis=0, keepdims=True)

        lax.fori_loop(0, nkc, drop_late_ties, zero_row)

    thr_sel = jnp.maximum(thr, F32_LOWEST)

    acc_scr[...] = jnp.zeros_like(acc_scr)

    def attend(p, carry):
        ms, ls = carry
        kk = misc_ref[0, pl.ds(p * 2 * kc, 2 * kc), 0:64]
        vt = vt_ref[0, :, pl.ds(p * 2 * kc, 2 * kc)]
        sel = jnp.concatenate([sc_scr[2 * p], sc_scr[2 * p + 1]], axis=0) >= thr_sel
        sel_bias = jnp.where(sel, 0.0, MASK_NEG)
        ss = [lax.dot_general(kk, qs[h], _NT, preferred_element_type=F32) + sel_bias
              for h in range(B_Q_HEADS)]
        ms_new, ls_new, alphas, ps = [], [], [], []
        for h in range(B_Q_HEADS):
            m_new = jnp.maximum(ms[h], jnp.max(ss[h], axis=0, keepdims=True))
            alpha = jnp.exp2(ms[h] - m_new)
            p = jnp.exp2(ss[h] - m_new)
            ls_new.append(alpha * ls[h] + jnp.sum(p, axis=0, keepdims=True))
            ms_new.append(m_new)
            alphas.append(alpha)
            ps.append(p.astype(BF16))
        for h in range(B_Q_HEADS):
            acc_scr[h] = alphas[h] * acc_scr[h] + jnp.dot(vt, ps[h], preferred_element_type=F32)
        return tuple(ms_new), tuple(ls_new)

    init = (tuple(jnp.full((1, tq), MASK_NEG, F32) for _ in range(B_Q_HEADS)),
            tuple(jnp.zeros((1, tq), F32) for _ in range(B_Q_HEADS)))
    carry = init
    for p in range(npairs):
        carry = attend(p, carry)
    ls = carry[1]

    outs = []
    for h in range(0, B_Q_HEADS, 2):
        pair = jnp.concatenate([acc_scr[h] / ls[h], acc_scr[h + 1] / ls[h + 1]], axis=0)
        outs.append(pair.T)
    o_ref[0] = jnp.concatenate(outs, axis=-1).astype(BF16)


def _dsa_attention(bq, biq, iwt, misc, vt):
    bsz, s, _ = bq.shape
    tq = DSA_TQ
    qspec = pl.BlockSpec((1, tq, B_Q_HEADS * HEAD_DIM), lambda b, i: (b, i, 0))
    return pl.pallas_call(
        _dsa_kernel,
        grid=(bsz, s // tq),
        in_specs=[qspec, qspec,
                  pl.BlockSpec((1, B_IDX_HEADS, tq), lambda b, i: (b, 0, i)),
                  pl.BlockSpec((1, s, misc.shape[2]), lambda b, i: (b, 0, 0)),
                  pl.BlockSpec((1, HEAD_DIM, s), lambda b, i: (b, 0, 0))],
        out_specs=qspec,
        out_shape=jax.ShapeDtypeStruct(bq.shape, BF16),
        scratch_shapes=[pltpu.VMEM((s // DSA_KC, DSA_KC, tq), F32),
                        pltpu.VMEM((s // DSA_KC, DSA_KC, tq), jnp.int16),
                        pltpu.VMEM((s // DSA_KC, DSA_KC, tq), jnp.int16),
                        pltpu.VMEM((8, tq), F32),
                        pltpu.VMEM((B_Q_HEADS, HEAD_DIM, tq), F32)],
        compiler_params=_params(("parallel", "arbitrary")),
        name="dsa_attention",
    )(bq, biq, iwt, misc, vt)


POST_TM = 512
POST_ROWS = 256
POST_TF = 1024
POST_VMEM_LIMIT = 56 * 1024 * 1024


def _post_attn_kernel(*refs, widths):
    n = len(widths)
    parts = refs[:n]
    (wout_ref, x_ref, gate1_ref, g1_ref, b1_ref, shift2_ref, scale2_ref, gate2_ref,
     w1_ref, w2_ref, g2_ref, b2_ref, o_ref) = refs[n:]
    groups = [slice(r * POST_ROWS, (r + 1) * POST_ROWS) for r in range(x_ref.shape[1] // POST_ROWS)]

    def out_proj(rows):
        y = None
        off = 0
        for p_ref, wd in zip(parts, widths):
            t = jnp.dot(p_ref[0, rows, :], wout_ref[off:off + wd, :], preferred_element_type=F32)
            y = t if y is None else y + t
            off += wd
        return y

    def norm1(rows, y):
        x1 = _layer_norm(DN_ALPHA * x_ref[0, rows, :] + gate1_ref[0] * y, g1_ref[...], b1_ref[...])
        return x1, (x1 * (1.0 + scale2_ref[0]) + shift2_ref[0]).astype(BF16)

    def mlp(h):
        acc = None
        for f in range(w1_ref.shape[1] // POST_TF):
            cols = slice(f * POST_TF, (f + 1) * POST_TF)
            u = jnp.maximum(jnp.dot(h, w1_ref[:, cols], preferred_element_type=F32), 0.0)
            t = jnp.dot((u * u).astype(BF16), w2_ref[cols, :], preferred_element_type=F32)
            acc = t if acc is None else acc + t
        return acc

    ys = [out_proj(rows) for rows in groups]
    x1s, accs = [], []
    for rows, y in zip(groups, ys):
        x1, h = norm1(rows, y)
        x1s.append(x1)
        accs.append(mlp(h))
    for rows, x1, acc in zip(groups, x1s, accs):
        o_ref[0, rows, :] = _layer_norm(DN_ALPHA * x1 + gate2_ref[0] * acc, g2_ref[...], b2_ref[...])


def _post_attn(parts, w_out, x, gate1, g1, b1, shift2, scale2, gate2, w1, w2, g2, b2):
    bsz, s, d = x.shape
    tm = POST_TM
    widths = tuple(p.shape[-1] for p in parts)
    row = lambda w: pl.BlockSpec((1, tm, w), lambda bi, i: (bi, i, 0))
    vec = pl.BlockSpec((1, 1, d), lambda bi, i: (bi, 0, 0))
    full = lambda a: pl.BlockSpec(a.shape, lambda bi, i: (0,) * a.ndim, pipeline_mode=pl.Buffered(1))
    return pl.pallas_call(
        functools.partial(_post_attn_kernel, widths=widths),
        grid=(bsz, s // tm),
        in_specs=([row(w) for w in widths]
                  + [full(w_out), row(d), vec, full(g1), full(b1), vec, vec, vec,
                     full(w1), full(w2), full(g2), full(b2)]),
        out_specs=row(d),
        out_shape=jax.ShapeDtypeStruct(x.shape, F32),
        compiler_params=pltpu.CompilerParams(dimension_semantics=("parallel", "parallel"),
                                             vmem_limit_bytes=POST_VMEM_LIMIT),
        name="post_attn",
    )(*parts, w_out, x, gate1, g1, b1, shift2, scale2, gate2, w1, w2, g2, b2)


def _proj_c_kernel(x_ref, shift_ref, scale_ref, tab_ref, win_ref, q_ref, k_ref, vt_ref, km_ref):
    h = x_ref[0] * (1.0 + scale_ref[0]) + shift_ref[0]
    proj = jnp.dot(h.astype(BF16), win_ref[...], preferred_element_type=F32)
    cw = C_HEADS * HEAD_DIM
    cos, sa, sb = _rope_coeffs(tab_ref[0])
    q_ref[0] = (_rope(proj[:, 0:cw], cos, sa, sb) * ATT_SCALE).astype(BF16)
    k = _rope(proj[:, cw:2 * cw], cos, sa, sb)
    k_ref[0] = k.astype(BF16)
    vt_ref[0] = proj[:, 2 * cw:3 * cw].T.astype(BF16)
    for t in range(k.shape[0] // C_BLOCK):
        km_ref[0, t] = jnp.mean(k[t * C_BLOCK:(t + 1) * C_BLOCK], axis=0, keepdims=True)


PROJ_C_TM = 2 * C_BLOCK


def _proj_c(x, shift, scale, rope_tab, w_in):
    bsz, s, d = x.shape
    tm = PROJ_C_TM
    nbt = tm // C_BLOCK
    cw = C_HEADS * HEAD_DIM
    row = lambda w: pl.BlockSpec((1, tm, w), lambda b, i: (b, i, 0))
    vec = pl.BlockSpec((1, 1, d), lambda b, i: (b, 0, 0))
    full = lambda a: pl.BlockSpec(a.shape, lambda b, i: (0,) * a.ndim)
    qkv = jax.ShapeDtypeStruct((bsz, s, cw), BF16)
    return pl.pallas_call(
        _proj_c_kernel,
        grid=(bsz, s // tm),
        in_specs=[row(d), vec, vec, pl.BlockSpec((1, 128, tm), lambda b, i: (b, 0, i)), full(w_in)],
        out_specs=[row(cw), row(cw), pl.BlockSpec((1, cw, tm), lambda b, i: (b, 0, i)),
                   pl.BlockSpec((1, nbt, 1, cw), lambda b, i: (b, i, 0, 0))],
        out_shape=[qkv, qkv, jax.ShapeDtypeStruct((bsz, cw, s), BF16),
                   jax.ShapeDtypeStruct((bsz, s // C_BLOCK, 1, cw), F32)],
        compiler_params=_params(("parallel", "parallel")),
        name="proj_c",
    )(x, shift, scale, rope_tab, w_in)


MOBA_HEADS_PER_STEP = 16


def _moba_kernel(q_ref, k_ref, vt_ref, km_ref, o_ref, bias_scr, acc_scr):
    i = pl.program_id(2)
    tq = q_ref.shape[1]
    nb = k_ref.shape[1] // C_BLOCK
    nh = MOBA_HEADS_PER_STEP
    hsl = [slice(hh * HEAD_DIM, (hh + 1) * HEAD_DIM) for hh in range(nh)]
    qs = [q_ref[0, :, hsl[hh]] for hh in range(nh)]

    n_idx = lax.broadcasted_iota(jnp.int32, (nb, tq), 0)
    for hh in range(nh):
        gate = lax.dot_general(km_ref[0, hh].astype(BF16), qs[hh], _NT, preferred_element_type=F32)
        cnt = jnp.zeros((nb, tq), F32)
        for m_idx in range(nb - 1):
            other = gate[m_idx:m_idx + 1, :]
            tie = jnp.where(n_idx > m_idx, 1.0, 0.0)
            beats = jnp.where(other > gate, 1.0, jnp.where(other == gate, tie, 0.0))
            cnt = cnt + jnp.where(m_idx < i, beats, 0.0)
        bias_scr[hh] = jnp.where(n_idx < i, jnp.where(cnt < float(C_TOPK), 0.0, MASK_NEG), MASK_NEG)

    def block_update(blk, nblk, mask, bias, ms, ls):
        rows = nblk * C_BLOCK
        k0 = pl.multiple_of(blk * C_BLOCK, C_BLOCK)
        ss = []
        for hh in range(nh):
            kj = k_ref[0, pl.ds(k0, rows), hsl[hh]]
            ss.append(mask(lax.dot_general(kj, qs[hh], _NT, preferred_element_type=F32)))
        ms_new, ls_new, alphas, ps = [], [], [], []
        for hh in range(nh):
            bs = [bias(hh, t) for t in range(nblk)]
            parts = [ss[hh][t * C_BLOCK:(t + 1) * C_BLOCK] for t in range(nblk)]
            m_new = ms[hh]
            for t in range(nblk):
                m_new = jnp.maximum(m_new, jnp.max(parts[t], axis=0, keepdims=True) + bs[t])
            alpha = jnp.exp2(ms[hh] - m_new)
            p = [jnp.exp2(parts[t] - (m_new - bs[t])) for t in range(nblk)]
            psum = sum(jnp.sum(pt, axis=0, keepdims=True) for pt in p)
            ls_new.append(alpha * ls[hh] + psum)
            ms_new.append(m_new)
            alphas.append(alpha)
            ps.append(jnp.concatenate(p, axis=0).astype(BF16) if nblk > 1 else p[0].astype(BF16))
        for hh in range(nh):
            vj = vt_ref[0, hsl[hh], pl.ds(k0, rows)]
            acc_scr[hh] = alphas[hh] * acc_scr[hh] + jnp.dot(vj, ps[hh], preferred_element_type=F32)
        return tuple(ms_new), tuple(ls_new)

    krow = lax.broadcasted_iota(jnp.int32, (C_BLOCK, tq), 0)
    qcol = lax.broadcasted_iota(jnp.int32, (C_BLOCK, tq), 1)
    causal = krow <= qcol
    acc_scr[...] = jnp.zeros_like(acc_scr)
    ms0 = tuple(jnp.full((1, tq), MASK_NEG, F32) for _ in range(nh))
    ls0 = tuple(jnp.zeros((1, tq), F32) for _ in range(nh))
    carry = block_update(i, 1, lambda s: jnp.where(causal, s, MASK_NEG), lambda hh, t: 0.0, ms0, ls0)
    past_bias = lambda first: (lambda hh, t: bias_scr[hh, pl.ds(first + t, 1), :])

    def past_pair(p, carry):
        return block_update(2 * p, 2, lambda s: s, past_bias(2 * p), carry[0], carry[1])

    carry = lax.fori_loop(0, i // 2, past_pair, carry)
    carry = lax.cond(i % 2 == 1,
                     lambda c: block_update(i - 1, 1, lambda s: s, past_bias(i - 1), c[0], c[1]),
                     lambda c: c, carry)
    ls = carry[1]

    outs = []
    for hh in range(0, nh, 2):
        pair = jnp.concatenate([acc_scr[hh] / ls[hh], acc_scr[hh + 1] / ls[hh + 1]], axis=0)
        outs.append(pair.T)
    o_ref[0] = jnp.concatenate(outs, axis=-1).astype(BF16)


def _moba_attention(q, k, vt, kmean):
    bsz, s, cw = q.shape
    tq = C_BLOCK
    nh = MOBA_HEADS_PER_STEP
    w = nh * HEAD_DIM
    nb = s // C_BLOCK
    qspec = pl.BlockSpec((1, tq, w), lambda b, hg, i: (b, i, hg))
    return pl.pallas_call(
        _moba_kernel,
        grid=(bsz, cw // w, s // tq),
        in_specs=[qspec,
                  pl.BlockSpec((1, s, w), lambda b, hg, i: (b, 0, hg)),
                  pl.BlockSpec((1, w, s), lambda b, hg, i: (b, hg, 0)),
                  pl.BlockSpec((1, nh, nb, HEAD_DIM), lambda b, hg, i: (b, hg, 0, 0))],
        out_specs=qspec,
        out_shape=jax.ShapeDtypeStruct(q.shape, BF16),
        scratch_shapes=[pltpu.VMEM((nh, nb, tq), F32), pltpu.VMEM((nh, HEAD_DIM, tq), F32)],
        compiler_params=_params(("parallel", "parallel", "arbitrary")),
        name="moba_attention",
    )(q, k, vt, kmean)


def _rope_table(positions):
    inv = ROPE_THETA ** (-jnp.arange(0, HEAD_DIM, 2, dtype=F32) / HEAD_DIM)
    ang = positions.astype(F32)[:, None, :] * inv[None, :, None]
    cos, sin = lax.optimization_barrier((jnp.cos(ang), jnp.sin(ang)))
    return jnp.concatenate([cos, sin, cos, sin], axis=1)


def kernel(x, c, positions, ab_w_in, ab_q_norm, ab_w_uq, ab_w_uiq, ab_sinks, ab_w_out, c_w_in, c_w_out,
           ada_w, ada_b, ln_g, ln_b, mlp_w1, mlp_w2):
    bsz, s, d = x.shape
    rope_tab = _rope_table(positions)
    mod = _ada_modulation(c, ada_w, ada_b)

    def mods(idx):
        m = mod[idx]
        return m[:, None, 0:d], m[:, None, d:2 * d], m[:, None, 2 * d:3 * d] + 1.0

    for layer in range(DEPTH):
        shift, scale, gate = mods(2 * layer)
        shift2, scale2, gate2 = mods(2 * layer + 1)
        post = lambda parts, w_out: _post_attn(
            parts, w_out.astype(BF16), x, gate, ln_g[layer, 0][None], ln_b[layer, 0][None],
            shift2, scale2, gate2, mlp_w1[layer].astype(BF16), mlp_w2[layer].astype(BF16),
            ln_g[layer, 1][None], ln_b[layer, 1][None])
        if layer % 2 == 0:
            e = layer // 2
            w_in = jnp.pad(ab_w_in[e], ((0, 0), (0, AB_IN_PAD - AB_IN_WIDTH))).astype(BF16)
            aq, ak, avt, bq, biq, misc, bvt, iwt = _proj_ab(
                x, shift, scale, rope_tab, w_in, ab_q_norm[e][None],
                ab_w_uq[e].astype(BF16), ab_w_uiq[e].astype(BF16))
            ya = _swa_attention(ab_sinks[e], aq, ak, avt)
            yb = _dsa_attention(bq, biq, iwt, misc, bvt)
            x = post([ya, yb], ab_w_out[e])
        else:
            o = layer // 2
            q, k, vt, kmean = _proj_c(x, shift, scale, rope_tab, c_w_in[o].astype(BF16))
            km = kmean.reshape(bsz, s // C_BLOCK, C_HEADS, HEAD_DIM).transpose(0, 2, 1, 3)
            y = _moba_attention(q, k, vt, km)
            x = post([y], c_w_out[o])
    return x
```
